```python
import math
import jax, jax.numpy as jnp
from jax import lax
import numpy as np

D_MODEL = 1024
BATCH = 4
SEQ = 4096
DEPTH = 2

GRID_W = 64
CTX_LEN = 256

DA_HEADS = 4
DA_QK = 32
DA_V = 64
GLA_HEADS = 6
GLA_DK = 32
GLA_DV = 64
GLA_RANK = 16
GLA_TAU = 16.0
GLA_CHUNK = 64
RW_HEADS = 6
RW_D = 64
RW_DECAY_RANK = 64
RW_A_RANK = 64
RW_GATE_RANK = 128
RW_GN_EPS = 64e-5
N_EXPERTS = 32
TOP_K = 4
D_EXPERT = 1024
SWIGLU_LIMIT = 7.0
SWIGLU_ALPHA = 1.702
MOE_BLOCK = 256
ROPE_BASE = 10000.0
Q_BLOCK = 128
LN_EPS = 1e-5
DEEPNORM_ALPHA = (2 * DEPTH) ** 0.25
DEEPNORM_BETA = (8 * DEPTH) ** -0.25

DA_W = DA_HEADS * DA_V
GLA_W = GLA_HEADS * GLA_DV
RW_W = RW_HEADS * RW_D
MIX_W = DA_W + GLA_W + RW_W

IN_SPLITS = (
    ("da_q", DA_HEADS * 2 * DA_QK),
    ("da_k", DA_HEADS * 2 * DA_QK),
    ("da_v", DA_HEADS * DA_V),
    ("gla_q", GLA_HEADS * GLA_DK),
    ("gla_k", GLA_HEADS * GLA_DK),
    ("gla_v", GLA_HEADS * GLA_DV),
    ("gla_gf", GLA_RANK),
    ("gla_gb", GLA_RANK),
    ("gla_r", GLA_HEADS * GLA_DV),
    ("rw_rkv", 3 * RW_W),
    ("rw_wf", RW_DECAY_RANK),
    ("rw_wb", RW_DECAY_RANK),
    ("rw_af", RW_A_RANK),
    ("rw_ab", RW_A_RANK),
    ("rw_g", RW_GATE_RANK),
)
D_IN = sum(n for _, n in IN_SPLITS)

kernel_name = "hybrid_diffattn_gla_rwkv7_moe_dit"


def _layer_norm(x, g, b, eps=LN_EPS):
    xf = x.astype(jnp.float32)
    mu = xf.mean(-1, keepdims=True)
    var = jnp.square(xf - mu).mean(-1, keepdims=True)
    return ((xf - mu) * lax.rsqrt(var + eps)).astype(x.dtype) * g + b


def _rms_norm(x, g, eps=LN_EPS):
    xf = x.astype(jnp.float32)
    return (xf * lax.rsqrt(jnp.mean(xf * xf, -1, keepdims=True) + eps)).astype(x.dtype) * g


def _split_in(z):
    idx = np.cumsum([n for _, n in IN_SPLITS])[:-1].tolist()
    return dict(zip([k for k, _ in IN_SPLITS], jnp.split(z, idx, axis=-1)))


def _axial_rope_tables(rows, dim):
    row = jnp.repeat(jnp.arange(rows), GRID_W).astype(jnp.float32)
    col = jnp.tile(jnp.arange(GRID_W), rows).astype(jnp.float32)
    quarter = dim // 4
    inv = ROPE_BASE ** (-jnp.arange(quarter, dtype=jnp.float32) / quarter)
    ang_r = row[:, None] * inv
    ang_c = col[:, None] * inv
    ang = jnp.concatenate([ang_r, ang_r, ang_c, ang_c], -1)
    return jnp.cos(ang), jnp.sin(ang)


def _apply_axial_rope(x, cos, sin):
    xr = x.reshape(x.shape[:-1] + (2, 2, x.shape[-1] // 4))
    rot = jnp.stack([-xr[..., 1, :], xr[..., 0, :]], axis=-2).reshape(x.shape)
    c = cos[None, :, None, None, :].astype(x.dtype)
    s = sin[None, :, None, None, :].astype(x.dtype)
    return x * c + rot * s


def _diff_softmax_attend(q, k, v, lam):
    s = jnp.einsum('bqhmd,bkhmd->bhmqk', q, k).astype(jnp.float32) * (DA_QK ** -0.5)
    p = jax.nn.softmax(s, axis=-1)
    w = p[:, :, 0] - lam * p[:, :, 1]
    return jnp.einsum('bhqk,bkhe->bqhe', w.astype(v.dtype), v)


def _diff_attn_mixer(pl, pc, lam_q1, lam_k1, lam_q2, lam_k2, subln_g, layer_idx, cos, sin, need_ctx_out):
    dtype = pl["da_v"].dtype
    lam_init = 0.8 - 0.6 * math.exp(-0.3 * layer_idx)
    f32 = jnp.float32
    lam = (jnp.exp(jnp.sum(lam_q1.astype(f32) * lam_k1.astype(f32)))
           - jnp.exp(jnp.sum(lam_q2.astype(f32) * lam_k2.astype(f32))) + lam_init)

    def heads(p):
        B, L, _ = p["da_q"].shape
        return (p["da_q"].reshape(B, L, DA_HEADS, 2, DA_QK),
                p["da_k"].reshape(B, L, DA_HEADS, 2, DA_QK),
                p["da_v"].reshape(B, L, DA_HEADS, DA_V))

    def post(o):
        o = _rms_norm(o, subln_g) * (1.0 - lam_init)
        return o.reshape(o.shape[:2] + (DA_W,)).astype(dtype)

    q, k, v = heads(pl)
    qc, kc, vc = heads(pc)
    q = _apply_axial_rope(q, cos, sin)
    k = _apply_axial_rope(k, cos, sin)
    k_all = jnp.concatenate([kc, k], axis=1)
    v_all = jnp.concatenate([vc, v], axis=1)
    B, S = q.shape[:2]
    nb = S // Q_BLOCK
    qb = q.reshape((B, nb, Q_BLOCK) + q.shape[2:]).swapaxes(0, 1)
    ob = lax.map(lambda blk: _diff_softmax_attend(blk, k_all, v_all, lam), qb)
    out_l = post(ob.swapaxes(0, 1).reshape(B, S, DA_HEADS, DA_V))
    out_c = post(_diff_softmax_attend(qc, kc, vc, lam)) if need_ctx_out else None
    return out_l, out_c


def _bidirectional(scan_fn, ctx_f, lat_f, ctx_b, lat_b, s0, axis, emit_ctx):
    flip = lambda ts: tuple(jnp.flip(t, axis) for t in ts)
    s_f, yc_f = scan_fn(*ctx_f, s0, emit_ctx)
    _, y_f = scan_fn(*lat_f, s_f, True)
    s_b, yc_b = scan_fn(*flip(ctx_b), s0, emit_ctx)
    _, y_b = scan_fn(*flip(lat_b), s_b, True)
    y_lat = y_f + jnp.flip(y_b, axis)
    y_ctx = (yc_f + jnp.flip(yc_b, axis)) if emit_ctx else None
    return y_lat, y_ctx


def _gla_chunk_scan(q, k, v, g, s0, emit):
    B, H, L, _ = q.shape
    dv = v.shape[-1]
    n = L // GLA_CHUNK
    ch = lambda t: t.reshape(B, H, n, GLA_CHUNK, t.shape[-1]).transpose(2, 0, 1, 3, 4)
    lower = jnp.tril(jnp.ones((GLA_CHUNK, GLA_CHUNK), bool))[:, :, None]

    def step(S, xs):
        qc, kc, vc, gc = xs
        b = jnp.cumsum(gc, axis=-2)
        b_last = b[..., -1:, :]
        S_new = (S * jnp.exp(b_last)[..., 0, :, None]
                 + jnp.einsum('bhsk,bhsv->bhkv', kc * jnp.exp(b_last - b), vc))
        if not emit:
            return S_new, None
        inter = jnp.einsum('bhtk,bhkv->bhtv', qc * jnp.exp(b), S)
        dec = jnp.exp(jnp.where(lower, b[..., :, None, :] - b[..., None, :, :], -jnp.inf))
        att = jnp.einsum('bhtk,bhsk,bhtsk->bhts', qc, kc, dec)
        return S_new, inter + jnp.einsum('bhts,bhsv->bhtv', att, vc)

    S_fin, o = lax.scan(step, s0, (ch(q), ch(k), ch(v), ch(g)))
    if emit:
        o = o.transpose(1, 2, 0, 3, 4).reshape(B, H, L, dv)
    return S_fin, o


def _gla_mixer(pl, pc, gate_w2, gate_b, norm_g, need_ctx_out):
    dtype = pl["gla_r"].dtype
    f32 = jnp.float32

    def prep(p):
        B, L, _ = p["gla_q"].shape
        hd = lambda t, d: t.astype(f32).reshape(B, L, GLA_HEADS, d).transpose(0, 2, 1, 3)
        q = hd(p["gla_q"], GLA_DK) * (GLA_DK ** -0.5)
        k = hd(p["gla_k"], GLA_DK)
        v = hd(p["gla_v"], GLA_DV)
        gs = [hd(jax.nn.log_sigmoid((p[nm] @ gate_w2[i] + gate_b[i]).astype(f32)) / GLA_TAU, GLA_DK)
              for i, nm in enumerate(("gla_gf", "gla_gb"))]
        return (q, k, v, gs[0]), (q, k, v, gs[1])

    lat_f, lat_b = prep(pl)
    ctx_f, ctx_b = prep(pc)
    B = lat_f[0].shape[0]
    s0 = jnp.zeros((B, GLA_HEADS, GLA_DK, GLA_DV), f32)
    o_l, o_c = _bidirectional(_gla_chunk_scan, ctx_f, lat_f, ctx_b, lat_b, s0, 2, need_ctx_out)

    def post(o, r):
        B_, H, L, _ = o.shape
        o = _rms_norm(o.transpose(0, 2, 1, 3), norm_g).reshape(B_, L, GLA_W)
        return (o * jax.nn.silu(r.astype(f32))).astype(dtype)

    out_l = post(o_l, pl["gla_r"])
    out_c = post(o_c, pc["gla_r"]) if need_ctx_out else None
    return out_l, out_c


def _centred_conv3(x, w):
    xp = jnp.pad(x, ((0, 0), (1, 1), (0, 0)))
    return xp[:, :-2] * w[0] + xp[:, 1:-1] * w[1] + xp[:, 2:] * w[2]


def _rwkv7_scan(r, decay, kk, a, k, v, s0, emit):
    tm = lambda t: jnp.moveaxis(t, 1, 0)

    def step(S, xs):
        r_t, w_t, kk_t, a_t, k_t, v_t = xs
        sa = jnp.einsum('bhvk,bhk->bhv', S, -kk_t)
        S = (S * w_t[:, :, None, :] + sa[..., None] * (kk_t * a_t)[:, :, None, :]
             + v_t[..., None] * k_t[:, :, None, :])
        return S, (jnp.einsum('bhvk,bhk->bhv', S, r_t) if emit else None)

    S, y = lax.scan(step, s0, tuple(tm(t) for t in (r, decay, kk, a, k, v)))
    return S, (jnp.moveaxis(y, 0, 1) if emit else None)


def _rwkv7_mixer(pl, pc, conv_w, w2, w0, a2, a0, g2, k_k, k_a, r_k, lnx_g, lnx_b, need_ctx_out):
    dtype = pl["rw_rkv"].dtype
    f32 = jnp.float32

    def prep(p):
        B, L, _ = p["rw_rkv"].shape
        r, k, v = jnp.split(_centred_conv3(p["rw_rkv"], conv_w), 3, axis=-1)
        hd = lambda t: t.astype(f32).reshape(B, L, RW_HEADS, RW_D)
        kk = hd(k * k_k)
        kk = kk / jnp.maximum(jnp.linalg.norm(kk, axis=-1, keepdims=True), 1e-12)
        r, v = hd(r), hd(v)
        dirs = []
        for i, (wn, an) in enumerate((("rw_wf", "rw_af"), ("rw_wb", "rw_ab"))):
            w_raw = (w0[i] + jnp.tanh(p[wn]) @ w2[i]).astype(f32)
            decay = jnp.exp(-jnp.exp(-jax.nn.softplus(-w_raw) - 0.5))
            a = jax.nn.sigmoid((a0[i] + p[an] @ a2[i]).astype(f32))
            k_mod = k.astype(f32) * (1.0 + (a - 1.0) * k_a)
            dirs.append((r, hd(decay), kk, hd(a), hd(k_mod), v))
        g = jax.nn.sigmoid(p["rw_g"]) @ g2
        return dirs, g

    (lat_f, lat_b), g_l = prep(pl)
    (ctx_f, ctx_b), g_c = prep(pc)
    B = g_l.shape[0]
    s0 = jnp.zeros((B, RW_HEADS, RW_D, RW_D), f32)
    y_l, y_c = _bidirectional(_rwkv7_scan, ctx_f, lat_f, ctx_b, lat_b, s0, 1, need_ctx_out)

    def post(y, df, db, g):
        r, _, _, _, kf, v = df
        kb = db[4]
        B_, L = y.shape[:2]
        yn = _layer_norm(y, lnx_g.reshape(RW_HEADS, RW_D), lnx_b.reshape(RW_HEADS, RW_D), eps=RW_GN_EPS)
        bonus = (jnp.sum(r * kf * r_k, -1, keepdims=True) + jnp.sum(r * kb * r_k, -1, keepdims=True)) * v
        return ((yn + bonus).reshape(B_, L, RW_W) * g).astype(dtype)

    out_l = post(y_l, lat_f, lat_b, g_l)
    out_c = post(y_c, ctx_f, ctx_b, g_c) if need_ctx_out else None
    return out_l, out_c


def _moe(h, router_w, router_b, w_gate, b_gate, w_up, b_up, w_down, b_down):
    T, D = h.shape
    logits = (h @ router_w + router_b).astype(jnp.float32)
    top_v, top_i = lax.top_k(logits, TOP_K)
    gates = jax.nn.softmax(top_v, axis=-1)
    n_assign = T * TOP_K
    flat_e = top_i.reshape(-1)
    order = jnp.argsort(flat_e)
    e_sorted = flat_e[order]
    tok_sorted = order // TOP_K
    counts = jnp.bincount(flat_e, length=N_EXPERTS)
    start = jnp.cumsum(counts) - counts
    padded = (counts + MOE_BLOCK - 1) // MOE_BLOCK * MOE_BLOCK
    pad_end = jnp.cumsum(padded)
    pad_start = pad_end - padded
    dest = pad_start[e_sorted] + jnp.arange(n_assign) - start[e_sorted]
    n_blocks = -(-n_assign // MOE_BLOCK) + N_EXPERTS
    cap = n_blocks * MOE_BLOCK
    slot_tok = jnp.full((cap,), T, jnp.int32).at[dest].set(tok_sorted.astype(jnp.int32))
    block_expert = jnp.minimum(
        jnp.searchsorted(pad_end, jnp.arange(n_blocks) * MOE_BLOCK, side='right'), N_EXPERTS - 1)
    h_pad = jnp.concatenate([h, jnp.zeros((1, D), h.dtype)], axis=0)
    xb = h_pad[slot_tok].reshape(n_blocks, MOE_BLOCK, D)

    def run_block(args):
        xs, e = args
        gt = jnp.minimum(xs @ w_gate[e] + b_gate[e], SWIGLU_LIMIT)
        up = jnp.clip(xs @ w_up[e] + b_up[e], -SWIGLU_LIMIT, SWIGLU_LIMIT)
        return ((up + 1.0) * gt * jax.nn.sigmoid(SWIGLU_ALPHA * gt)) @ w_down[e] + b_down[e]

    yb = lax.map(run_block, (xb, block_expert)).reshape(cap, D)
    w_assign = gates.reshape(-1)[order].astype(h.dtype)
    return jax.ops.segment_sum(yb[dest] * w_assign[:, None], tok_sorted, num_segments=T)


def setup_inputs(seed: int = 0) -> dict:
    key = jax.random.key(seed)
    ks = iter(jax.random.split(key, 64))
    nrm = lambda shape, std: std * jax.random.normal(next(ks), shape, jnp.float32)
    L, D, E, F = DEPTH, D_MODEL, N_EXPERTS, D_EXPERT
    GK = GLA_HEADS * GLA_DK
    return {
        "x": nrm((BATCH, SEQ, D), 1.0),
        "c": nrm((BATCH, D), 1.0),
        "ctx": nrm((BATCH, CTX_LEN, D), 1.0),
        "c_ctx": nrm((D,), 1.0),
        "ln_in_g": 1.0 + nrm((D,), 0.05),
        "ln_in_b": nrm((D,), 0.02),
        "ada_w": nrm((L, D, 6 * D), 0.5 * D ** -0.5),
        "ada_b": nrm((L, 6 * D), 0.02),
        "w_in": nrm((L, D, D_IN), D ** -0.5),
        "lam_q1": nrm((L, DA_QK), 0.1),
        "lam_k1": nrm((L, DA_QK), 0.1),
        "lam_q2": nrm((L, DA_QK), 0.1),
        "lam_k2": nrm((L, DA_QK), 0.1),
        "da_subln_g": 1.0 + nrm((L, DA_V), 0.05),
        "gla_gate_w2": nrm((L, 2, GLA_RANK, GK), GLA_RANK ** -0.5),
        "gla_gate_b": nrm((L, 2, GK), 0.02),
        "gla_norm_g": 1.0 + nrm((L, GLA_DV), 0.05),
        "rw_conv_w": jnp.array([0.0, 1.0, 0.0], jnp.float32)[None, :, None] + nrm((L, 3, 3 * RW_W), 0.3),
        "rw_w2": nrm((L, 2, RW_DECAY_RANK, RW_W), 0.1),
        "rw_w0": nrm((L, 2, RW_W), 1.0) - 1.0,
        "rw_a2": nrm((L, 2, RW_A_RANK, RW_W), 0.5 * RW_A_RANK ** -0.5),
        "rw_a0": nrm((L, 2, RW_W), 0.5),
        "rw_g2": nrm((L, RW_GATE_RANK, RW_W), RW_GATE_RANK ** -0.5),
        "rw_k_k": 0.85 + nrm((L, RW_W), 0.05),
        "rw_k_a": 1.0 + nrm((L, RW_W), 0.05),
        "rw_r_k": nrm((L, RW_HEADS, RW_D), 0.1),
        "rw_lnx_g": 1.0 + nrm((L, RW_W), 0.05),
        "rw_lnx_b": nrm((L, RW_W), 0.02),
        "w_out": nrm((L, MIX_W, D), DEEPNORM_BETA * MIX_W ** -0.5),
        "ln1_g": 1.0 + nrm((L, D), 0.05),
        "ln1_b": nrm((L, D), 0.02),
        "router_w": nrm((L, D, E), D ** -0.5),
        "router_b": nrm((L, E), 0.01),
        "moe_w_gate": nrm((L, E, D, F), D ** -0.5),
        "moe_b_gate": nrm((L, E, F), 0.02),
        "moe_w_up": nrm((L, E, D, F), D ** -0.5),
        "moe_b_up": nrm((L, E, F), 0.02),
        "moe_w_down": nrm((L, E, F, D), DEEPNORM_BETA * F ** -0.5),
        "moe_b_down": nrm((L, E, D), 0.02),
        "ln2_g": 1.0 + nrm((L, D), 0.05),
        "ln2_b": nrm((L, D), 0.02),
    }


def reference(x, c, ctx, c_ctx, ln_in_g, ln_in_b, ada_w, ada_b, w_in, lam_q1, lam_k1, lam_q2, lam_k2,
              da_subln_g, gla_gate_w2, gla_gate_b, gla_norm_g, rw_conv_w, rw_w2, rw_w0, rw_a2, rw_a0,
              rw_g2, rw_k_k, rw_k_a, rw_r_k, rw_lnx_g, rw_lnx_b, w_out, ln1_g, ln1_b, router_w, router_b,
              moe_w_gate, moe_b_gate, moe_w_up, moe_b_up, moe_w_down, moe_b_down, ln2_g, ln2_b):
    B, S, D = x.shape
    Lc = ctx.shape[1]
    ROWS = S // GRID_W
    cos, sin = _axial_rope_tables(ROWS, DA_QK)
    h_lat = _layer_norm(x, ln_in_g, ln_in_b)
    h_ctx = _layer_norm(ctx, ln_in_g, ln_in_b)
    for l in range(DEPTH):
        need_ctx = l < DEPTH - 1
        mod = jax.nn.silu(c) @ ada_w[l] + ada_b[l]
        mod_c = jax.nn.silu(c_ctx) @ ada_w[l] + ada_b[l]
        sh1, sc1, g1, sh2, sc2, g2 = jnp.split(mod[:, None, :], 6, axis=-1)
        sh1c, sc1c, g1c, sh2c, sc2c, g2c = jnp.split(mod_c, 6, axis=-1)

        pl = _split_in((h_lat * (1.0 + sc1) + sh1) @ w_in[l])
        pc = _split_in((h_ctx * (1.0 + sc1c) + sh1c) @ w_in[l])
        a_l, a_c = _diff_attn_mixer(pl, pc, lam_q1[l], lam_k1[l], lam_q2[l], lam_k2[l], da_subln_g[l],
                                    l, cos, sin, need_ctx)
        b_l, b_c = _gla_mixer(pl, pc, gla_gate_w2[l], gla_gate_b[l], gla_norm_g[l], need_ctx)
        r_l, r_c = _rwkv7_mixer(pl, pc, rw_conv_w[l], rw_w2[l], rw_w0[l], rw_a2[l], rw_a0[l], rw_g2[l],
                                rw_k_k[l], rw_k_a[l], rw_r_k[l], rw_lnx_g[l], rw_lnx_b[l], need_ctx)
        mix_l = jnp.concatenate([a_l, b_l, r_l], axis=-1) @ w_out[l]
        h_lat = _layer_norm(DEEPNORM_ALPHA * h_lat + g1 * mix_l, ln1_g[l], ln1_b[l])

        m_lat = (h_lat * (1.0 + sc2) + sh2).reshape(B * S, D)
        if need_ctx:
            mix_c = jnp.concatenate([a_c, b_c, r_c], axis=-1) @ w_out[l]
            h_ctx = _layer_norm(DEEPNORM_ALPHA * h_ctx + g1c * mix_c, ln1_g[l], ln1_b[l])
            m_in = jnp.concatenate([m_lat, (h_ctx * (1.0 + sc2c) + sh2c).reshape(B * Lc, D)], axis=0)
        else:
            m_in = m_lat
        f = _moe(m_in, router_w[l], router_b[l], moe_w_gate[l], moe_b_gate[l], moe_w_up[l], moe_b_up[l],
                 moe_w_down[l], moe_b_down[l])
        h_lat = _layer_norm(DEEPNORM_ALPHA * h_lat + g2 * f[:B * S].reshape(B, S, D), ln2_g[l], ln2_b[l])
        if need_ctx:
            h_ctx = _layer_norm(DEEPNORM_ALPHA * h_ctx + g2c * f[B * S:].reshape(B, Lc, D), ln2_g[l], ln2_b[l])
    return h_lat
```

```python
import functools
import math

import jax
import jax.numpy as jnp
import numpy as np
from jax import lax
from jax.experimental import pallas as pl
from jax.experimental.pallas import tpu as pltpu

F32 = jnp.float32
BF16 = jnp.bfloat16
HI = lax.Precision.HIGHEST

GRID_W = 64
DA_HEADS, DA_QK, DA_V = 4, 32, 64
GLA_HEADS, GLA_DK, GLA_DV, GLA_RANK, GLA_TAU = 6, 32, 64, 16, 16.0
RW_HEADS, RW_D, RW_DECAY_RANK, RW_A_RANK, RW_GATE_RANK = 6, 64, 64, 64, 128
RW_GN_EPS = 64e-5
N_EXPERTS, TOP_K = 32, 4
SWIGLU_LIMIT, SWIGLU_ALPHA = 7.0, 1.702
ROPE_BASE = 10000.0
LN_EPS = 1e-5

DA_W = DA_HEADS * DA_V
GLA_KW = GLA_HEADS * GLA_DK
GLA_W = GLA_HEADS * GLA_DV
RW_W = RW_HEADS * RW_D
MIX_W = DA_W + GLA_W + RW_W

LANES = 128
ROW_TILE = 256
CHUNK = 64
MOE_BLOCK = 256
VMEM_LIMIT = 56 * 1024 * 1024

Z_ATTN = 3 * DA_W
GLA_KP = 256
Z_GLA = 2 * GLA_KP + 2 * GLA_W + LANES
Z_RW = 3 * RW_W + 3 * LANES
Z_ALL = Z_ATTN + Z_GLA + Z_RW


def _cparams(sem):
    return pltpu.CompilerParams(dimension_semantics=sem, vmem_limit_bytes=VMEM_LIMIT)


def _ln(x, g, b, eps):
    mu = jnp.mean(x, axis=-1, keepdims=True)
    xc = x - mu
    var = jnp.mean(xc * xc, axis=-1, keepdims=True)
    return xc * lax.rsqrt(var + eps) * g + b


def _silu(x):
    return x * jax.nn.sigmoid(x)


def _dot(a, b):
    return jnp.dot(a.astype(BF16), b.astype(BF16), preferred_element_type=F32)


def _dot_hi(a, b):
    return jnp.dot(a, b, precision=HI, preferred_element_type=F32)


def _dot_nt(a, b):
    return lax.dot_general(a.astype(BF16), b.astype(BF16), (((1,), (1,)), ((), ())),
                           preferred_element_type=F32)


def _group_sum(x, gmat):
    hi = x.astype(BF16)
    lo = (x - hi.astype(F32)).astype(BF16)
    return (jnp.dot(hi, gmat, preferred_element_type=F32)
            + jnp.dot(lo, gmat, preferred_element_type=F32))


def _group_matrix(width, group):
    idx = np.arange(width) // group
    return jnp.asarray((idx[:, None] == idx[None, :]).astype(np.float32), dtype=BF16)


def _ada_kernel(c_ref, w_ref, b_ref, o_ref):
    o_ref[0] = _dot_hi(_silu(c_ref[...]), w_ref[0]) + b_ref[0]


def _ada_mod(c_all, ada_w, ada_b):
    nl, d, n6 = ada_w.shape
    tn = 1536
    return pl.pallas_call(
        _ada_kernel,
        out_shape=jax.ShapeDtypeStruct((nl, c_all.shape[0], n6), F32),
        grid=(nl, n6 // tn),
        in_specs=[pl.BlockSpec((c_all.shape[0], d), lambda l, j: (0, 0)),
                  pl.BlockSpec((1, d, tn), lambda l, j: (l, 0, j)),
                  pl.BlockSpec((1, 1, tn), lambda l, j: (l, 0, j))],
        out_specs=pl.BlockSpec((1, c_all.shape[0], tn), lambda l, j: (l, 0, j)),
        compiler_params=_cparams(("arbitrary", "arbitrary")),
        name="ada_mod",
    )(c_all, ada_w, ada_b.reshape(nl, 1, n6))


def _ln_in_kernel(x_ref, g_ref, b_ref, o_ref):
    o_ref[0] = _ln(x_ref[0], g_ref[...], b_ref[...], LN_EPS)


def _ln_in(xa, g, b):
    bsz, lt, d = xa.shape
    return pl.pallas_call(
        _ln_in_kernel,
        out_shape=jax.ShapeDtypeStruct(xa.shape, F32),
        grid=(bsz, lt // ROW_TILE),
        in_specs=[pl.BlockSpec((1, ROW_TILE, d), lambda b_, i: (b_, i, 0)),
                  pl.BlockSpec((1, d), lambda b_, i: (0, 0)),
                  pl.BlockSpec((1, d), lambda b_, i: (0, 0))],
        out_specs=pl.BlockSpec((1, ROW_TILE, d), lambda b_, i: (b_, i, 0)),
        compiler_params=_cparams(("arbitrary", "arbitrary")),
        name="ln_in",
    )(xa, g.reshape(1, d), b.reshape(1, d))


def _inproj_kernel(h_ref, mod_ref, w_ref, cos_ref, sin_ref, qk_ref, v_ref, gla_ref, rw_ref):
    h = h_ref[0]
    sc = mod_ref[0, 0, 0:1, :]
    sh = mod_ref[0, 0, 1:2, :]
    xm = (h * (1.0 + sc) + sh).astype(BF16)
    qk = jnp.dot(xm, w_ref[:, 0:2 * DA_W], preferred_element_type=F32)
    cos = cos_ref[...]
    sin = sin_ref[...]
    lane = lax.broadcasted_iota(jnp.int32, (1, LANES), 1)
    first = (lane % 16) < 8
    qscale = DA_QK ** -0.5
    for j in range(4):
        x = qk[:, j * LANES:(j + 1) * LANES]
        rot = jnp.where(first, pltpu.roll(x, LANES - 8, 1), pltpu.roll(x, 8, 1))
        y = x * cos + rot * sin
        if j < 2:
            y = y * qscale
        qk_ref[0, :, j * LANES:(j + 1) * LANES] = y.astype(BF16)
    v_ref[0] = jnp.dot(xm, w_ref[:, 2 * DA_W:Z_ATTN], preferred_element_type=F32).astype(BF16)
    gla_ref[0] = jnp.dot(xm, w_ref[:, Z_ATTN:Z_ATTN + Z_GLA], preferred_element_type=F32)
    rw_ref[0] = jnp.dot(xm, w_ref[:, Z_ATTN + Z_GLA:Z_ALL], preferred_element_type=F32)


def _inproj(h, mod1, w_p, cos_t, sin_t, n_ctx_tiles):
    bsz, lt, d = h.shape
    nt = lt // ROW_TILE
    row = lambda b_, i: (b_, i, 0)
    return pl.pallas_call(
        _inproj_kernel,
        out_shape=(jax.ShapeDtypeStruct((bsz, lt, 2 * DA_W), BF16),
                   jax.ShapeDtypeStruct((bsz, lt, DA_W), BF16),
                   jax.ShapeDtypeStruct((bsz, lt, Z_GLA), F32),
                   jax.ShapeDtypeStruct((bsz, lt, Z_RW), F32)),
        grid=(bsz, nt),
        in_specs=[pl.BlockSpec((1, ROW_TILE, d), row),
                  pl.BlockSpec((1, 1, 2, d),
                               lambda b_, i: (b_, jnp.where(i < n_ctx_tiles, 0, 1), 0, 0)),
                  pl.BlockSpec((d, Z_ALL), lambda b_, i: (0, 0)),
                  pl.BlockSpec((ROW_TILE, LANES), lambda b_, i: (i, 0)),
                  pl.BlockSpec((ROW_TILE, LANES), lambda b_, i: (i, 0))],
        out_specs=(pl.BlockSpec((1, ROW_TILE, 2 * DA_W), row),
                   pl.BlockSpec((1, ROW_TILE, DA_W), row),
                   pl.BlockSpec((1, ROW_TILE, Z_GLA), row),
                   pl.BlockSpec((1, ROW_TILE, Z_RW), row)),
        compiler_params=_cparams(("arbitrary", "arbitrary")),
        name="inproj",
    )(h, mod1, w_p, cos_t, sin_t)


def _pack_w_in(w):
    d = w.shape[0]
    o = 0
    parts = {}
    for name, n in (("da_q", 256), ("da_k", 256), ("da_v", 256), ("gla_q", GLA_KW), ("gla_k", GLA_KW),
                    ("gla_v", GLA_W), ("gla_gf", GLA_RANK), ("gla_gb", GLA_RANK), ("gla_r", GLA_W),
                    ("rw_rkv", 3 * RW_W), ("rw_wf", 64), ("rw_wb", 64), ("rw_af", 64), ("rw_ab", 64),
                    ("rw_g", 128)):
        parts[name] = w[:, o:o + n]
        o += n
    z = lambda n: jnp.zeros((d, n), w.dtype)
    cols = [parts["da_q"], parts["da_k"], parts["da_v"],
            parts["gla_q"], z(GLA_KP - GLA_KW), parts["gla_k"], z(GLA_KP - GLA_KW),
            parts["gla_v"], parts["gla_r"], parts["gla_gf"], parts["gla_gb"], z(LANES - 2 * GLA_RANK),
            parts["rw_rkv"], parts["rw_wf"], parts["rw_wb"], parts["rw_af"], parts["rw_ab"], parts["rw_g"]]
    return jnp.concatenate(cols, axis=1).astype(BF16)


def _rope_tables(n_ctx, seq):
    t = np.arange(seq)
    row = (t // GRID_W).astype(np.float32)
    col = (t % GRID_W).astype(np.float32)
    quarter = DA_QK // 4
    inv = (ROPE_BASE ** (-np.arange(quarter, dtype=np.float32) / quarter)).astype(np.float32)
    ang_r = row[:, None] * inv
    ang_c = col[:, None] * inv
    ang = np.concatenate([ang_r, ang_r, ang_c, ang_c], -1).astype(np.float32)
    cos = np.tile(np.cos(ang), (1, LANES // DA_QK))
    sin = np.tile(np.sin(ang), (1, LANES // DA_QK))
    sign = np.where((np.arange(LANES) % 16) < 8, -1.0, 1.0).astype(np.float32)
    cos = np.concatenate([np.ones((n_ctx, LANES), np.float32), cos], 0)
    sin = np.concatenate([np.zeros((n_ctx, LANES), np.float32), sin * sign], 0)
    return jnp.asarray(cos, F32), jnp.asarray(sin, F32)


def _attn_kernel(lam_ref, q_ref, k_ref, v_ref, o_ref):
    q = q_ref[0]
    k = k_ref[0]
    v = v_ref[0]
    lam = lam_ref[0, 0]
    lane = lax.broadcasted_iota(jnp.int32, (1, LANES), 1)
    outs = []
    for hh in range(2):
        acc = None
        for m in range(2):
            lo = hh * 64 + m * DA_QK
            qm = jnp.where((lane >= lo) & (lane < lo + DA_QK), q, jnp.zeros_like(q))
            s = _dot_nt(qm, k)
            e = jnp.exp(s - jnp.max(s, axis=-1, keepdims=True))
            den = jnp.sum(e, axis=-1, keepdims=True)
            pv = jnp.dot(e.astype(BF16), v, preferred_element_type=F32) / den
            acc = pv if m == 0 else acc - lam * pv
        outs.append(acc)
    o_ref[0] = jnp.where(lane < 64, outs[0], outs[1])


def _attention(qk, v, lam, q_tile0, n_q_tiles, n_k_rows):
    bsz, lt, _ = qk.shape
    out = pl.pallas_call(
        _attn_kernel,
        out_shape=jax.ShapeDtypeStruct((bsz, n_q_tiles * ROW_TILE, DA_W), F32),
        grid=(bsz, 2, n_q_tiles),
        in_specs=[pl.BlockSpec(memory_space=pltpu.SMEM),
                  pl.BlockSpec((1, ROW_TILE, LANES), lambda b_, p, i: (b_, q_tile0 + i, p)),
                  pl.BlockSpec((1, n_k_rows, LANES), lambda b_, p, i: (b_, 0, 2 + p)),
                  pl.BlockSpec((1, n_k_rows, LANES), lambda b_, p, i: (b_, 0, p))],
        out_specs=pl.BlockSpec((1, ROW_TILE, LANES), lambda b_, p, i: (b_, i, p)),
        compiler_params=_cparams(("arbitrary", "arbitrary", "arbitrary")),
        name="diff_attn",
    )(lam.reshape(1, 1), qk, qk, v)
    return out


def _mixout_kernel(alpha, sub_scale,
                   ao_ref, gof_ref, gob_ref, zg_ref, yf_ref, yb_ref, bonus_ref, rg_ref, h_ref, mod_ref,
                   wo_ref, vec_ref, nrm_ref, g256_ref, g384_ref, rw_ref, rb_ref,
                   h1_ref, m_ref, ti_ref, gt_ref):
    a = ao_ref[0]
    a = a * lax.rsqrt(_group_sum(a * a, g256_ref[...]) * (1.0 / DA_V) + LN_EPS) * nrm_ref[0:1, 0:DA_W] * sub_scale
    o = gof_ref[0] + gob_ref[0]
    r = zg_ref[0, :, 2 * GLA_KP + GLA_W:2 * GLA_KP + 2 * GLA_W]
    gl = (o * lax.rsqrt(_group_sum(o * o, g384_ref[...]) * (1.0 / GLA_DV) + LN_EPS)
          * nrm_ref[1:2, :] * _silu(r))
    y = yf_ref[0] + yb_ref[0]
    mu = _group_sum(y, g384_ref[...]) * (1.0 / RW_D)
    yc = y - mu
    var = _group_sum(yc * yc, g384_ref[...]) * (1.0 / RW_D)
    yn = yc * lax.rsqrt(var + RW_GN_EPS) * nrm_ref[2:3, :] + nrm_ref[3:4, :]
    rw = (yn + bonus_ref[0]) * rg_ref[0]
    mix = (_dot(a, wo_ref[0:DA_W, :]) + _dot(gl, wo_ref[DA_W:DA_W + GLA_W, :])
           + _dot(rw, wo_ref[DA_W + GLA_W:MIX_W, :]))
    g1 = mod_ref[0, 0, 0:1, :]
    sc2 = mod_ref[0, 0, 1:2, :]
    sh2 = mod_ref[0, 0, 2:3, :]
    h1 = _ln(alpha * h_ref[0] + g1 * mix, vec_ref[0:1, :], vec_ref[1:2, :], LN_EPS)
    h1_ref[0] = h1
    m = h1 * (1.0 + sc2) + sh2
    m_ref[0] = m
    logits = _dot_hi(m, rw_ref[...]) + rb_ref[...]
    lane = lax.broadcasted_iota(jnp.int32, logits.shape, 1)
    ti = jnp.zeros(logits.shape, jnp.int32)
    tv = jnp.full(logits.shape, -1e30, F32)
    for j in range(TOP_K):
        mx = jnp.max(logits, axis=-1, keepdims=True)
        idx = jnp.min(jnp.where(logits == mx, lane, LANES), axis=-1, keepdims=True)
        ti = jnp.where(lane == j, idx, ti)
        tv = jnp.where(lane == j, mx, tv)
        logits = jnp.where(lane == idx, -jnp.inf, logits)
    e = jnp.exp(tv - jnp.max(tv, axis=-1, keepdims=True))
    ti_ref[0] = ti
    gt_ref[0] = e / jnp.sum(e, axis=-1, keepdims=True)


def _mixout(alpha, sub_scale, tile0, n_tiles, n_ctx_tiles,
            ao, gof, gob, zg, yf, yb, bonus, rg, h, mod2, wo, vec, nrm, rw_p, rb_p):
    bsz, lt, d = h.shape
    row = lambda b_, i: (b_, tile0 + i, 0)
    orow = lambda b_, i: (b_, i, 0)
    full = lambda shape: pl.BlockSpec(shape, lambda b_, i: tuple(0 for _ in shape))
    nrows = n_tiles * ROW_TILE
    return pl.pallas_call(
        functools.partial(_mixout_kernel, alpha, sub_scale),
        out_shape=(jax.ShapeDtypeStruct((bsz, nrows, d), F32),
                   jax.ShapeDtypeStruct((bsz, nrows, d), F32),
                   jax.ShapeDtypeStruct((bsz, nrows, LANES), jnp.int32),
                   jax.ShapeDtypeStruct((bsz, nrows, LANES), F32)),
        grid=(bsz, n_tiles),
        in_specs=[pl.BlockSpec((1, ROW_TILE, DA_W), orow),
                  pl.BlockSpec((1, ROW_TILE, GLA_W), row),
                  pl.BlockSpec((1, ROW_TILE, GLA_W), row),
                  pl.BlockSpec((1, ROW_TILE, Z_GLA), row),
                  pl.BlockSpec((1, ROW_TILE, RW_W), row),
                  pl.BlockSpec((1, ROW_TILE, RW_W), row),
                  pl.BlockSpec((1, ROW_TILE, RW_W), row),
                  pl.BlockSpec((1, ROW_TILE, RW_W), row),
                  pl.BlockSpec((1, ROW_TILE, d), row),
                  pl.BlockSpec((1, 1, 3, d),
                               lambda b_, i: (b_, jnp.where(tile0 + i < n_ctx_tiles, 0, 1), 0, 0)),
                  full((MIX_W, d)), full((2, d)), full((4, GLA_W)),
                  full((DA_W, DA_W)), full((GLA_W, GLA_W)), full((d, LANES)), full((1, LANES))],
        out_specs=(pl.BlockSpec((1, ROW_TILE, d), orow),
                   pl.BlockSpec((1, ROW_TILE, d), orow),
                   pl.BlockSpec((1, ROW_TILE, LANES), orow),
                   pl.BlockSpec((1, ROW_TILE, LANES), orow)),
        compiler_params=_cparams(("arbitrary", "arbitrary")),
        name="mix_out",
    )(ao, gof, gob, zg, yf, yb, bonus, rg, h, mod2, wo, vec, nrm,
      _group_matrix(DA_W, DA_V), _group_matrix(GLA_W, GLA_DV), rw_p, rb_p)


def _bwd_chunk(step, n_ctx_chunks, n_chunks):
    return jnp.where(step < n_ctx_chunks, n_ctx_chunks - 1 - step, n_chunks + n_ctx_chunks - 1 - step)


def _tri(n, reverse, strict):
    t = lax.broadcasted_iota(jnp.int32, (n, n), 0)
    s = lax.broadcasted_iota(jnp.int32, (n, n), 1)
    if reverse:
        return (s > t) if strict else (s >= t)
    return (s < t) if strict else (s <= t)


def _dot_tn(a, b):
    return lax.dot_general(a, b, (((0,), (0,)), ((), ())), precision=HI, preferred_element_type=F32)


def _dot_nt_hi(a, b):
    return lax.dot_general(a, b, (((1,), (1,)), ((), ())), precision=HI, preferred_element_type=F32)


def _log_sigmoid(x):
    return jnp.minimum(x, 0.0) - jnp.log1p(jnp.exp(-jnp.abs(x)))


def _softplus(x):
    return jnp.maximum(x, 0.0) + jnp.log1p(jnp.exp(-jnp.abs(x)))


GLA_SUB = 16
GLA_EXP_CLAMP = 60.0


def _gla_dir(zg, w2_ref, bias_ref, st_ref, reverse, col0):
    q = zg[:, 0:GLA_KP] * (GLA_DK ** -0.5)
    k = zg[:, GLA_KP:2 * GLA_KP]
    v = zg[:, 2 * GLA_KP:2 * GLA_KP + GLA_W]
    gpre = _dot_hi(zg[:, 2 * GLA_KP + 2 * GLA_W:Z_GLA], w2_ref[:, col0:col0 + GLA_KP]) \
        + bias_ref[:, col0:col0 + GLA_KP]
    g = _log_sigmoid(gpre) * (1.0 / GLA_TAU)
    b = _dot_hi(jnp.where(_tri(CHUNK, reverse, False), 1.0, 0.0).astype(F32), g)
    last = 0 if reverse else CHUNK - 1
    b_last = b[last:last + 1, :]
    st = st_ref[...]
    inter = _dot_nt_hi(q * jnp.exp(b), st)
    kv = _dot_tn(v, k * jnp.exp(b_last - b))
    vi = lax.broadcasted_iota(jnp.int32, (GLA_W, GLA_KP), 0) // GLA_DV
    ki = lax.broadcasted_iota(jnp.int32, (GLA_W, GLA_KP), 1) // GLA_DK
    st_ref[...] = st * jnp.exp(b_last) + jnp.where(vi == ki, kv, 0.0)

    lane_head = lax.broadcasted_iota(jnp.int32, (1, GLA_KP), 1) // GLA_DK
    nsub = CHUNK // GLA_SUB
    per_head = [[None] * nsub for _ in range(GLA_HEADS)]
    for i in range(nsub):
        r0 = i * GLA_SUB
        ref_row = r0 + GLA_SUB - 1 if reverse else r0
        b_ref = b[ref_row:ref_row + 1, :]
        qi = q[r0:r0 + GLA_SUB] * jnp.exp(b[r0:r0 + GLA_SUB] - b_ref)
        ki_ = k * jnp.exp(jnp.minimum(b_ref - b, GLA_EXP_CLAMP))
        qh = jnp.concatenate([jnp.where(lane_head == h, qi, 0.0) for h in range(GLA_HEADS)], axis=0)
        a = _dot_nt_hi(qh, ki_)
        for h in range(GLA_HEADS):
            per_head[h][i] = a[h * GLA_SUB:(h + 1) * GLA_SUB]
    causal = _tri(CHUNK, reverse, False)
    lane = lax.broadcasted_iota(jnp.int32, (1, LANES), 1)
    pieces = []
    for p in range(GLA_HEADS // 2):
        vp = v[:, p * LANES:(p + 1) * LANES]
        halves = []
        for hh in range(2):
            a_h = jnp.where(causal, jnp.concatenate(per_head[2 * p + hh], axis=0), 0.0)
            halves.append(_dot_hi(a_h, vp))
        pieces.append(jnp.where(lane < GLA_DV, halves[0], halves[1]))
    return inter + jnp.concatenate(pieces, axis=1)


def _gla_kernel(zf_ref, zb_ref, w2_ref, bias_ref, of_ref, ob_ref, sf_ref, sb_ref):
    @pl.when(pl.program_id(1) == 0)
    def _():
        sf_ref[...] = jnp.zeros_like(sf_ref)
        sb_ref[...] = jnp.zeros_like(sb_ref)

    of_ref[0] = _gla_dir(zf_ref[0], w2_ref, bias_ref, sf_ref, False, 0)
    ob_ref[0] = _gla_dir(zb_ref[0], w2_ref, bias_ref, sb_ref, True, GLA_KP)


def _gla_scan(zg, w2p, biasp, n_ctx_chunks):
    bsz, lt, _ = zg.shape
    nc = lt // CHUNK
    fwd = lambda b_, s: (b_, s, 0)
    bwd = lambda b_, s: (b_, _bwd_chunk(s, n_ctx_chunks, nc), 0)
    return pl.pallas_call(
        _gla_kernel,
        out_shape=(jax.ShapeDtypeStruct((bsz, lt, GLA_W), F32),
                   jax.ShapeDtypeStruct((bsz, lt, GLA_W), F32)),
        grid=(bsz, nc),
        in_specs=[pl.BlockSpec((1, CHUNK, Z_GLA), fwd),
                  pl.BlockSpec((1, CHUNK, Z_GLA), bwd),
                  pl.BlockSpec((LANES, 2 * GLA_KP), lambda b_, s: (0, 0)),
                  pl.BlockSpec((1, 2 * GLA_KP), lambda b_, s: (0, 0))],
        out_specs=(pl.BlockSpec((1, CHUNK, GLA_W), fwd),
                   pl.BlockSpec((1, CHUNK, GLA_W), bwd)),
        scratch_shapes=[pltpu.VMEM((GLA_W, GLA_KP), F32), pltpu.VMEM((GLA_W, GLA_KP), F32)],
        compiler_params=_cparams(("arbitrary", "arbitrary")),
        name="gla_scan",
    )(zg, zg, w2p, biasp)


def _pack_gla_gate(w2, bias):
    w = jnp.zeros((LANES, 2 * GLA_KP), F32)
    w = w.at[0:GLA_RANK, 0:GLA_KW].set(w2[0]).at[GLA_RANK:2 * GLA_RANK, GLA_KP:GLA_KP + GLA_KW].set(w2[1])
    b = jnp.zeros((1, 2 * GLA_KP), F32)
    b = b.at[0, 0:GLA_KW].set(bias[0]).at[0, GLA_KP:GLA_KP + GLA_KW].set(bias[1])
    return w, b


def _rwprep_kernel(n_ctx_tiles, n_tiles,
                   z_ref, zp_ref, zn_ref, cw_ref, w2_ref, a2_ref, g2_ref, vec_ref, gm_ref,
                   sh_ref, df_ref, db_ref, g_ref, bonus_ref):
    i = pl.program_id(1)
    z = z_ref[0]
    x = z[:, 0:3 * RW_W]
    seg_first = (i == 0) | (i == n_ctx_tiles)
    seg_last = (i == n_ctx_tiles - 1) | (i == n_tiles - 1)
    prev_row = jnp.where(seg_first, 0.0, zp_ref[0, 7:8, 0:3 * RW_W])
    next_row = jnp.where(seg_last, 0.0, zn_ref[0, 0:1, 0:3 * RW_W])
    ridx = lax.broadcasted_iota(jnp.int32, (ROW_TILE, 1), 0)
    x_prev = jnp.where(ridx == 0, prev_row, pltpu.roll(x, 1, 0))
    x_next = jnp.where(ridx == ROW_TILE - 1, next_row, pltpu.roll(x, ROW_TILE - 1, 0))
    xc = x_prev * cw_ref[0:1, :] + x * cw_ref[1:2, :] + x_next * cw_ref[2:3, :]
    r = xc[:, 0:RW_W]
    k = xc[:, RW_W:2 * RW_W]
    v = xc[:, 2 * RW_W:3 * RW_W]
    gm = gm_ref[...]
    kk = k * vec_ref[0:1, :]
    kk = kk / jnp.maximum(jnp.sqrt(_group_sum(kk * kk, gm)), 1e-12)
    k_a = vec_ref[1:2, :]
    r_k = vec_ref[2:3, :]
    w_raw = _dot_hi(jnp.tanh(z[:, 3 * RW_W:3 * RW_W + LANES]), w2_ref[...])
    a_raw = _dot_hi(z[:, 3 * RW_W + LANES:3 * RW_W + 2 * LANES], a2_ref[...])
    g_ref[0] = _dot_hi(jax.nn.sigmoid(z[:, 3 * RW_W + 2 * LANES:Z_RW]), g2_ref[...])
    sh_ref[0, :, 0:RW_W] = r
    sh_ref[0, :, RW_W:2 * RW_W] = v
    sh_ref[0, :, 2 * RW_W:3 * RW_W] = kk
    rk_sum = None
    for d, d_ref in enumerate((df_ref, db_ref)):
        wr = w_raw[:, d * RW_W:(d + 1) * RW_W] + vec_ref[3 + d:4 + d, :]
        logw = -jnp.exp(-_softplus(-wr) - 0.5)
        a = jax.nn.sigmoid(a_raw[:, d * RW_W:(d + 1) * RW_W] + vec_ref[5 + d:6 + d, :])
        k_mod = k * (1.0 + (a - 1.0) * k_a)
        d_ref[0, :, 0:RW_W] = logw
        d_ref[0, :, RW_W:2 * RW_W] = kk * a
        d_ref[0, :, 2 * RW_W:3 * RW_W] = k_mod
        s = _group_sum(r * k_mod * r_k, gm)
        rk_sum = s if d == 0 else rk_sum + s
    bonus_ref[0] = rk_sum * v


def _rw_prep(zr, cw, w2p, a2p, g2, vec, n_ctx_tiles):
    bsz, lt, _ = zr.shape
    nt = lt // ROW_TILE
    hb = ROW_TILE // 8
    row = lambda b_, i: (b_, i, 0)
    full = lambda shape: pl.BlockSpec(shape, lambda b_, i: tuple(0 for _ in shape))
    o3 = jax.ShapeDtypeStruct((bsz, lt, 3 * RW_W), F32)
    o1 = jax.ShapeDtypeStruct((bsz, lt, RW_W), F32)
    return pl.pallas_call(
        functools.partial(_rwprep_kernel, n_ctx_tiles, nt),
        out_shape=(o3, o3, o3, o1, o1),
        grid=(bsz, nt),
        in_specs=[pl.BlockSpec((1, ROW_TILE, Z_RW), row),
                  pl.BlockSpec((1, 8, Z_RW), lambda b_, i: (b_, jnp.maximum(i * hb - 1, 0), 0)),
                  pl.BlockSpec((1, 8, Z_RW), lambda b_, i: (b_, jnp.minimum((i + 1) * hb, nt * hb - 1), 0)),
                  full((3, 3 * RW_W)), full((LANES, 2 * RW_W)), full((LANES, 2 * RW_W)),
                  full((RW_GATE_RANK, RW_W)), full((8, RW_W)), full((RW_W, RW_W))],
        out_specs=(pl.BlockSpec((1, ROW_TILE, 3 * RW_W), row),
                   pl.BlockSpec((1, ROW_TILE, 3 * RW_W), row),
                   pl.BlockSpec((1, ROW_TILE, 3 * RW_W), row),
                   pl.BlockSpec((1, ROW_TILE, RW_W), row),
                   pl.BlockSpec((1, ROW_TILE, RW_W), row)),
        compiler_params=_cparams(("arbitrary", "arbitrary")),
        name="rwkv_prep",
    )(zr, zr, zr, cw, w2p, a2p, g2, vec, _group_matrix(RW_W, RW_D))


def _block2(w):
    r, n = w.shape[1:]
    z = jnp.zeros((r, n), w.dtype)
    return jnp.concatenate([jnp.concatenate([w[0], z], 1), jnp.concatenate([z, w[1]], 1)], 0)


def _rw_head(r, v, kk, logw, beta, k, h_ref, hi, reverse):
    c = CHUNK
    incl = _tri(c, reverse, False)
    strict = _tri(c, reverse, True)
    b = _dot_hi(jnp.where(incl, 1.0, 0.0).astype(F32), logw)
    last = 0 if reverse else c - 1
    b_last = b[last:last + 1, :]
    e_b = jnp.exp(b)
    e_nb = jnp.exp(-b)
    e_tot = jnp.exp(b_last - b)
    abar = kk * jnp.exp(b - logw)
    rbar = r * e_b
    kt = k * e_nb
    bt = beta * e_nb
    l_ab = jnp.where(strict, _dot_nt_hi(abar, bt), 0.0)
    l_ak = jnp.where(strict, _dot_nt_hi(abar, kt), 0.0)
    l_rb = jnp.where(incl, _dot_nt_hi(rbar, bt), 0.0)
    l_rk = jnp.where(incl, _dot_nt_hi(rbar, kt), 0.0)
    ti = lax.broadcasted_iota(jnp.int32, (c, c), 0)
    si = lax.broadcasted_iota(jnp.int32, (c, c), 1)
    eye = jnp.where(ti == si, 1.0, 0.0).astype(F32)
    t_inv = None
    s = 1
    while s < c:
        same = (ti // (2 * s)) == (si // (2 * s))
        if reverse:
            off = same & ((ti // s) % 2 == 0) & ((si // s) % 2 == 1)
        else:
            off = same & ((ti // s) % 2 == 1) & ((si // s) % 2 == 0)
        l_off = jnp.where(off, l_ab, 0.0)
        if t_inv is None:
            t_inv = eye - l_off
        else:
            t_inv = t_inv - _dot_hi(t_inv, _dot_hi(l_off, t_inv))
        s *= 2
    h0 = h_ref[hi]
    u = _dot_hi(t_inv, _dot_hi(abar, h0) + _dot_hi(l_ak, v))
    y = _dot_hi(rbar, h0) + _dot_hi(l_rk, v) - _dot_hi(l_rb, u)
    dg = jnp.where(ti == si, jnp.broadcast_to(jnp.exp(b_last), (c, c)), 0.0)
    h_ref[hi] = _dot_hi(dg, h0) + _dot_tn(k * e_tot, v) - _dot_tn(beta * e_tot, u)
    return y


def _rw_kernel(sf_ref, df_ref, sb_ref, db_ref, yf_ref, yb_ref, hf_ref, hb_ref):
    @pl.when(pl.program_id(1) == 0)
    def _():
        hf_ref[...] = jnp.zeros_like(hf_ref)
        hb_ref[...] = jnp.zeros_like(hb_ref)

    for s_ref, d_ref, y_ref, h_ref, reverse in ((sf_ref, df_ref, yf_ref, hf_ref, False),
                                                (sb_ref, db_ref, yb_ref, hb_ref, True)):
        sh = s_ref[0]
        dd = d_ref[0]
        ys = []
        for h in range(RW_HEADS):
            sl = lambda arr, j: arr[:, j * RW_W + h * RW_D:j * RW_W + (h + 1) * RW_D]
            ys.append(_rw_head(sl(sh, 0), sl(sh, 1), sl(sh, 2), sl(dd, 0), sl(dd, 1), sl(dd, 2),
                               h_ref, h, reverse))
        y_ref[0] = jnp.concatenate(ys, axis=1)


def _rw_scan(shared, dfw, dbw, n_ctx_chunks):
    bsz, lt, _ = shared.shape
    nc = lt // CHUNK
    fwd = lambda b_, s: (b_, s, 0)
    bwd = lambda b_, s: (b_, _bwd_chunk(s, n_ctx_chunks, nc), 0)
    return pl.pallas_call(
        _rw_kernel,
        out_shape=(jax.ShapeDtypeStruct((bsz, lt, RW_W), F32),
                   jax.ShapeDtypeStruct((bsz, lt, RW_W), F32)),
        grid=(bsz, nc),
        in_specs=[pl.BlockSpec((1, CHUNK, 3 * RW_W), fwd),
                  pl.BlockSpec((1, CHUNK, 3 * RW_W), fwd),
                  pl.BlockSpec((1, CHUNK, 3 * RW_W), bwd),
                  pl.BlockSpec((1, CHUNK, 3 * RW_W), bwd)],
        out_specs=(pl.BlockSpec((1, CHUNK, RW_W), fwd),
                   pl.BlockSpec((1, CHUNK, RW_W), bwd)),
        scratch_shapes=[pltpu.VMEM((RW_HEADS, RW_D, RW_D), F32), pltpu.VMEM((RW_HEADS, RW_D, RW_D), F32)],
        compiler_params=_cparams(("arbitrary", "arbitrary")),
        name="rwkv_scan",
    )(shared, dfw, shared, dbw)


def _rank_kernel(n, ti_ref, dest_ref, meta_ref, cnt_ref, run_ref, start_ref):
    ph = pl.program_id(0)
    i = pl.program_id(1)
    ti = ti_ref[...]
    lane = lax.broadcasted_iota(jnp.int32, (ROW_TILE, LANES), 1)
    ohs = [jnp.where(ti[:, j:j + 1] == lane, 1.0, 0.0).astype(F32) for j in range(TOP_K)]
    oh = ohs[0] + ohs[1] + ohs[2] + ohs[3]
    tile_cnt = jnp.sum(oh, axis=0, keepdims=True)

    @pl.when((ph == 0) & (i == 0))
    def _():
        cnt_ref[...] = jnp.zeros_like(cnt_ref)
        run_ref[...] = jnp.zeros_like(run_ref)

    @pl.when(ph == 0)
    def _():
        cnt_ref[...] += tile_cnt

    @pl.when((ph == 0) & (i == n - 1))
    def _():
        cnt = cnt_ref[...]
        shift = MOE_BLOCK.bit_length() - 1
        padded = jnp.left_shift(jnp.right_shift(cnt.astype(jnp.int32) + (MOE_BLOCK - 1), shift),
                                shift).astype(F32)
        e0 = lax.broadcasted_iota(jnp.int32, (LANES, LANES), 0)
        e1 = lax.broadcasted_iota(jnp.int32, (LANES, LANES), 1)
        before = jnp.where(e0 < e1, 1.0, 0.0).astype(F32)
        start = _dot_hi(jnp.broadcast_to(padded, (8, LANES)), before)[0:1]
        start_ref[...] = start
        meta_ref[0:1, :] = cnt
        meta_ref[1:2, :] = start
        meta_ref[2:3, :] = padded
        meta_ref[3:8, :] = jnp.zeros((5, LANES), F32)

    @pl.when(ph == 1)
    def _():
        t0 = lax.broadcasted_iota(jnp.int32, (ROW_TILE, ROW_TILE), 0)
        t1 = lax.broadcasted_iota(jnp.int32, (ROW_TILE, ROW_TILE), 1)
        earlier = jnp.where(t1 < t0, 1.0, 0.0).astype(BF16)
        pos = (jnp.dot(earlier, oh.astype(BF16), preferred_element_type=F32)
               + run_ref[...] + start_ref[...])
        dest = jnp.zeros((ROW_TILE, LANES), F32)
        for j in range(TOP_K):
            dj = jnp.sum(ohs[j] * pos, axis=-1, keepdims=True)
            dest = jnp.where(lane == j, dj, dest)
        dest_ref[...] = dest.astype(jnp.int32)
        run_ref[...] += tile_cnt


def _moe_rank(ti):
    n = ti.shape[0]
    nt = n // ROW_TILE
    return pl.pallas_call(
        functools.partial(_rank_kernel, nt),
        out_shape=(jax.ShapeDtypeStruct((n, LANES), jnp.int32),
                   jax.ShapeDtypeStruct((8, LANES), F32)),
        grid=(2, nt),
        in_specs=[pl.BlockSpec((ROW_TILE, LANES), lambda p, i: (i, 0))],
        out_specs=(pl.BlockSpec((ROW_TILE, LANES), lambda p, i: (i * p, 0)),
                   pl.BlockSpec((8, LANES), lambda p, i: (0, 0))),
        scratch_shapes=[pltpu.VMEM((1, LANES), F32), pltpu.VMEM((1, LANES), F32),
                        pltpu.VMEM((1, LANES), F32)],
        compiler_params=_cparams(("arbitrary", "arbitrary")),
        name="moe_rank",
    )(ti)


def _row_copy(src_ref, src_row, dst_ref, dst_row, sem):
    return pltpu.make_async_copy(src_ref.at[pl.ds(src_row, 1), :], dst_ref.at[pl.ds(dst_row, 1), :], sem)


def _dispatch_kernel(dest_ref, m_ref, xb_in_ref, xb_ref, sem):
    del xb_in_ref
    i = pl.program_id(0)

    def issue(t, c):
        for j in range(TOP_K):
            _row_copy(m_ref, t, xb_ref, dest_ref[i, t * TOP_K + j], sem).start()
        return c

    lax.fori_loop(0, ROW_TILE, issue, 0)

    def drain(t, c):
        for j in range(TOP_K):
            _row_copy(m_ref, 0, xb_ref, 0, sem).wait()
        return c

    lax.fori_loop(0, ROW_TILE, drain, 0)


def _moe_dispatch(dest2d, m, cap):
    n, d = m.shape
    nt = n // ROW_TILE
    return pl.pallas_call(
        _dispatch_kernel,
        out_shape=jax.ShapeDtypeStruct((cap, d), m.dtype),
        grid_spec=pltpu.PrefetchScalarGridSpec(
            num_scalar_prefetch=1,
            grid=(nt,),
            in_specs=[pl.BlockSpec((ROW_TILE, d), lambda i, dst: (i, 0)),
                      pl.BlockSpec(memory_space=pl.ANY)],
            out_specs=pl.BlockSpec(memory_space=pl.ANY),
            scratch_shapes=[pltpu.SemaphoreType.DMA(())]),
        input_output_aliases={2: 0},
        compiler_params=_cparams(("arbitrary",)),
        name="moe_dispatch",
    )(dest2d, m, jnp.zeros((cap, d), m.dtype))


def _expert_kernel(be_ref, nu_ref, x_ref, wg_ref, bg_ref, wu_ref, bu_ref, wd_ref, bd_ref, y_ref,
                   wg_s, wu_s, wd_s):
    b = pl.program_id(0)
    e = be_ref[b]
    changed = (b == 0) | (e != be_ref[jnp.maximum(b - 1, 0)])

    @pl.when(changed & (b < nu_ref[0]))
    def _():
        wg_s[...] = wg_ref[0].astype(BF16)
        wu_s[...] = wu_ref[0].astype(BF16)
        wd_s[...] = wd_ref[0].astype(BF16)

    @pl.when(b < nu_ref[0])
    def _():
        x = x_ref[...].astype(BF16)
        gt = jnp.minimum(jnp.dot(x, wg_s[...], preferred_element_type=F32) + bg_ref[0], SWIGLU_LIMIT)
        up = jnp.clip(jnp.dot(x, wu_s[...], preferred_element_type=F32) + bu_ref[0],
                      -SWIGLU_LIMIT, SWIGLU_LIMIT)
        act = (up + 1.0) * gt * jax.nn.sigmoid(SWIGLU_ALPHA * gt)
        y_ref[...] = jnp.dot(act.astype(BF16), wd_s[...], preferred_element_type=F32) + bd_ref[0]

    @pl.when(b >= nu_ref[0])
    def _():
        y_ref[...] = jnp.zeros_like(y_ref)


def _moe_experts(block_expert, n_used, xb, wg, bg, wu, bu, wd, bd):
    cap, d = xb.shape
    ne, _, f = wg.shape
    nb = cap // MOE_BLOCK
    wmap = lambda b, be, nu: (be[jnp.minimum(b, nu[0] - 1)], 0, 0)
    return pl.pallas_call(
        _expert_kernel,
        out_shape=jax.ShapeDtypeStruct((cap, d), F32),
        grid_spec=pltpu.PrefetchScalarGridSpec(
            num_scalar_prefetch=2,
            grid=(nb,),
            in_specs=[pl.BlockSpec((MOE_BLOCK, d), lambda b, be, nu: (b, 0)),
                      pl.BlockSpec((1, d, f), wmap), pl.BlockSpec((1, 1, f), wmap),
                      pl.BlockSpec((1, d, f), wmap), pl.BlockSpec((1, 1, f), wmap),
                      pl.BlockSpec((1, f, d), wmap), pl.BlockSpec((1, 1, d), wmap)],
            out_specs=pl.BlockSpec((MOE_BLOCK, d), lambda b, be, nu: (b, 0)),
            scratch_shapes=[pltpu.VMEM((d, f), BF16), pltpu.VMEM((d, f), BF16), pltpu.VMEM((f, d), BF16)]),
        compiler_params=_cparams(("arbitrary",)),
        name="moe_experts",
    )(block_expert, n_used, xb, wg, bg.reshape(ne, 1, f), wu, bu.reshape(ne, 1, f), wd, bd.reshape(ne, 1, d))


def _combine_kernel(alpha, n_tiles,
                    dest_ref, gt_ref, h_ref, mod_ref, vec_ref, yb_ref, o_ref, buf, sem):
    bi = pl.program_id(0)
    i = pl.program_id(1)
    tile = bi * n_tiles + i

    def issue(t, c):
        for j in range(TOP_K):
            _row_copy(yb_ref, dest_ref[tile, t * TOP_K + j], buf.at[j], t, sem).start()
        return c

    lax.fori_loop(0, ROW_TILE, issue, 0)

    def drain(t, c):
        for j in range(TOP_K):
            _row_copy(yb_ref, 0, buf.at[j], 0, sem).wait()
        return c

    lax.fori_loop(0, ROW_TILE, drain, 0)
    gt = gt_ref[0]
    f = gt[:, 0:1] * buf[0]
    for j in range(1, TOP_K):
        f = f + gt[:, j:j + 1] * buf[j]
    g2 = mod_ref[0, 0, 0:1, :]
    o_ref[0] = _ln(alpha * h_ref[0] + g2 * f, vec_ref[0:1, :], vec_ref[1:2, :], LN_EPS)


def _moe_combine(alpha, n_ctx_tiles, tile0, dest2d, gates, h1, mod3, vec, yb):
    bsz, rows, d = h1.shape
    nt = rows // ROW_TILE
    row = lambda b_, i, dst: (b_, i, 0)
    return pl.pallas_call(
        functools.partial(_combine_kernel, alpha, nt),
        out_shape=jax.ShapeDtypeStruct((bsz, rows, d), F32),
        grid_spec=pltpu.PrefetchScalarGridSpec(
            num_scalar_prefetch=1,
            grid=(bsz, nt),
            in_specs=[pl.BlockSpec((1, ROW_TILE, LANES), row),
                      pl.BlockSpec((1, ROW_TILE, d), row),
                      pl.BlockSpec((1, 1, 1, d),
                                   lambda b_, i, dst: (b_, jnp.where(tile0 + i < n_ctx_tiles, 0, 1), 0, 0)),
                      pl.BlockSpec((2, d), lambda b_, i, dst: (0, 0)),
                      pl.BlockSpec(memory_space=pl.ANY)],
            out_specs=pl.BlockSpec((1, ROW_TILE, d), row),
            scratch_shapes=[pltpu.VMEM((TOP_K, ROW_TILE, d), F32), pltpu.SemaphoreType.DMA(())]),
        compiler_params=_cparams(("arbitrary", "arbitrary")),
        name="moe_combine",
    )(dest2d, gates, h1, mod3, vec, yb)


def _moe(alpha, n_ctx_tiles, tile0, h1, m, ti, gates, mod3, ln2, router_unused, wg, bg, wu, bu, wd, bd):
    del router_unused
    bsz, rows, d = h1.shape
    n = bsz * rows
    dest, meta = _moe_rank(ti.reshape(n, LANES))
    dest2d = dest[:, :TOP_K].reshape(n // ROW_TILE, ROW_TILE * TOP_K)
    n_blocks = -(-(n * TOP_K) // MOE_BLOCK) + N_EXPERTS
    cap = n_blocks * MOE_BLOCK
    pad_end = (meta[1, :N_EXPERTS] + meta[2, :N_EXPERTS]).astype(jnp.int32)
    block_expert = jnp.minimum(
        jnp.searchsorted(pad_end, jnp.arange(n_blocks, dtype=jnp.int32) * MOE_BLOCK, side="right"),
        N_EXPERTS - 1).astype(jnp.int32)
    n_used = (pad_end[-1:] // MOE_BLOCK).astype(jnp.int32)
    xb = _moe_dispatch(dest2d, m.reshape(n, d), cap)
    yb = _moe_experts(block_expert, n_used, xb, wg, bg, wu, bu, wd, bd)
    return _moe_combine(alpha, n_ctx_tiles, tile0, dest2d, gates, h1, mod3, ln2, yb)


def kernel(x, c, ctx, c_ctx, ln_in_g, ln_in_b, ada_w, ada_b, w_in, lam_q1, lam_k1, lam_q2, lam_k2,
           da_subln_g, gla_gate_w2, gla_gate_b, gla_norm_g, rw_conv_w, rw_w2, rw_w0, rw_a2, rw_a0,
           rw_g2, rw_k_k, rw_k_a, rw_r_k, rw_lnx_g, rw_lnx_b, w_out, ln1_g, ln1_b, router_w, router_b,
           moe_w_gate, moe_b_gate, moe_w_up, moe_b_up, moe_w_down, moe_b_down, ln2_g, ln2_b):
    bsz, seq, d = x.shape
    n_ctx = ctx.shape[1]
    depth = w_in.shape[0]
    assert n_ctx % ROW_TILE == 0 and seq % ROW_TILE == 0 and seq % GRID_W == 0
    assert w_in.shape[2] == 3488 and bsz + 1 <= 8
    nct = n_ctx // ROW_TILE
    ncc = n_ctx // CHUNK
    nt = (n_ctx + seq) // ROW_TILE
    alpha = (2 * depth) ** 0.25

    c_all = jnp.concatenate([c, c_ctx[None], jnp.zeros((8 - bsz - 1, d), F32)], axis=0)
    mods = _ada_mod(c_all, ada_w, ada_b).reshape(depth, 8, 6, d)
    h = _ln_in(jnp.concatenate([ctx, x], axis=1), ln_in_g, ln_in_b)
    cos_t, sin_t = _rope_tables(n_ctx, seq)

    def pick(l, idx):
        mc = jnp.broadcast_to(mods[l, bsz][None, idx], (bsz, len(idx), d))
        return jnp.stack([mc, mods[l, :bsz][:, idx]], axis=1)

    for l in range(depth):
        last = l == depth - 1
        tile0 = nct if last else 0
        n_out_tiles = nt - tile0
        lam_init = 0.8 - 0.6 * math.exp(-0.3 * l)
        lam = (jnp.exp(jnp.sum(lam_q1[l] * lam_k1[l])) - jnp.exp(jnp.sum(lam_q2[l] * lam_k2[l])) + lam_init)

        qk, v, zg, zr = _inproj(h, pick(l, [1, 0]), _pack_w_in(w_in[l]), cos_t, sin_t, nct)
        ao = _attention(qk, v, lam, nct, nt - nct, n_ctx + seq)
        if not last:
            ao = jnp.concatenate([_attention(qk, v, lam, 0, nct, n_ctx), ao], axis=1)
        w2p, biasp = _pack_gla_gate(gla_gate_w2[l], gla_gate_b[l])
        gof, gob = _gla_scan(zg, w2p, biasp, ncc)
        rvec = jnp.stack([rw_k_k[l], rw_k_a[l], rw_r_k[l].reshape(-1), rw_w0[l, 0], rw_w0[l, 1],
                          rw_a0[l, 0], rw_a0[l, 1], jnp.zeros((RW_W,), F32)], axis=0)
        shared, dfw, dbw, rg, bonus = _rw_prep(zr, rw_conv_w[l], _block2(rw_w2[l]), _block2(rw_a2[l]),
                                               rw_g2[l], rvec, nct)
        yf, yb = _rw_scan(shared, dfw, dbw, ncc)

        nrm = jnp.stack([jnp.pad(jnp.tile(da_subln_g[l], DA_HEADS), (0, GLA_W - DA_W)),
                         jnp.tile(gla_norm_g[l], GLA_HEADS), rw_lnx_g[l], rw_lnx_b[l]], axis=0)
        rw_p = jnp.pad(router_w[l], ((0, 0), (0, LANES - N_EXPERTS)))
        rb_p = jnp.pad(router_b[l], (0, LANES - N_EXPERTS), constant_values=-1e30).reshape(1, LANES)
        h1, m, ti, gates = _mixout(alpha, 1.0 - lam_init, tile0, n_out_tiles, nct,
                                   ao, gof, gob, zg, yf, yb, bonus, rg, h, pick(l, [2, 4, 3]),
                                   w_out[l].astype(BF16), jnp.stack([ln1_g[l], ln1_b[l]], 0), nrm, rw_p, rb_p)
        h = _moe(alpha, nct, tile0, h1, m, ti, gates, pick(l, [5]), jnp.stack([ln2_g[l], ln2_b[l]], 0), None,
                 moe_w_gate[l], moe_b_gate[l], moe_w_up[l], moe_b_up[l], moe_w_down[l], moe_b_down[l])
    return h
```

```python
import functools
import math

import jax
import jax.numpy as jnp
import numpy as np
from jax import lax
from jax.experimental import pallas as pl
from jax.experimental.pallas import tpu as pltpu

F32 = jnp.float32
BF16 = jnp.bfloat16
HI = lax.Precision.HIGHEST

GRID_W = 64
DA_HEADS, DA_QK, DA_V = 4, 32, 64
GLA_HEADS, GLA_DK, GLA_DV, GLA_RANK, GLA_TAU = 6, 32, 64, 16, 16.0
RW_HEADS, RW_D, RW_DECAY_RANK, RW_A_RANK, RW_GATE_RANK = 6, 64, 64, 64, 128
RW_GN_EPS = 64e-5
N_EXPERTS, TOP_K = 32, 4
SWIGLU_LIMIT, SWIGLU_ALPHA = 7.0, 1.702
ROPE_BASE = 10000.0
LN_EPS = 1e-5

DA_W = DA_HEADS * DA_V
GLA_KW = GLA_HEADS * GLA_DK
GLA_W = GLA_HEADS * GLA_DV
RW_W = RW_HEADS * RW_D
MIX_W = DA_W + GLA_W + RW_W

LANES = 128
ROW_TILE = 256
CHUNK = 64
MOE_BLOCK = 256
VMEM_LIMIT = 56 * 1024 * 1024

Z_ATTN = 3 * DA_W
GLA_KP = 256
Z_GLA = 2 * GLA_KP + 2 * GLA_W + LANES
Z_RW = 3 * RW_W + 3 * LANES
Z_ALL = Z_ATTN + Z_GLA + Z_RW


def _cparams(sem):
    return pltpu.CompilerParams(dimension_semantics=sem, vmem_limit_bytes=VMEM_LIMIT)


def _ln(x, g, b, eps):
    mu = jnp.mean(x, axis=-1, keepdims=True)
    xc = x - mu
    var = jnp.mean(xc * xc, axis=-1, keepdims=True)
    return xc * lax.rsqrt(var + eps) * g + b


def _silu(x):
    return x * jax.nn.sigmoid(x)


def _dot(a, b):
    return jnp.dot(a.astype(BF16), b.astype(BF16), preferred_element_type=F32)


def _dot_hi(a, b):
    return jnp.dot(a, b, precision=HI, preferred_element_type=F32)


def _dot_nt(a, b):
    return lax.dot_general(a.astype(BF16), b.astype(BF16), (((1,), (1,)), ((), ())),
                           preferred_element_type=F32)


def _group_sum(x, gmat):
    hi = x.astype(BF16)
    lo = (x - hi.astype(F32)).astype(BF16)
    return (jnp.dot(hi, gmat, preferred_element_type=F32)
            + jnp.dot(lo, gmat, preferred_element_type=F32))


def _group_matrix(width, group):
    idx = np.arange(width) // group
    return jnp.asarray((idx[:, None] == idx[None, :]).astype(np.float32), dtype=BF16)


def _ada_kernel(c_ref, w_ref, b_ref, o_ref):
    o_ref[0] = _dot_hi(_silu(c_ref[...]), w_ref[0]) + b_ref[0]


def _ada_mod(c_all, ada_w, ada_b):
    nl, d, n6 = ada_w.shape
    tn = 1536
    return pl.pallas_call(
        _ada_kernel,
        out_shape=jax.ShapeDtypeStruct((nl, c_all.shape[0], n6), F32),
        grid=(nl, n6 // tn),
        in_specs=[pl.BlockSpec((c_all.shape[0], d), lambda l, j: (0, 0)),
                  pl.BlockSpec((1, d, tn), lambda l, j: (l, 0, j)),
                  pl.BlockSpec((1, 1, tn), lambda l, j: (l, 0, j))],
        out_specs=pl.BlockSpec((1, c_all.shape[0], tn), lambda l, j: (l, 0, j)),
        compiler_params=_cparams(("arbitrary", "arbitrary")),
        name="ada_mod",
    )(c_all, ada_w, ada_b.reshape(nl, 1, n6))


def _ln_in_kernel(x_ref, g_ref, b_ref, o_ref):
    o_ref[0] = _ln(x_ref[0], g_ref[...], b_ref[...], LN_EPS)


def _ln_in(xa, g, b):
    bsz, lt, d = xa.shape
    return pl.pallas_call(
        _ln_in_kernel,
        out_shape=jax.ShapeDtypeStruct(xa.shape, F32),
        grid=(bsz, lt // ROW_TILE),
        in_specs=[pl.BlockSpec((1, ROW_TILE, d), lambda b_, i: (b_, i, 0)),
                  pl.BlockSpec((1, d), lambda b_, i: (0, 0)),
                  pl.BlockSpec((1, d), lambda b_, i: (0, 0))],
        out_specs=pl.BlockSpec((1, ROW_TILE, d), lambda b_, i: (b_, i, 0)),
        compiler_params=_cparams(("arbitrary", "arbitrary")),
        name="ln_in",
    )(xa, g.reshape(1, d), b.reshape(1, d))


def _inproj_kernel(h_ref, mod_ref, w_ref, cos_ref, sin_ref, qk_ref, v_ref, gla_ref, rw_ref):
    h = h_ref[0]
    sc = mod_ref[0, 0, 0:1, :]
    sh = mod_ref[0, 0, 1:2, :]
    xm = (h * (1.0 + sc) + sh).astype(BF16)
    qk = jnp.dot(xm, w_ref[:, 0:2 * DA_W], preferred_element_type=F32)
    cos = cos_ref[...]
    sin = sin_ref[...]
    lane = lax.broadcasted_iota(jnp.int32, (1, LANES), 1)
    first = (lane % 16) < 8
    qscale = DA_QK ** -0.5
    for j in range(4):
        x = qk[:, j * LANES:(j + 1) * LANES]
        rot = jnp.where(first, pltpu.roll(x, LANES - 8, 1), pltpu.roll(x, 8, 1))
        y = x * cos + rot * sin
        if j < 2:
            y = y * qscale
        qk_ref[0, :, j * LANES:(j + 1) * LANES] = y.astype(BF16)
    v_ref[0] = jnp.dot(xm, w_ref[:, 2 * DA_W:Z_ATTN], preferred_element_type=F32).astype(BF16)
    gla_ref[0] = jnp.dot(xm, w_ref[:, Z_ATTN:Z_ATTN + Z_GLA], preferred_element_type=F32)
    rw_ref[0] = jnp.dot(xm, w_ref[:, Z_ATTN + Z_GLA:Z_ALL], preferred_element_type=F32)


def _inproj(h, mod1, w_p, cos_t, sin_t, n_ctx_tiles):
    bsz, lt, d = h.shape
    nt = lt // ROW_TILE
    row = lambda b_, i: (b_, i, 0)
    return pl.pallas_call(
        _inproj_kernel,
        out_shape=(jax.ShapeDtypeStruct((bsz, lt, 2 * DA_W), BF16),
                   jax.ShapeDtypeStruct((bsz, lt, DA_W), BF16),
                   jax.ShapeDtypeStruct((bsz, lt, Z_GLA), F32),
                   jax.ShapeDtypeStruct((bsz, lt, Z_RW), F32)),
        grid=(bsz, nt),
        in_specs=[pl.BlockSpec((1, ROW_TILE, d), row),
                  pl.BlockSpec((1, 1, 2, d),
                               lambda b_, i: (b_, jnp.where(i < n_ctx_tiles, 0, 1), 0, 0)),
                  pl.BlockSpec((d, Z_ALL), lambda b_, i: (0, 0)),
                  pl.BlockSpec((ROW_TILE, LANES), lambda b_, i: (i, 0)),
                  pl.BlockSpec((ROW_TILE, LANES), lambda b_, i: (i, 0))],
        out_specs=(pl.BlockSpec((1, ROW_TILE, 2 * DA_W), row),
                   pl.BlockSpec((1, ROW_TILE, DA_W), row),
                   pl.BlockSpec((1, ROW_TILE, Z_GLA), row),
                   pl.BlockSpec((1, ROW_TILE, Z_RW), row)),
        compiler_params=_cparams(("arbitrary", "arbitrary")),
        name="inproj",
    )(h, mod1, w_p, cos_t, sin_t)


def _pack_w_in(w):
    d = w.shape[0]
    o = 0
    parts = {}
    for name, n in (("da_q", 256), ("da_k", 256), ("da_v", 256), ("gla_q", GLA_KW), ("gla_k", GLA_KW),
                    ("gla_v", GLA_W), ("gla_gf", GLA_RANK), ("gla_gb", GLA_RANK), ("gla_r", GLA_W),
                    ("rw_rkv", 3 * RW_W), ("rw_wf", 64), ("rw_wb", 64), ("rw_af", 64), ("rw_ab", 64),
                    ("rw_g", 128)):
        parts[name] = w[:, o:o + n]
        o += n
    z = lambda n: jnp.zeros((d, n), w.dtype)
    cols = [parts["da_q"], parts["da_k"], parts["da_v"],
            parts["gla_q"], z(GLA_KP - GLA_KW), parts["gla_k"], z(GLA_KP - GLA_KW),
            parts["gla_v"], parts["gla_r"], parts["gla_gf"], parts["gla_gb"], z(LANES - 2 * GLA_RANK),
            parts["rw_rkv"], parts["rw_wf"], parts["rw_wb"], parts["rw_af"], parts["rw_ab"], parts["rw_g"]]
    return jnp.concatenate(cols, axis=1).astype(BF16)


def _rope_tables(n_ctx, seq):
    t = np.arange(seq)
    row = (t // GRID_W).astype(np.float32)
    col = (t % GRID_W).astype(np.float32)
    quarter = DA_QK // 4
    inv = (ROPE_BASE ** (-np.arange(quarter, dtype=np.float32) / quarter)).astype(np.float32)
    ang_r = row[:, None] * inv
    ang_c = col[:, None] * inv
    ang = np.concatenate([ang_r, ang_r, ang_c, ang_c], -1).astype(np.float32)
    cos = np.tile(np.cos(ang), (1, LANES // DA_QK))
    sin = np.tile(np.sin(ang), (1, LANES // DA_QK))
    sign = np.where((np.arange(LANES) % 16) < 8, -1.0, 1.0).astype(np.float32)
    cos = np.concatenate([np.ones((n_ctx, LANES), np.float32), cos], 0)
    sin = np.concatenate([np.zeros((n_ctx, LANES), np.float32), sin * sign], 0)
    return jnp.asarray(cos, F32), jnp.asarray(sin, F32)


def _attn_kernel(lam_ref, q_ref, k_ref, v_ref, o_ref):
    q = q_ref[0]
    k = k_ref[0]
    v = v_ref[0]
    lam = lam_ref[0, 0]
    lane = lax.broadcasted_iota(jnp.int32, (1, LANES), 1)
    outs = []
    for hh in range(2):
        acc = None
        for m in range(2):
            lo = hh * 64 + m * DA_QK
            qm = jnp.where((lane >= lo) & (lane < lo + DA_QK), q, jnp.zeros_like(q))
            s = _dot_nt(qm, k)
            e = jnp.exp(s - jnp.max(s, axis=-1, keepdims=True))
            den = jnp.sum(e, axis=-1, keepdims=True)
            pv = jnp.dot(e.astype(BF16), v, preferred_element_type=F32) / den
            acc = pv if m == 0 else acc - lam * pv
        outs.append(acc)
    o_ref[0] = jnp.where(lane < 64, outs[0], outs[1])


def _attention(qk, v, lam, q_tile0, n_q_tiles, n_k_rows):
    bsz, lt, _ = qk.shape
    out = pl.pallas_call(
        _attn_kernel,
        out_shape=jax.ShapeDtypeStruct((bsz, n_q_tiles * ROW_TILE, DA_W), F32),
        grid=(bsz, 2, n_q_tiles),
        in_specs=[pl.BlockSpec(memory_space=pltpu.SMEM),
                  pl.BlockSpec((1, ROW_TILE, LANES), lambda b_, p, i: (b_, q_tile0 + i, p)),
                  pl.BlockSpec((1, n_k_rows, LANES), lambda b_, p, i: (b_, 0, 2 + p)),
                  pl.BlockSpec((1, n_k_rows, LANES), lambda b_, p, i: (b_, 0, p))],
        out_specs=pl.BlockSpec((1, ROW_TILE, LANES), lambda b_, p, i: (b_, i, p)),
        compiler_params=_cparams(("arbitrary", "arbitrary", "arbitrary")),
        name="diff_attn",
    )(lam.reshape(1, 1), qk, qk, v)
    return out


def _mixout_kernel(alpha, sub_scale,
                   ao_ref, gof_ref, gob_ref, zg_ref, yf_ref, yb_ref, bonus_ref, rg_ref, h_ref, mod_ref,
                   wo_ref, vec_ref, nrm_ref, g256_ref, g384_ref, rw_ref, rb_ref,
                   h1_ref, m_ref, ti_ref, gt_ref):
    a = ao_ref[0]
    a = a * lax.rsqrt(_group_sum(a * a, g256_ref[...]) * (1.0 / DA_V) + LN_EPS) * nrm_ref[0:1, 0:DA_W] * sub_scale
    o = gof_ref[0] + gob_ref[0]
    r = zg_ref[0, :, 2 * GLA_KP + GLA_W:2 * GLA_KP + 2 * GLA_W]
    gl = (o * lax.rsqrt(_group_sum(o * o, g384_ref[...]) * (1.0 / GLA_DV) + LN_EPS)
          * nrm_ref[1:2, :] * _silu(r))
    y = yf_ref[0] + yb_ref[0]
    mu = _group_sum(y, g384_ref[...]) * (1.0 / RW_D)
    yc = y - mu
    var = _group_sum(yc * yc, g384_ref[...]) * (1.0 / RW_D)
    yn = yc * lax.rsqrt(var + RW_GN_EPS) * nrm_ref[2:3, :] + nrm_ref[3:4, :]
    rw = (yn + bonus_ref[0]) * rg_ref[0]
    mix = (_dot(a, wo_ref[0:DA_W, :]) + _dot(gl, wo_ref[DA_W:DA_W + GLA_W, :])
           + _dot(rw, wo_ref[DA_W + GLA_W:MIX_W, :]))
    g1 = mod_ref[0, 0, 0:1, :]
    sc2 = mod_ref[0, 0, 1:2, :]
    sh2 = mod_ref[0, 0, 2:3, :]
    h1 = _ln(alpha * h_ref[0] + g1 * mix, vec_ref[0:1, :], vec_ref[1:2, :], LN_EPS)
    h1_ref[0] = h1
    m = h1 * (1.0 + sc2) + sh2
    m_ref[0] = m
    logits = _dot_hi(m, rw_ref[...]) + rb_ref[...]
    lane = lax.broadcasted_iota(jnp.int32, logits.shape, 1)
    ti = jnp.zeros(logits.shape, jnp.int32)
    tv = jnp.full(logits.shape, -1e30, F32)
    for j in range(TOP_K):
        mx = jnp.max(logits, axis=-1, keepdims=True)
        idx = jnp.min(jnp.where(logits == mx, lane, LANES), axis=-1, keepdims=True)
        ti = jnp.where(lane == j, idx, ti)
        tv = jnp.where(lane == j, mx, tv)
        logits = jnp.where(lane == idx, -jnp.inf, logits)
    e = jnp.exp(tv - jnp.max(tv, axis=-1, keepdims=True))
    ti_ref[0] = ti
    gt_ref[0] = e / jnp.sum(e, axis=-1, keepdims=True)


def _mixout(alpha, sub_scale, tile0, n_tiles, n_ctx_tiles,
            ao, gof, gob, zg, yf, yb, bonus, rg, h, mod2, wo, vec, nrm, rw_p, rb_p):
    bsz, lt, d = h.shape
    row = lambda b_, i: (b_, tile0 + i, 0)
    orow = lambda b_, i: (b_, i, 0)
    full = lambda shape: pl.BlockSpec(shape, lambda b_, i: tuple(0 for _ in shape))
    nrows = n_tiles * ROW_TILE
    return pl.pallas_call(
        functools.partial(_mixout_kernel, alpha, sub_scale),
        out_shape=(jax.ShapeDtypeStruct((bsz, nrows, d), F32),
                   jax.ShapeDtypeStruct((bsz, nrows, d), F32),
                   jax.ShapeDtypeStruct((bsz, nrows, LANES), jnp.int32),
                   jax.ShapeDtypeStruct((bsz, nrows, LANES), F32)),
        grid=(bsz, n_tiles),
        in_specs=[pl.BlockSpec((1, ROW_TILE, DA_W), orow),
                  pl.BlockSpec((1, ROW_TILE, GLA_W), row),
                  pl.BlockSpec((1, ROW_TILE, GLA_W), row),
                  pl.BlockSpec((1, ROW_TILE, Z_GLA), row),
                  pl.BlockSpec((1, ROW_TILE, RW_W), row),
                  pl.BlockSpec((1, ROW_TILE, RW_W), row),
                  pl.BlockSpec((1, ROW_TILE, RW_W), row),
                  pl.BlockSpec((1, ROW_TILE, RW_W), row),
                  pl.BlockSpec((1, ROW_TILE, d), row),
                  pl.BlockSpec((1, 1, 3, d),
                               lambda b_, i: (b_, jnp.where(tile0 + i < n_ctx_tiles, 0, 1), 0, 0)),
                  full((MIX_W, d)), full((2, d)), full((4, GLA_W)),
                  full((DA_W, DA_W)), full((GLA_W, GLA_W)), full((d, LANES)), full((1, LANES))],
        out_specs=(pl.BlockSpec((1, ROW_TILE, d), orow),
                   pl.BlockSpec((1, ROW_TILE, d), orow),
                   pl.BlockSpec((1, ROW_TILE, LANES), orow),
                   pl.BlockSpec((1, ROW_TILE, LANES), orow)),
        compiler_params=_cparams(("arbitrary", "arbitrary")),
        name="mix_out",
    )(ao, gof, gob, zg, yf, yb, bonus, rg, h, mod2, wo, vec, nrm,
      _group_matrix(DA_W, DA_V), _group_matrix(GLA_W, GLA_DV), rw_p, rb_p)


def _bwd_chunk(step, n_ctx_chunks, n_chunks):
    return jnp.where(step < n_ctx_chunks, n_ctx_chunks - 1 - step, n_chunks + n_ctx_chunks - 1 - step)


def _tri(n, reverse, strict):
    t = lax.broadcasted_iota(jnp.int32, (n, n), 0)
    s = lax.broadcasted_iota(jnp.int32, (n, n), 1)
    if reverse:
        return (s > t) if strict else (s >= t)
    return (s < t) if strict else (s <= t)


def _dot_tn(a, b):
    return lax.dot_general(a, b, (((0,), (0,)), ((), ())), precision=HI, preferred_element_type=F32)


def _dot_nt_hi(a, b):
    return lax.dot_general(a, b, (((1,), (1,)), ((), ())), precision=HI, preferred_element_type=F32)


def _log_sigmoid(x):
    return jnp.minimum(x, 0.0) - jnp.log1p(jnp.exp(-jnp.abs(x)))


def _softplus(x):
    return jnp.maximum(x, 0.0) + jnp.log1p(jnp.exp(-jnp.abs(x)))


def _split3(x):
    t0 = x.astype(BF16)
    r1 = x - t0.astype(F32)
    t1 = r1.astype(BF16)
    t2 = (r1 - t1.astype(F32)).astype(BF16)
    return t0, t1, t2


def _cumsum_rows(x, reverse):
    tri = jnp.where(_tri(CHUNK, reverse, False), 1.0, 0.0).astype(BF16)
    return sum(jnp.dot(tri, t, preferred_element_type=F32) for t in _split3(x))


def _bdot(a, b):
    return jnp.dot(a, b, preferred_element_type=F32)


def _bdot_nt(a, b):
    return lax.dot_general(a, b, (((1,), (1,)), ((), ())), preferred_element_type=F32)


def _bdot_tn(a, b):
    return lax.dot_general(a, b, (((0,), (0,)), ((), ())), preferred_element_type=F32)


GLA_SUB = 16
GLA_EXP_CLAMP = 60.0


def _gla_dir(zg, w2_ref, bias_ref, st_ref, reverse, col0):
    q = zg[:, 0:GLA_KP] * (GLA_DK ** -0.5)
    k = zg[:, GLA_KP:2 * GLA_KP]
    v = zg[:, 2 * GLA_KP:2 * GLA_KP + GLA_W]
    gpre = _dot(zg[:, 2 * GLA_KP + 2 * GLA_W:Z_GLA], w2_ref[:, col0:col0 + GLA_KP]) \
        + bias_ref[:, col0:col0 + GLA_KP]
    g = _log_sigmoid(gpre) * (1.0 / GLA_TAU)
    b = _cumsum_rows(g, reverse)
    last = 0 if reverse else CHUNK - 1
    b_last = b[last:last + 1, :]
    st = st_ref[...]
    inter = _dot_nt(q * jnp.exp(b), st)
    kv = _bdot_tn(v.astype(BF16), (k * jnp.exp(b_last - b)).astype(BF16))
    vi = lax.broadcasted_iota(jnp.int32, (GLA_W, GLA_KP), 0) // GLA_DV
    ki = lax.broadcasted_iota(jnp.int32, (GLA_W, GLA_KP), 1) // GLA_DK
    st_ref[...] = st * jnp.exp(b_last) + jnp.where(vi == ki, kv, 0.0)

    lane_head = lax.broadcasted_iota(jnp.int32, (1, GLA_KP), 1) // GLA_DK
    nsub = CHUNK // GLA_SUB
    per_head = [[None] * nsub for _ in range(GLA_HEADS)]
    for i in range(nsub):
        r0 = i * GLA_SUB
        ref_row = r0 + GLA_SUB - 1 if reverse else r0
        b_ref = b[ref_row:ref_row + 1, :]
        qi = q[r0:r0 + GLA_SUB] * jnp.exp(b[r0:r0 + GLA_SUB] - b_ref)
        ki_ = k * jnp.exp(jnp.minimum(b_ref - b, GLA_EXP_CLAMP))
        qh = jnp.concatenate([jnp.where(lane_head == h, qi, 0.0) for h in range(GLA_HEADS)], axis=0)
        a = _dot_nt(qh, ki_)
        for h in range(GLA_HEADS):
            per_head[h][i] = a[h * GLA_SUB:(h + 1) * GLA_SUB]
    causal = _tri(CHUNK, reverse, False)
    lane = lax.broadcasted_iota(jnp.int32, (1, LANES), 1)
    pieces = []
    for p in range(GLA_HEADS // 2):
        vp = v[:, p * LANES:(p + 1) * LANES]
        halves = []
        for hh in range(2):
            a_h = jnp.where(causal, jnp.concatenate(per_head[2 * p + hh], axis=0), 0.0)
            halves.append(_dot(a_h, vp))
        pieces.append(jnp.where(lane < GLA_DV, halves[0], halves[1]))
    return inter + jnp.concatenate(pieces, axis=1)


def _gla_kernel(zf_ref, zb_ref, w2_ref, bias_ref, of_ref, ob_ref, sf_ref, sb_ref):
    @pl.when(pl.program_id(1) == 0)
    def _():
        sf_ref[...] = jnp.zeros_like(sf_ref)
        sb_ref[...] = jnp.zeros_like(sb_ref)

    of_ref[0] = _gla_dir(zf_ref[0], w2_ref, bias_ref, sf_ref, False, 0)
    ob_ref[0] = _gla_dir(zb_ref[0], w2_ref, bias_ref, sb_ref, True, GLA_KP)


def _gla_scan(zg, w2p, biasp, n_ctx_chunks):
    bsz, lt, _ = zg.shape
    nc = lt // CHUNK
    fwd = lambda b_, s: (b_, s, 0)
    bwd = lambda b_, s: (b_, _bwd_chunk(s, n_ctx_chunks, nc), 0)
    return pl.pallas_call(
        _gla_kernel,
        out_shape=(jax.ShapeDtypeStruct((bsz, lt, GLA_W), F32),
                   jax.ShapeDtypeStruct((bsz, lt, GLA_W), F32)),
        grid=(bsz, nc),
        in_specs=[pl.BlockSpec((1, CHUNK, Z_GLA), fwd),
                  pl.BlockSpec((1, CHUNK, Z_GLA), bwd),
                  pl.BlockSpec((LANES, 2 * GLA_KP), lambda b_, s: (0, 0)),
                  pl.BlockSpec((1, 2 * GLA_KP), lambda b_, s: (0, 0))],
        out_specs=(pl.BlockSpec((1, CHUNK, GLA_W), fwd),
                   pl.BlockSpec((1, CHUNK, GLA_W), bwd)),
        scratch_shapes=[pltpu.VMEM((GLA_W, GLA_KP), F32), pltpu.VMEM((GLA_W, GLA_KP), F32)],
        compiler_params=_cparams(("arbitrary", "arbitrary")),
        name="gla_scan",
    )(zg, zg, w2p, biasp)


def _pack_gla_gate(w2, bias):
    w = jnp.zeros((LANES, 2 * GLA_KP), F32)
    w = w.at[0:GLA_RANK, 0:GLA_KW].set(w2[0]).at[GLA_RANK:2 * GLA_RANK, GLA_KP:GLA_KP + GLA_KW].set(w2[1])
    b = jnp.zeros((1, 2 * GLA_KP), F32)
    b = b.at[0, 0:GLA_KW].set(bias[0]).at[0, GLA_KP:GLA_KP + GLA_KW].set(bias[1])
    return w, b


def _rwprep_kernel(n_ctx_tiles, n_tiles,
                   z_ref, zp_ref, zn_ref, cw_ref, w2_ref, a2_ref, g2_ref, vec_ref, gm_ref,
                   sh_ref, df_ref, db_ref, g_ref, bonus_ref):
    i = pl.program_id(1)
    z = z_ref[0]
    x = z[:, 0:3 * RW_W]
    seg_first = (i == 0) | (i == n_ctx_tiles)
    seg_last = (i == n_ctx_tiles - 1) | (i == n_tiles - 1)
    prev_row = jnp.where(seg_first, 0.0, zp_ref[0, 7:8, 0:3 * RW_W])
    next_row = jnp.where(seg_last, 0.0, zn_ref[0, 0:1, 0:3 * RW_W])
    ridx = lax.broadcasted_iota(jnp.int32, (ROW_TILE, 1), 0)
    x_prev = jnp.where(ridx == 0, prev_row, pltpu.roll(x, 1, 0))
    x_next = jnp.where(ridx == ROW_TILE - 1, next_row, pltpu.roll(x, ROW_TILE - 1, 0))
    xc = x_prev * cw_ref[0:1, :] + x * cw_ref[1:2, :] + x_next * cw_ref[2:3, :]
    r = xc[:, 0:RW_W]
    k = xc[:, RW_W:2 * RW_W]
    v = xc[:, 2 * RW_W:3 * RW_W]
    gm = gm_ref[...]
    kk = k * vec_ref[0:1, :]
    kk = kk / jnp.maximum(jnp.sqrt(_group_sum(kk * kk, gm)), 1e-12)
    k_a = vec_ref[1:2, :]
    r_k = vec_ref[2:3, :]
    w_raw = _dot_hi(jnp.tanh(z[:, 3 * RW_W:3 * RW_W + LANES]), w2_ref[...])
    a_raw = _dot_hi(z[:, 3 * RW_W + LANES:3 * RW_W + 2 * LANES], a2_ref[...])
    g_ref[0] = _dot_hi(jax.nn.sigmoid(z[:, 3 * RW_W + 2 * LANES:Z_RW]), g2_ref[...])
    sh_ref[0, :, 0:RW_W] = r
    sh_ref[0, :, RW_W:2 * RW_W] = v
    sh_ref[0, :, 2 * RW_W:3 * RW_W] = kk
    rk_sum = None
    for d, d_ref in enumerate((df_ref, db_ref)):
        wr = w_raw[:, d * RW_W:(d + 1) * RW_W] + vec_ref[3 + d:4 + d, :]
        logw = -jnp.exp(-_softplus(-wr) - 0.5)
        a = jax.nn.sigmoid(a_raw[:, d * RW_W:(d + 1) * RW_W] + vec_ref[5 + d:6 + d, :])
        k_mod = k * (1.0 + (a - 1.0) * k_a)
        d_ref[0, :, 0:RW_W] = logw
        d_ref[0, :, RW_W:2 * RW_W] = kk * a
        d_ref[0, :, 2 * RW_W:3 * RW_W] = k_mod
        s = _group_sum(r * k_mod * r_k, gm)
        rk_sum = s if d == 0 else rk_sum + s
    bonus_ref[0] = rk_sum * v


def _rw_prep(zr, cw, w2p, a2p, g2, vec, n_ctx_tiles):
    bsz, lt, _ = zr.shape
    nt = lt // ROW_TILE
    hb = ROW_TILE // 8
    row = lambda b_, i: (b_, i, 0)
    full = lambda shape: pl.BlockSpec(shape, lambda b_, i: tuple(0 for _ in shape))
    o3 = jax.ShapeDtypeStruct((bsz, lt, 3 * RW_W), F32)
    o1 = jax.ShapeDtypeStruct((bsz, lt, RW_W), F32)
    return pl.pallas_call(
        functools.partial(_rwprep_kernel, n_ctx_tiles, nt),
        out_shape=(o3, o3, o3, o1, o1),
        grid=(bsz, nt),
        in_specs=[pl.BlockSpec((1, ROW_TILE, Z_RW), row),
                  pl.BlockSpec((1, 8, Z_RW), lambda b_, i: (b_, jnp.maximum(i * hb - 1, 0), 0)),
                  pl.BlockSpec((1, 8, Z_RW), lambda b_, i: (b_, jnp.minimum((i + 1) * hb, nt * hb - 1), 0)),
                  full((3, 3 * RW_W)), full((LANES, 2 * RW_W)), full((LANES, 2 * RW_W)),
                  full((RW_GATE_RANK, RW_W)), full((8, RW_W)), full((RW_W, RW_W))],
        out_specs=(pl.BlockSpec((1, ROW_TILE, 3 * RW_W), row),
                   pl.BlockSpec((1, ROW_TILE, 3 * RW_W), row),
                   pl.BlockSpec((1, ROW_TILE, 3 * RW_W), row),
                   pl.BlockSpec((1, ROW_TILE, RW_W), row),
                   pl.BlockSpec((1, ROW_TILE, RW_W), row)),
        compiler_params=_cparams(("arbitrary", "arbitrary")),
        name="rwkv_prep",
    )(zr, zr, zr, cw, w2p, a2p, g2, vec, _group_matrix(RW_W, RW_D))


def _block2(w):
    r, n = w.shape[1:]
    z = jnp.zeros((r, n), w.dtype)
    return jnp.concatenate([jnp.concatenate([w[0], z], 1), jnp.concatenate([z, w[1]], 1)], 0)


def _bmm(a, b):
    return lax.dot_general(a, b, (((2,), (1,)), ((0,), (0,))), preferred_element_type=F32)


def _bmm_nt(a, b):
    return lax.dot_general(a, b, (((2,), (2,)), ((0,), (0,))), preferred_element_type=F32)


def _bmm_tn(a, b):
    return lax.dot_general(a, b, (((1,), (1,)), ((0,), (0,))), preferred_element_type=F32)


def _rw_operands(sh, dd, reverse):
    c = CHUNK
    r, v, kk = sh[:, 0:RW_W], sh[:, RW_W:2 * RW_W], sh[:, 2 * RW_W:3 * RW_W]
    logw, beta, k = dd[:, 0:RW_W], dd[:, RW_W:2 * RW_W], dd[:, 2 * RW_W:3 * RW_W]
    b = _cumsum_rows(logw, reverse)
    last = 0 if reverse else c - 1
    e_b = jnp.exp(b)
    e_nb = jnp.exp(-b)
    e_last = jnp.exp(b[last:last + 1, :])
    e_tot = e_last * e_nb
    abar = (kk * jnp.exp(b - logw)).astype(BF16)
    rbar = (r * e_b).astype(BF16)
    kt = (k * e_nb).astype(BF16)
    bt = (beta * e_nb).astype(BF16)
    khat = (k * e_tot).astype(BF16)
    nbhat = (-(beta * e_tot)).astype(BF16)
    vb = v.astype(BF16)
    head_a = lax.broadcasted_iota(jnp.int32, (1, LANES), 1) < RW_D
    zero = jnp.zeros((c, LANES), BF16)

    def stack(*xs):
        rows = []
        for x in xs:
            rows += [jnp.where(head_a, x, zero), jnp.where(head_a, zero, x)]
        return jnp.concatenate(rows, axis=0)

    out = []
    for p in range(RW_HEADS // 2):
        sl = slice(p * LANES, (p + 1) * LANES)
        out.append(dict(xar=stack(abar[:, sl], rbar[:, sl]), yb=stack(bt[:, sl]), yk=stack(kt[:, sl]),
                        vs=stack(vb[:, sl]), kb=stack(khat[:, sl], nbhat[:, sl]),
                        e_col=jnp.broadcast_to(e_last[:, sl], (LANES, LANES)).T))
    return out


def _rw_kernel(sf_ref, df_ref, sb_ref, db_ref, yf_ref, yb_ref, h_ref):
    @pl.when(pl.program_id(1) == 0)
    def _():
        h_ref[...] = jnp.zeros_like(h_ref)

    c = CHUNK
    c2 = 2 * c
    npair = RW_HEADS // 2
    ops = _rw_operands(sf_ref[0], df_ref[0], False) + _rw_operands(sb_ref[0], db_ref[0], True)
    cat = lambda name: jnp.stack([o[name] for o in ops], axis=0)
    xar, yb_, yk, vs, kb, e_col = (cat(n) for n in ("xar", "yb", "yk", "vs", "kb", "e_col"))

    ti = lax.broadcasted_iota(jnp.int32, (c2, c2), 0)
    si = lax.broadcasted_iota(jnp.int32, (c2, c2), 1)
    same_head = (ti // c) == (si // c)
    both = lambda fwd, bwd: jnp.concatenate([jnp.broadcast_to(fwd[None], (npair, c2, c2)),
                                             jnp.broadcast_to(bwd[None], (npair, c2, c2))], axis=0)
    strict = both(same_head & (si < ti), same_head & (si > ti))
    incl = both(same_head & (si <= ti), same_head & (si >= ti))
    eye = jnp.where(ti == si, 1.0, 0.0).astype(F32)

    gb = _bmm_nt(xar, yb_)
    gk = _bmm_nt(xar, yk)
    l_ab = jnp.where(strict, gb[:, 0:c2], 0.0)
    l_rb = jnp.where(incl, gb[:, c2:2 * c2], 0.0).astype(BF16)
    l_ak = jnp.where(strict, gk[:, 0:c2], 0.0)
    l_rk = jnp.where(incl, gk[:, c2:2 * c2], 0.0)
    t_inv = None
    s = 1
    while s < c:
        same = (ti // (2 * s)) == (si // (2 * s))
        lo, hi = (ti // s) % 2, (si // s) % 2
        off = both(same & (lo == 1) & (hi == 0), same & (lo == 0) & (hi == 1))
        l_off = jnp.where(off, l_ab, 0.0)
        if t_inv is None:
            t_inv = eye[None] - l_off
        else:
            tb = t_inv.astype(BF16)
            t_inv = t_inv - _bmm(tb, _bmm(l_off.astype(BF16), tb).astype(BF16))
        s *= 2
    h0 = h_ref[...]
    xh = _bmm(xar, h0.astype(BF16))
    lv = _bmm(jnp.concatenate([l_ak, l_rk], axis=1).astype(BF16), vs)
    ub = _bmm(t_inv.astype(BF16), (xh[:, 0:c2] + lv[:, 0:c2]).astype(BF16)).astype(BF16)
    y2 = xh[:, c2:2 * c2] + lv[:, c2:2 * c2] - _bmm(l_rb, ub)
    y = y2[:, 0:c] + y2[:, c:c2]
    h_ref[...] = e_col * h0 + _bmm_tn(kb, jnp.concatenate([vs, ub], axis=1))
    yf_ref[0] = jnp.concatenate([y[p] for p in range(npair)], axis=1)
    yb_ref[0] = jnp.concatenate([y[npair + p] for p in range(npair)], axis=1)


def _rw_scan(shared, dfw, dbw, n_ctx_chunks):
    bsz, lt, _ = shared.shape
    nc = lt // CHUNK
    fwd = lambda b_, s: (b_, s, 0)
    bwd = lambda b_, s: (b_, _bwd_chunk(s, n_ctx_chunks, nc), 0)
    return pl.pallas_call(
        _rw_kernel,
        out_shape=(jax.ShapeDtypeStruct((bsz, lt, RW_W), F32),
                   jax.ShapeDtypeStruct((bsz, lt, RW_W), F32)),
        grid=(bsz, nc),
        in_specs=[pl.BlockSpec((1, CHUNK, 3 * RW_W), fwd),
                  pl.BlockSpec((1, CHUNK, 3 * RW_W), fwd),
                  pl.BlockSpec((1, CHUNK, 3 * RW_W), bwd),
                  pl.BlockSpec((1, CHUNK, 3 * RW_W), bwd)],
        out_specs=(pl.BlockSpec((1, CHUNK, RW_W), fwd),
                   pl.BlockSpec((1, CHUNK, RW_W), bwd)),
        scratch_shapes=[pltpu.VMEM((RW_HEADS, LANES, LANES), F32)],
        compiler_params=_cparams(("arbitrary", "arbitrary")),
        name="rwkv_scan",
    )(shared, dfw, shared, dbw)


def _rank_kernel(n, ti_ref, dest_ref, meta_ref, cnt_ref, run_ref, start_ref):
    ph = pl.program_id(0)
    i = pl.program_id(1)
    ti = ti_ref[...]
    lane = lax.broadcasted_iota(jnp.int32, (ROW_TILE, LANES), 1)
    ohs = [jnp.where(ti[:, j:j + 1] == lane, 1.0, 0.0).astype(F32) for j in range(TOP_K)]
    oh = ohs[0] + ohs[1] + ohs[2] + ohs[3]
    tile_cnt = jnp.sum(oh, axis=0, keepdims=True)

    @pl.when((ph == 0) & (i == 0))
    def _():
        cnt_ref[...] = jnp.zeros_like(cnt_ref)
        run_ref[...] = jnp.zeros_like(run_ref)

    @pl.when(ph == 0)
    def _():
        cnt_ref[...] += tile_cnt

    @pl.when((ph == 0) & (i == n - 1))
    def _():
        cnt = cnt_ref[...]
        shift = MOE_BLOCK.bit_length() - 1
        padded = jnp.left_shift(jnp.right_shift(cnt.astype(jnp.int32) + (MOE_BLOCK - 1), shift),
                                shift).astype(F32)
        e0 = lax.broadcasted_iota(jnp.int32, (LANES, LANES), 0)
        e1 = lax.broadcasted_iota(jnp.int32, (LANES, LANES), 1)
        before = jnp.where(e0 < e1, 1.0, 0.0).astype(F32)
        start = _dot_hi(jnp.broadcast_to(padded, (8, LANES)), before)[0:1]
        start_ref[...] = start
        meta_ref[0:1, :] = cnt
        meta_ref[1:2, :] = start
        meta_ref[2:3, :] = padded
        meta_ref[3:8, :] = jnp.zeros((5, LANES), F32)

    @pl.when(ph == 1)
    def _():
        t0 = lax.broadcasted_iota(jnp.int32, (ROW_TILE, ROW_TILE), 0)
        t1 = lax.broadcasted_iota(jnp.int32, (ROW_TILE, ROW_TILE), 1)
        earlier = jnp.where(t1 < t0, 1.0, 0.0).astype(BF16)
        pos = (jnp.dot(earlier, oh.astype(BF16), preferred_element_type=F32)
               + run_ref[...] + start_ref[...])
        dest = jnp.zeros((ROW_TILE, LANES), F32)
        for j in range(TOP_K):
            dj = jnp.sum(ohs[j] * pos, axis=-1, keepdims=True)
            dest = jnp.where(lane == j, dj, dest)
        dest_ref[...] = dest.astype(jnp.int32)
        run_ref[...] += tile_cnt


def _moe_rank(ti):
    n = ti.shape[0]
    nt = n // ROW_TILE
    return pl.pallas_call(
        functools.partial(_rank_kernel, nt),
        out_shape=(jax.ShapeDtypeStruct((n, LANES), jnp.int32),
                   jax.ShapeDtypeStruct((8, LANES), F32)),
        grid=(2, nt),
        in_specs=[pl.BlockSpec((ROW_TILE, LANES), lambda p, i: (i, 0))],
        out_specs=(pl.BlockSpec((ROW_TILE, LANES), lambda p, i: (i * p, 0)),
                   pl.BlockSpec((8, LANES), lambda p, i: (0, 0))),
        scratch_shapes=[pltpu.VMEM((1, LANES), F32), pltpu.VMEM((1, LANES), F32),
                        pltpu.VMEM((1, LANES), F32)],
        compiler_params=_cparams(("arbitrary", "arbitrary")),
        name="moe_rank",
    )(ti)


def _row_copy(src_ref, src_row, dst_ref, dst_row, sem):
    return pltpu.make_async_copy(src_ref.at[pl.ds(src_row, 1), :], dst_ref.at[pl.ds(dst_row, 1), :], sem)


def _dispatch_kernel(dest_ref, m_ref, xb_in_ref, xb_ref, sem):
    del xb_in_ref
    i = pl.program_id(0)

    def issue(t, c):
        for j in range(TOP_K):
            _row_copy(m_ref, t, xb_ref, dest_ref[i, t * TOP_K + j], sem).start()
        return c

    lax.fori_loop(0, ROW_TILE, issue, 0)

    def drain(t, c):
        for j in range(TOP_K):
            _row_copy(m_ref, 0, xb_ref, 0, sem).wait()
        return c

    lax.fori_loop(0, ROW_TILE, drain, 0)


def _moe_dispatch(dest2d, m, cap):
    n, d = m.shape
    nt = n // ROW_TILE
    return pl.pallas_call(
        _dispatch_kernel,
        out_shape=jax.ShapeDtypeStruct((cap, d), m.dtype),
        grid_spec=pltpu.PrefetchScalarGridSpec(
            num_scalar_prefetch=1,
            grid=(nt,),
            in_specs=[pl.BlockSpec((ROW_TILE, d), lambda i, dst: (i, 0)),
                      pl.BlockSpec(memory_space=pl.ANY)],
            out_specs=pl.BlockSpec(memory_space=pl.ANY),
            scratch_shapes=[pltpu.SemaphoreType.DMA(())]),
        input_output_aliases={2: 0},
        compiler_params=_cparams(("arbitrary",)),
        name="moe_dispatch",
    )(dest2d, m, jnp.zeros((cap, d), m.dtype))


def _expert_kernel(be_ref, nu_ref, x_ref, wg_ref, bg_ref, wu_ref, bu_ref, wd_ref, bd_ref, y_ref,
                   wg_s, wu_s, wd_s):
    b = pl.program_id(0)
    e = be_ref[b]
    changed = (b == 0) | (e != be_ref[jnp.maximum(b - 1, 0)])

    @pl.when(changed & (b < nu_ref[0]))
    def _():
        wg_s[...] = wg_ref[0].astype(BF16)
        wu_s[...] = wu_ref[0].astype(BF16)
        wd_s[...] = wd_ref[0].astype(BF16)

    @pl.when(b < nu_ref[0])
    def _():
        x = x_ref[...].astype(BF16)
        gt = jnp.minimum(jnp.dot(x, wg_s[...], preferred_element_type=F32) + bg_ref[0], SWIGLU_LIMIT)
        up = jnp.clip(jnp.dot(x, wu_s[...], preferred_element_type=F32) + bu_ref[0],
                      -SWIGLU_LIMIT, SWIGLU_LIMIT)
        act = (up + 1.0) * gt * jax.nn.sigmoid(SWIGLU_ALPHA * gt)
        y_ref[...] = jnp.dot(act.astype(BF16), wd_s[...], preferred_element_type=F32) + bd_ref[0]

    @pl.when(b >= nu_ref[0])
    def _():
        y_ref[...] = jnp.zeros_like(y_ref)


def _moe_experts(block_expert, n_used, xb, wg, bg, wu, bu, wd, bd):
    cap, d = xb.shape
    ne, _, f = wg.shape
    nb = cap // MOE_BLOCK
    wmap = lambda b, be, nu: (be[jnp.minimum(b, nu[0] - 1)], 0, 0)
    return pl.pallas_call(
        _expert_kernel,
        out_shape=jax.ShapeDtypeStruct((cap, d), F32),
        grid_spec=pltpu.PrefetchScalarGridSpec(
            num_scalar_prefetch=2,
            grid=(nb,),
            in_specs=[pl.BlockSpec((MOE_BLOCK, d), lambda b, be, nu: (b, 0)),
                      pl.BlockSpec((1, d, f), wmap), pl.BlockSpec((1, 1, f), wmap),
                      pl.BlockSpec((1, d, f), wmap), pl.BlockSpec((1, 1, f), wmap),
                      pl.BlockSpec((1, f, d), wmap), pl.BlockSpec((1, 1, d), wmap)],
            out_specs=pl.BlockSpec((MOE_BLOCK, d), lambda b, be, nu: (b, 0)),
            scratch_shapes=[pltpu.VMEM((d, f), BF16), pltpu.VMEM((d, f), BF16), pltpu.VMEM((f, d), BF16)]),
        compiler_params=_cparams(("arbitrary",)),
        name="moe_experts",
    )(block_expert, n_used, xb, wg, bg.reshape(ne, 1, f), wu, bu.reshape(ne, 1, f), wd, bd.reshape(ne, 1, d))


def _combine_kernel(alpha, n_tiles,
                    dest_ref, gt_ref, h_ref, mod_ref, vec_ref, yb_ref, o_ref, buf, sem):
    bi = pl.program_id(0)
    i = pl.program_id(1)
    tile = bi * n_tiles + i

    def issue(t, c):
        for j in range(TOP_K):
            _row_copy(yb_ref, dest_ref[tile, t * TOP_K + j], buf.at[j], t, sem).start()
        return c

    lax.fori_loop(0, ROW_TILE, issue, 0)

    def drain(t, c):
        for j in range(TOP_K):
            _row_copy(yb_ref, 0, buf.at[j], 0, sem).wait()
        return c

    lax.fori_loop(0, ROW_TILE, drain, 0)
    gt = gt_ref[0]
    f = gt[:, 0:1] * buf[0]
    for j in range(1, TOP_K):
        f = f + gt[:, j:j + 1] * buf[j]
    g2 = mod_ref[0, 0, 0:1, :]
    o_ref[0] = _ln(alpha * h_ref[0] + g2 * f, vec_ref[0:1, :], vec_ref[1:2, :], LN_EPS)


def _moe_combine(alpha, n_ctx_tiles, tile0, dest2d, gates, h1, mod3, vec, yb):
    bsz, rows, d = h1.shape
    nt = rows // ROW_TILE
    row = lambda b_, i, dst: (b_, i, 0)
    return pl.pallas_call(
        functools.partial(_combine_kernel, alpha, nt),
        out_shape=jax.ShapeDtypeStruct((bsz, rows, d), F32),
        grid_spec=pltpu.PrefetchScalarGridSpec(
            num_scalar_prefetch=1,
            grid=(bsz, nt),
            in_specs=[pl.BlockSpec((1, ROW_TILE, LANES), row),
                      pl.BlockSpec((1, ROW_TILE, d), row),
                      pl.BlockSpec((1, 1, 1, d),
                                   lambda b_, i, dst: (b_, jnp.where(tile0 + i < n_ctx_tiles, 0, 1), 0, 0)),
                      pl.BlockSpec((2, d), lambda b_, i, dst: (0, 0)),
                      pl.BlockSpec(memory_space=pl.ANY)],
            out_specs=pl.BlockSpec((1, ROW_TILE, d), row),
            scratch_shapes=[pltpu.VMEM((TOP_K, ROW_TILE, d), F32), pltpu.SemaphoreType.DMA(())]),
        compiler_params=_cparams(("arbitrary", "arbitrary")),
        name="moe_combine",
    )(dest2d, gates, h1, mod3, vec, yb)


def _moe(alpha, n_ctx_tiles, tile0, h1, m, ti, gates, mod3, ln2, router_unused, wg, bg, wu, bu, wd, bd):
    del router_unused
    bsz, rows, d = h1.shape
    n = bsz * rows
    dest, meta = _moe_rank(ti.reshape(n, LANES))
    dest2d = dest[:, :TOP_K].reshape(n // ROW_TILE, ROW_TILE * TOP_K)
    n_blocks = -(-(n * TOP_K) // MOE_BLOCK) + N_EXPERTS
    cap = n_blocks * MOE_BLOCK
    pad_end = (meta[1, :N_EXPERTS] + meta[2, :N_EXPERTS]).astype(jnp.int32)
    block_expert = jnp.minimum(
        jnp.searchsorted(pad_end, jnp.arange(n_blocks, dtype=jnp.int32) * MOE_BLOCK, side="right"),
        N_EXPERTS - 1).astype(jnp.int32)
    n_used = (pad_end[-1:] // MOE_BLOCK).astype(jnp.int32)
    xb = _moe_dispatch(dest2d, m.reshape(n, d), cap)
    yb = _moe_experts(block_expert, n_used, xb, wg, bg, wu, bu, wd, bd)
    return _moe_combine(alpha, n_ctx_tiles, tile0, dest2d, gates, h1, mod3, ln2, yb)


def kernel(x, c, ctx, c_ctx, ln_in_g, ln_in_b, ada_w, ada_b, w_in, lam_q1, lam_k1, lam_q2, lam_k2,
           da_subln_g, gla_gate_w2, gla_gate_b, gla_norm_g, rw_conv_w, rw_w2, rw_w0, rw_a2, rw_a0,
           rw_g2, rw_k_k, rw_k_a, rw_r_k, rw_lnx_g, rw_lnx_b, w_out, ln1_g, ln1_b, router_w, router_b,
           moe_w_gate, moe_b_gate, moe_w_up, moe_b_up, moe_w_down, moe_b_down, ln2_g, ln2_b):
    bsz, seq, d = x.shape
    n_ctx = ctx.shape[1]
    depth = w_in.shape[0]
    assert n_ctx % ROW_TILE == 0 and seq % ROW_TILE == 0 and seq % GRID_W == 0
    assert w_in.shape[2] == 3488 and bsz + 1 <= 8
    nct = n_ctx // ROW_TILE
    ncc = n_ctx // CHUNK
    nt = (n_ctx + seq) // ROW_TILE
    alpha = (2 * depth) ** 0.25

    c_all = jnp.concatenate([c, c_ctx[None], jnp.zeros((8 - bsz - 1, d), F32)], axis=0)
    mods = _ada_mod(c_all, ada_w, ada_b).reshape(depth, 8, 6, d)
    h = _ln_in(jnp.concatenate([ctx, x], axis=1), ln_in_g, ln_in_b)
    cos_t, sin_t = _rope_tables(n_ctx, seq)

    def pick(l, idx):
        mc = jnp.broadcast_to(mods[l, bsz][None, idx], (bsz, len(idx), d))
        return jnp.stack([mc, mods[l, :bsz][:, idx]], axis=1)

    for l in range(depth):
        last = l == depth - 1
        tile0 = nct if last else 0
        n_out_tiles = nt - tile0
        lam_init = 0.8 - 0.6 * math.exp(-0.3 * l)
        lam = (jnp.exp(jnp.sum(lam_q1[l] * lam_k1[l])) - jnp.exp(jnp.sum(lam_q2[l] * lam_k2[l])) + lam_init)

        qk, v, zg, zr = _inproj(h, pick(l, [1, 0]), _pack_w_in(w_in[l]), cos_t, sin_t, nct)
        ao = _attention(qk, v, lam, nct, nt - nct, n_ctx + seq)
        if not last:
            ao = jnp.concatenate([_attention(qk, v, lam, 0, nct, n_ctx), ao], axis=1)
        w2p, biasp = _pack_gla_gate(gla_gate_w2[l], gla_gate_b[l])
        gof, gob = _gla_scan(zg, w2p, biasp, ncc)
        rvec = jnp.stack([rw_k_k[l], rw_k_a[l], rw_r_k[l].reshape(-1), rw_w0[l, 0], rw_w0[l, 1],
                          rw_a0[l, 0], rw_a0[l, 1], jnp.zeros((RW_W,), F32)], axis=0)
        shared, dfw, dbw, rg, bonus = _rw_prep(zr, rw_conv_w[l], _block2(rw_w2[l]), _block2(rw_a2[l]),
                                               rw_g2[l], rvec, nct)
        yf, yb = _rw_scan(shared, dfw, dbw, ncc)

        nrm = jnp.stack([jnp.pad(jnp.tile(da_subln_g[l], DA_HEADS), (0, GLA_W - DA_W)),
                         jnp.tile(gla_norm_g[l], GLA_HEADS), rw_lnx_g[l], rw_lnx_b[l]], axis=0)
        rw_p = jnp.pad(router_w[l], ((0, 0), (0, LANES - N_EXPERTS)))
        rb_p = jnp.pad(router_b[l], (0, LANES - N_EXPERTS), constant_values=-1e30).reshape(1, LANES)
        h1, m, ti, gates = _mixout(alpha, 1.0 - lam_init, tile0, n_out_tiles, nct,
                                   ao, gof, gob, zg, yf, yb, bonus, rg, h, pick(l, [2, 4, 3]),
                                   w_out[l].astype(BF16), jnp.stack([ln1_g[l], ln1_b[l]], 0), nrm, rw_p, rb_p)
        h = _moe(alpha, nct, tile0, h1, m, ti, gates, pick(l, [5]), jnp.stack([ln2_g[l], ln2_b[l]], 0), None,
                 moe_w_gate[l], moe_b_gate[l], moe_w_up[l], moe_b_up[l], moe_w_down[l], moe_b_down[l])
    return h
```

```python
import functools
import math

import jax
import jax.numpy as jnp
import numpy as np
from jax import lax
from jax.experimental import pallas as pl
from jax.experimental.pallas import tpu as pltpu

F32 = jnp.float32
BF16 = jnp.bfloat16
HI = lax.Precision.HIGHEST

GRID_W = 64
DA_HEADS, DA_QK, DA_V = 4, 32, 64
GLA_HEADS, GLA_DK, GLA_DV, GLA_RANK, GLA_TAU = 6, 32, 64, 16, 16.0
RW_HEADS, RW_D, RW_DECAY_RANK, RW_A_RANK, RW_GATE_RANK = 6, 64, 64, 64, 128
RW_GN_EPS = 64e-5
N_EXPERTS, TOP_K = 32, 4
SWIGLU_LIMIT, SWIGLU_ALPHA = 7.0, 1.702
ROPE_BASE = 10000.0
LN_EPS = 1e-5

DA_W = DA_HEADS * DA_V
GLA_KW = GLA_HEADS * GLA_DK
GLA_W = GLA_HEADS * GLA_DV
RW_W = RW_HEADS * RW_D
MIX_W = DA_W + GLA_W + RW_W

LANES = 128
ROW_TILE = 256
CHUNK = 64
MOE_BLOCK = 256
VMEM_LIMIT = 56 * 1024 * 1024

Z_ATTN = 3 * DA_W
GLA_KP = 256
Z_GLA = 2 * GLA_KP + 2 * GLA_W + LANES
Z_RW = 3 * RW_W + 3 * LANES
Z_ALL = Z_ATTN + Z_GLA + Z_RW


def _cparams(sem):
    return pltpu.CompilerParams(dimension_semantics=sem, vmem_limit_bytes=VMEM_LIMIT)


def _ln(x, g, b, eps):
    mu = jnp.mean(x, axis=-1, keepdims=True)
    xc = x - mu
    var = jnp.mean(xc * xc, axis=-1, keepdims=True)
    return xc * lax.rsqrt(var + eps) * g + b


def _silu(x):
    return x * jax.nn.sigmoid(x)


def _dot(a, b):
    return jnp.dot(a.astype(BF16), b.astype(BF16), preferred_element_type=F32)


def _dot_hi(a, b):
    return jnp.dot(a, b, precision=HI, preferred_element_type=F32)


def _dot_nt(a, b):
    return lax.dot_general(a.astype(BF16), b.astype(BF16), (((1,), (1,)), ((), ())),
                           preferred_element_type=F32)


def _group_sum(x, gmat):
    hi = x.astype(BF16)
    lo = (x - hi.astype(F32)).astype(BF16)
    return (jnp.dot(hi, gmat, preferred_element_type=F32)
            + jnp.dot(lo, gmat, preferred_element_type=F32))


def _group_matrix(width, group):
    idx = np.arange(width) // group
    return jnp.asarray((idx[:, None] == idx[None, :]).astype(np.float32), dtype=BF16)


def _ada_kernel(c_ref, w_ref, b_ref, o_ref):
    o_ref[0] = _dot_hi(_silu(c_ref[...]), w_ref[0]) + b_ref[0]


def _ada_mod(c_all, ada_w, ada_b):
    nl, d, n6 = ada_w.shape
    tn = 1536
    return pl.pallas_call(
        _ada_kernel,
        out_shape=jax.ShapeDtypeStruct((nl, c_all.shape[0], n6), F32),
        grid=(nl, n6 // tn),
        in_specs=[pl.BlockSpec((c_all.shape[0], d), lambda l, j: (0, 0)),
                  pl.BlockSpec((1, d, tn), lambda l, j: (l, 0, j)),
                  pl.BlockSpec((1, 1, tn), lambda l, j: (l, 0, j))],
        out_specs=pl.BlockSpec((1, c_all.shape[0], tn), lambda l, j: (l, 0, j)),
        compiler_params=_cparams(("arbitrary", "arbitrary")),
        name="ada_mod",
    )(c_all, ada_w, ada_b.reshape(nl, 1, n6))


def _ln_in_kernel(n_ctx_tiles, c_ref, x_ref, g_ref, b_ref, o_ref):
    @pl.when(pl.program_id(1) < n_ctx_tiles)
    def _():
        o_ref[0] = _ln(c_ref[0], g_ref[...], b_ref[...], LN_EPS)

    @pl.when(pl.program_id(1) >= n_ctx_tiles)
    def _():
        o_ref[0] = _ln(x_ref[0], g_ref[...], b_ref[...], LN_EPS)


def _ln_in(ctx, x, g, b):
    bsz, n_ctx, d = ctx.shape
    nct = n_ctx // ROW_TILE
    nt = nct + x.shape[1] // ROW_TILE
    return pl.pallas_call(
        functools.partial(_ln_in_kernel, nct),
        out_shape=jax.ShapeDtypeStruct((bsz, nt * ROW_TILE, d), F32),
        grid=(bsz, nt),
        in_specs=[pl.BlockSpec((1, ROW_TILE, d), lambda b_, i: (b_, jnp.minimum(i, nct - 1), 0)),
                  pl.BlockSpec((1, ROW_TILE, d), lambda b_, i: (b_, jnp.maximum(i - nct, 0), 0)),
                  pl.BlockSpec((1, d), lambda b_, i: (0, 0)),
                  pl.BlockSpec((1, d), lambda b_, i: (0, 0))],
        out_specs=pl.BlockSpec((1, ROW_TILE, d), lambda b_, i: (b_, i, 0)),
        compiler_params=_cparams(("arbitrary", "arbitrary")),
        name="ln_in",
    )(ctx, x, g.reshape(1, d), b.reshape(1, d))


def _inproj_kernel(h_ref, mod_ref, w_ref, cos_ref, sin_ref, qk_ref, v_ref, gla_ref, rw_ref):
    h = h_ref[0]
    sc = mod_ref[0, 0, 0:1, :]
    sh = mod_ref[0, 0, 1:2, :]
    xm = (h * (1.0 + sc) + sh).astype(BF16)
    qk = jnp.dot(xm, w_ref[:, 0:2 * DA_W], preferred_element_type=F32)
    cos = cos_ref[...]
    sin = sin_ref[...]
    lane = lax.broadcasted_iota(jnp.int32, (1, LANES), 1)
    first = (lane % 16) < 8
    qscale = DA_QK ** -0.5
    for j in range(4):
        x = qk[:, j * LANES:(j + 1) * LANES]
        rot = jnp.where(first, pltpu.roll(x, LANES - 8, 1), pltpu.roll(x, 8, 1))
        y = x * cos + rot * sin
        if j < 2:
            y = y * qscale
        qk_ref[0, :, j * LANES:(j + 1) * LANES] = y.astype(BF16)
    v_ref[0] = jnp.dot(xm, w_ref[:, 2 * DA_W:Z_ATTN], preferred_element_type=F32).astype(BF16)
    gla_ref[0] = jnp.dot(xm, w_ref[:, Z_ATTN:Z_ATTN + Z_GLA], preferred_element_type=F32)
    rw_ref[0] = jnp.dot(xm, w_ref[:, Z_ATTN + Z_GLA:Z_ALL], preferred_element_type=F32)


def _inproj(h, mod1, w_p, cos_t, sin_t, n_ctx_tiles):
    bsz, lt, d = h.shape
    nt = lt // ROW_TILE
    row = lambda b_, i: (b_, i, 0)
    return pl.pallas_call(
        _inproj_kernel,
        out_shape=(jax.ShapeDtypeStruct((bsz, lt, 2 * DA_W), BF16),
                   jax.ShapeDtypeStruct((bsz, lt, DA_W), BF16),
                   jax.ShapeDtypeStruct((bsz, lt, Z_GLA), F32),
                   jax.ShapeDtypeStruct((bsz, lt, Z_RW), F32)),
        grid=(bsz, nt),
        in_specs=[pl.BlockSpec((1, ROW_TILE, d), row),
                  pl.BlockSpec((1, 1, 2, d),
                               lambda b_, i: (b_, jnp.where(i < n_ctx_tiles, 0, 1), 0, 0)),
                  pl.BlockSpec((d, Z_ALL), lambda b_, i: (0, 0)),
                  pl.BlockSpec((ROW_TILE, LANES), lambda b_, i: (i, 0)),
                  pl.BlockSpec((ROW_TILE, LANES), lambda b_, i: (i, 0))],
        out_specs=(pl.BlockSpec((1, ROW_TILE, 2 * DA_W), row),
                   pl.BlockSpec((1, ROW_TILE, DA_W), row),
                   pl.BlockSpec((1, ROW_TILE, Z_GLA), row),
                   pl.BlockSpec((1, ROW_TILE, Z_RW), row)),
        compiler_params=_cparams(("arbitrary", "arbitrary")),
        name="inproj",
    )(h, mod1, w_p, cos_t, sin_t)


def _pack_w_in(w):
    d = w.shape[0]
    o = 0
    parts = {}
    for name, n in (("da_q", 256), ("da_k", 256), ("da_v", 256), ("gla_q", GLA_KW), ("gla_k", GLA_KW),
                    ("gla_v", GLA_W), ("gla_gf", GLA_RANK), ("gla_gb", GLA_RANK), ("gla_r", GLA_W),
                    ("rw_rkv", 3 * RW_W), ("rw_wf", 64), ("rw_wb", 64), ("rw_af", 64), ("rw_ab", 64),
                    ("rw_g", 128)):
        parts[name] = w[:, o:o + n]
        o += n
    z = lambda n: jnp.zeros((d, n), w.dtype)
    cols = [parts["da_q"], parts["da_k"], parts["da_v"],
            parts["gla_q"], z(GLA_KP - GLA_KW), parts["gla_k"], z(GLA_KP - GLA_KW),
            parts["gla_v"], parts["gla_r"], parts["gla_gf"], parts["gla_gb"], z(LANES - 2 * GLA_RANK),
            parts["rw_rkv"], parts["rw_wf"], parts["rw_wb"], parts["rw_af"], parts["rw_ab"], parts["rw_g"]]
    return jnp.concatenate(cols, axis=1).astype(BF16)


def _rope_tables(n_ctx, seq):
    t = np.arange(seq)
    row = (t // GRID_W).astype(np.float32)
    col = (t % GRID_W).astype(np.float32)
    quarter = DA_QK // 4
    inv = (ROPE_BASE ** (-np.arange(quarter, dtype=np.float32) / quarter)).astype(np.float32)
    ang_r = row[:, None] * inv
    ang_c = col[:, None] * inv
    ang = np.concatenate([ang_r, ang_r, ang_c, ang_c], -1).astype(np.float32)
    cos = np.tile(np.cos(ang), (1, LANES // DA_QK))
    sin = np.tile(np.sin(ang), (1, LANES // DA_QK))
    sign = np.where((np.arange(LANES) % 16) < 8, -1.0, 1.0).astype(np.float32)
    cos = np.concatenate([np.ones((n_ctx, LANES), np.float32), cos], 0)
    sin = np.concatenate([np.zeros((n_ctx, LANES), np.float32), sin * sign], 0)
    return jnp.asarray(cos, F32), jnp.asarray(sin, F32)


def _attn_kernel(lam_ref, q_ref, k_ref, v_ref, o_ref):
    q = q_ref[0]
    k = k_ref[0]
    v = v_ref[0]
    lam = lam_ref[0, 0]
    lane = lax.broadcasted_iota(jnp.int32, (1, LANES), 1)
    outs = []
    for hh in range(2):
        acc = None
        for m in range(2):
            lo = hh * 64 + m * DA_QK
            qm = jnp.where((lane >= lo) & (lane < lo + DA_QK), q, jnp.zeros_like(q))
            s = _dot_nt(qm, k)
            e = jnp.exp(s - jnp.max(s, axis=-1, keepdims=True))
            den = jnp.sum(e, axis=-1, keepdims=True)
            pv = jnp.dot(e.astype(BF16), v, preferred_element_type=F32) / den
            acc = pv if m == 0 else acc - lam * pv
        outs.append(acc)
    o_ref[0] = jnp.where(lane < 64, outs[0], outs[1])


def _attention(qk, v, lam, q_tile0, n_q_tiles, n_k_rows):
    bsz, lt, _ = qk.shape
    out = pl.pallas_call(
        _attn_kernel,
        out_shape=jax.ShapeDtypeStruct((bsz, n_q_tiles * ROW_TILE, DA_W), F32),
        grid=(bsz, 2, n_q_tiles),
        in_specs=[pl.BlockSpec(memory_space=pltpu.SMEM),
                  pl.BlockSpec((1, ROW_TILE, LANES), lambda b_, p, i: (b_, q_tile0 + i, p)),
                  pl.BlockSpec((1, n_k_rows, LANES), lambda b_, p, i: (b_, 0, 2 + p)),
                  pl.BlockSpec((1, n_k_rows, LANES), lambda b_, p, i: (b_, 0, p))],
        out_specs=pl.BlockSpec((1, ROW_TILE, LANES), lambda b_, p, i: (b_, i, p)),
        compiler_params=_cparams(("arbitrary", "arbitrary", "arbitrary")),
        name="diff_attn",
    )(lam.reshape(1, 1), qk, qk, v)
    return out


def _mixout_kernel(alpha, sub_scale,
                   ao_ref, gof_ref, gob_ref, zg_ref, yf_ref, yb_ref, bonus_ref, rg_ref, h_ref, mod_ref,
                   wo_ref, vec_ref, nrm_ref, g256_ref, g384_ref, rw_ref, rb_ref,
                   h1_ref, m_ref, ti_ref, gt_ref):
    a = ao_ref[0]
    a = a * lax.rsqrt(_group_sum(a * a, g256_ref[...]) * (1.0 / DA_V) + LN_EPS) * nrm_ref[0:1, 0:DA_W] * sub_scale
    o = gof_ref[0] + gob_ref[0]
    r = zg_ref[0, :, 2 * GLA_KP + GLA_W:2 * GLA_KP + 2 * GLA_W]
    gl = (o * lax.rsqrt(_group_sum(o * o, g384_ref[...]) * (1.0 / GLA_DV) + LN_EPS)
          * nrm_ref[1:2, :] * _silu(r))
    y = yf_ref[0] + yb_ref[0]
    mu = _group_sum(y, g384_ref[...]) * (1.0 / RW_D)
    yc = y - mu
    var = _group_sum(yc * yc, g384_ref[...]) * (1.0 / RW_D)
    yn = yc * lax.rsqrt(var + RW_GN_EPS) * nrm_ref[2:3, :] + nrm_ref[3:4, :]
    rw = (yn + bonus_ref[0]) * rg_ref[0]
    mix = (_dot(a, wo_ref[0:DA_W, :]) + _dot(gl, wo_ref[DA_W:DA_W + GLA_W, :])
           + _dot(rw, wo_ref[DA_W + GLA_W:MIX_W, :]))
    g1 = mod_ref[0, 0, 0:1, :]
    sc2 = mod_ref[0, 0, 1:2, :]
    sh2 = mod_ref[0, 0, 2:3, :]
    h1 = _ln(alpha * h_ref[0] + g1 * mix, vec_ref[0:1, :], vec_ref[1:2, :], LN_EPS)
    h1_ref[0] = h1
    m = h1 * (1.0 + sc2) + sh2
    m_ref[0] = m
    logits = _dot_hi(m, rw_ref[...]) + rb_ref[...]
    lane = lax.broadcasted_iota(jnp.int32, logits.shape, 1)
    ti = jnp.zeros(logits.shape, jnp.int32)
    tv = jnp.full(logits.shape, -1e30, F32)
    for j in range(TOP_K):
        mx = jnp.max(logits, axis=-1, keepdims=True)
        idx = jnp.min(jnp.where(logits == mx, lane, LANES), axis=-1, keepdims=True)
        ti = jnp.where(lane == j, idx, ti)
        tv = jnp.where(lane == j, mx, tv)
        logits = jnp.where(lane == idx, -jnp.inf, logits)
    e = jnp.exp(tv - jnp.max(tv, axis=-1, keepdims=True))
    ti_ref[0] = ti
    gt_ref[0] = e / jnp.sum(e, axis=-1, keepdims=True)


def _mixout(alpha, sub_scale, tile0, n_tiles, n_ctx_tiles,
            ao, gof, gob, zg, yf, yb, bonus, rg, h, mod2, wo, vec, nrm, rw_p, rb_p):
    bsz, lt, d = h.shape
    row = lambda b_, i: (b_, tile0 + i, 0)
    orow = lambda b_, i: (b_, i, 0)
    full = lambda shape: pl.BlockSpec(shape, lambda b_, i: tuple(0 for _ in shape))
    nrows = n_tiles * ROW_TILE
    return pl.pallas_call(
        functools.partial(_mixout_kernel, alpha, sub_scale),
        out_shape=(jax.ShapeDtypeStruct((bsz, nrows, d), F32),
                   jax.ShapeDtypeStruct((bsz, nrows, d), F32),
                   jax.ShapeDtypeStruct((bsz, nrows, LANES), jnp.int32),
                   jax.ShapeDtypeStruct((bsz, nrows, LANES), F32)),
        grid=(bsz, n_tiles),
        in_specs=[pl.BlockSpec((1, ROW_TILE, DA_W), orow),
                  pl.BlockSpec((1, ROW_TILE, GLA_W), row),
                  pl.BlockSpec((1, ROW_TILE, GLA_W), row),
                  pl.BlockSpec((1, ROW_TILE, Z_GLA), row),
                  pl.BlockSpec((1, ROW_TILE, RW_W), row),
                  pl.BlockSpec((1, ROW_TILE, RW_W), row),
                  pl.BlockSpec((1, ROW_TILE, RW_W), row),
                  pl.BlockSpec((1, ROW_TILE, RW_W), row),
                  pl.BlockSpec((1, ROW_TILE, d), row),
                  pl.BlockSpec((1, 1, 3, d),
                               lambda b_, i: (b_, jnp.where(tile0 + i < n_ctx_tiles, 0, 1), 0, 0)),
                  full((MIX_W, d)), full((2, d)), full((4, GLA_W)),
                  full((DA_W, DA_W)), full((GLA_W, GLA_W)), full((d, LANES)), full((1, LANES))],
        out_specs=(pl.BlockSpec((1, ROW_TILE, d), orow),
                   pl.BlockSpec((1, ROW_TILE, d), orow),
                   pl.BlockSpec((1, ROW_TILE, LANES), orow),
                   pl.BlockSpec((1, ROW_TILE, LANES), orow)),
        compiler_params=_cparams(("arbitrary", "arbitrary")),
        name="mix_out",
    )(ao, gof, gob, zg, yf, yb, bonus, rg, h, mod2, wo, vec, nrm,
      _group_matrix(DA_W, DA_V), _group_matrix(GLA_W, GLA_DV), rw_p, rb_p)


def _bwd_chunk(step, n_ctx_chunks, n_chunks):
    return jnp.where(step < n_ctx_chunks, n_ctx_chunks - 1 - step, n_chunks + n_ctx_chunks - 1 - step)


def _tri(n, reverse, strict):
    t = lax.broadcasted_iota(jnp.int32, (n, n), 0)
    s = lax.broadcasted_iota(jnp.int32, (n, n), 1)
    if reverse:
        return (s > t) if strict else (s >= t)
    return (s < t) if strict else (s <= t)


def _dot_tn(a, b):
    return lax.dot_general(a, b, (((0,), (0,)), ((), ())), precision=HI, preferred_element_type=F32)


def _dot_nt_hi(a, b):
    return lax.dot_general(a, b, (((1,), (1,)), ((), ())), precision=HI, preferred_element_type=F32)


def _log_sigmoid(x):
    return jnp.minimum(x, 0.0) - jnp.log1p(jnp.exp(-jnp.abs(x)))


def _softplus(x):
    return jnp.maximum(x, 0.0) + jnp.log1p(jnp.exp(-jnp.abs(x)))


def _split3(x):
    t0 = x.astype(BF16)
    r1 = x - t0.astype(F32)
    t1 = r1.astype(BF16)
    t2 = (r1 - t1.astype(F32)).astype(BF16)
    return t0, t1, t2


def _cumsum_rows(x, reverse):
    tri = jnp.where(_tri(CHUNK, reverse, False), 1.0, 0.0).astype(BF16)
    return sum(jnp.dot(tri, t, preferred_element_type=F32) for t in _split3(x))


def _bdot(a, b):
    return jnp.dot(a, b, preferred_element_type=F32)


def _bdot_nt(a, b):
    return lax.dot_general(a, b, (((1,), (1,)), ((), ())), preferred_element_type=F32)


def _bdot_tn(a, b):
    return lax.dot_general(a, b, (((0,), (0,)), ((), ())), preferred_element_type=F32)


GLA_SUB = 16
GLA_EXP_CLAMP = 60.0


def _gla_kernel(zf_ref, zb_ref, w2_ref, bias_ref, of_ref, ob_ref, st_ref):
    @pl.when(pl.program_id(1) == 0)
    def _():
        st_ref[...] = jnp.zeros_like(st_ref)

    c = CHUNK
    nsub = c // GLA_SUB
    qs, ks, vs, bs, b_lasts = [], [], [], [], []
    for reverse, z_ref in ((False, zf_ref), (True, zb_ref)):
        zg = z_ref[0]
        col0 = GLA_KP if reverse else 0
        gpre = _dot(zg[:, 2 * GLA_KP + 2 * GLA_W:Z_GLA], w2_ref[:, col0:col0 + GLA_KP]) \
            + bias_ref[:, col0:col0 + GLA_KP]
        b = _cumsum_rows(_log_sigmoid(gpre) * (1.0 / GLA_TAU), reverse)
        last = 0 if reverse else c - 1
        qs.append(zg[:, 0:GLA_KP] * (GLA_DK ** -0.5))
        ks.append(zg[:, GLA_KP:2 * GLA_KP])
        vs.append(zg[:, 2 * GLA_KP:2 * GLA_KP + GLA_W])
        bs.append(b)
        b_lasts.append(b[last:last + 1, :])
    q, k, v, b, b_last = (jnp.stack(x, axis=0) for x in (qs, ks, vs, bs, b_lasts))
    vb = v.astype(BF16)
    st = st_ref[...]
    inter = _bmm_nt((q * jnp.exp(b)).astype(BF16), st.astype(BF16))
    kv = _bmm_tn(vb, (k * jnp.exp(b_last - b)).astype(BF16))
    vi = lax.broadcasted_iota(jnp.int32, (GLA_W, GLA_KP), 0) // GLA_DV
    ki = lax.broadcasted_iota(jnp.int32, (GLA_W, GLA_KP), 1) // GLA_DK
    st_ref[...] = st * jnp.exp(b_last) + jnp.where((vi == ki)[None], kv, 0.0)

    lane_head = lax.broadcasted_iota(jnp.int32, (1, 1, GLA_KP), 2) // GLA_DK
    per_head = [[None] * nsub for _ in range(GLA_HEADS)]
    for i in range(nsub):
        r0 = i * GLA_SUB
        b_ref = jnp.stack([b[0, r0:r0 + 1], b[1, r0 + GLA_SUB - 1:r0 + GLA_SUB]], axis=0)
        qi = q[:, r0:r0 + GLA_SUB] * jnp.exp(b[:, r0:r0 + GLA_SUB] - b_ref)
        ki_ = (k * jnp.exp(jnp.minimum(b_ref - b, GLA_EXP_CLAMP))).astype(BF16)
        qh = jnp.concatenate([jnp.where(lane_head == h, qi, 0.0) for h in range(GLA_HEADS)], axis=1)
        a = _bmm_nt(qh.astype(BF16), ki_)
        for h in range(GLA_HEADS):
            per_head[h][i] = a[:, h * GLA_SUB:(h + 1) * GLA_SUB]
    causal = jnp.stack([_tri(c, False, False), _tri(c, True, False)], axis=0)
    lane = lax.broadcasted_iota(jnp.int32, (1, 1, LANES), 2)
    pieces = []
    for p in range(GLA_HEADS // 2):
        vp = vb[:, :, p * LANES:(p + 1) * LANES]
        halves = []
        for hh in range(2):
            a_h = jnp.where(causal, jnp.concatenate(per_head[2 * p + hh], axis=1), 0.0)
            halves.append(_bmm(a_h.astype(BF16), vp))
        pieces.append(jnp.where(lane < GLA_DV, halves[0], halves[1]))
    o = inter + jnp.concatenate(pieces, axis=2)
    of_ref[0] = o[0]
    ob_ref[0] = o[1]


def _gla_scan(zg, w2p, biasp, n_ctx_chunks):
    bsz, lt, _ = zg.shape
    nc = lt // CHUNK
    fwd = lambda b_, s: (b_, s, 0)
    bwd = lambda b_, s: (b_, _bwd_chunk(s, n_ctx_chunks, nc), 0)
    return pl.pallas_call(
        _gla_kernel,
        out_shape=(jax.ShapeDtypeStruct((bsz, lt, GLA_W), F32),
                   jax.ShapeDtypeStruct((bsz, lt, GLA_W), F32)),
        grid=(bsz, nc),
        in_specs=[pl.BlockSpec((1, CHUNK, Z_GLA), fwd),
                  pl.BlockSpec((1, CHUNK, Z_GLA), bwd),
                  pl.BlockSpec((LANES, 2 * GLA_KP), lambda b_, s: (0, 0)),
                  pl.BlockSpec((1, 2 * GLA_KP), lambda b_, s: (0, 0))],
        out_specs=(pl.BlockSpec((1, CHUNK, GLA_W), fwd),
                   pl.BlockSpec((1, CHUNK, GLA_W), bwd)),
        scratch_shapes=[pltpu.VMEM((2, GLA_W, GLA_KP), F32)],
        compiler_params=_cparams(("arbitrary", "arbitrary")),
        name="gla_scan",
    )(zg, zg, w2p, biasp)


def _pack_gla_gate(w2, bias):
    w = jnp.zeros((LANES, 2 * GLA_KP), F32)
    w = w.at[0:GLA_RANK, 0:GLA_KW].set(w2[0]).at[GLA_RANK:2 * GLA_RANK, GLA_KP:GLA_KP + GLA_KW].set(w2[1])
    b = jnp.zeros((1, 2 * GLA_KP), F32)
    b = b.at[0, 0:GLA_KW].set(bias[0]).at[0, GLA_KP:GLA_KP + GLA_KW].set(bias[1])
    return w, b


def _rwprep_kernel(n_ctx_tiles, n_tiles,
                   z_ref, zp_ref, zn_ref, cw_ref, w2_ref, a2_ref, g2_ref, vec_ref, gm_ref,
                   sh_ref, df_ref, db_ref, g_ref, bonus_ref):
    i = pl.program_id(1)
    z = z_ref[0]
    x = z[:, 0:3 * RW_W]
    seg_first = (i == 0) | (i == n_ctx_tiles)
    seg_last = (i == n_ctx_tiles - 1) | (i == n_tiles - 1)
    prev_row = jnp.where(seg_first, 0.0, zp_ref[0, 7:8, 0:3 * RW_W])
    next_row = jnp.where(seg_last, 0.0, zn_ref[0, 0:1, 0:3 * RW_W])
    ridx = lax.broadcasted_iota(jnp.int32, (ROW_TILE, 1), 0)
    x_prev = jnp.where(ridx == 0, prev_row, pltpu.roll(x, 1, 0))
    x_next = jnp.where(ridx == ROW_TILE - 1, next_row, pltpu.roll(x, ROW_TILE - 1, 0))
    xc = x_prev * cw_ref[0:1, :] + x * cw_ref[1:2, :] + x_next * cw_ref[2:3, :]
    r = xc[:, 0:RW_W]
    k = xc[:, RW_W:2 * RW_W]
    v = xc[:, 2 * RW_W:3 * RW_W]
    gm = gm_ref[...]
    kk = k * vec_ref[0:1, :]
    kk = kk / jnp.maximum(jnp.sqrt(_group_sum(kk * kk, gm)), 1e-12)
    k_a = vec_ref[1:2, :]
    r_k = vec_ref[2:3, :]
    w_raw = _dot_hi(jnp.tanh(z[:, 3 * RW_W:3 * RW_W + LANES]), w2_ref[...])
    a_raw = _dot_hi(z[:, 3 * RW_W + LANES:3 * RW_W + 2 * LANES], a2_ref[...])
    g_ref[0] = _dot_hi(jax.nn.sigmoid(z[:, 3 * RW_W + 2 * LANES:Z_RW]), g2_ref[...])
    sh_ref[0, :, 0:RW_W] = r
    sh_ref[0, :, RW_W:2 * RW_W] = v
    sh_ref[0, :, 2 * RW_W:3 * RW_W] = kk
    rk_sum = None
    for d, d_ref in enumerate((df_ref, db_ref)):
        wr = w_raw[:, d * RW_W:(d + 1) * RW_W] + vec_ref[3 + d:4 + d, :]
        logw = -jnp.exp(-_softplus(-wr) - 0.5)
        a = jax.nn.sigmoid(a_raw[:, d * RW_W:(d + 1) * RW_W] + vec_ref[5 + d:6 + d, :])
        k_mod = k * (1.0 + (a - 1.0) * k_a)
        d_ref[0, :, 0:RW_W] = logw
        d_ref[0, :, RW_W:2 * RW_W] = kk * a
        d_ref[0, :, 2 * RW_W:3 * RW_W] = k_mod
        s = _group_sum(r * k_mod * r_k, gm)
        rk_sum = s if d == 0 else rk_sum + s
    bonus_ref[0] = rk_sum * v


def _rw_prep(zr, cw, w2p, a2p, g2, vec, n_ctx_tiles):
    bsz, lt, _ = zr.shape
    nt = lt // ROW_TILE
    hb = ROW_TILE // 8
    row = lambda b_, i: (b_, i, 0)
    full = lambda shape: pl.BlockSpec(shape, lambda b_, i: tuple(0 for _ in shape))
    o3 = jax.ShapeDtypeStruct((bsz, lt, 3 * RW_W), F32)
    o1 = jax.ShapeDtypeStruct((bsz, lt, RW_W), F32)
    return pl.pallas_call(
        functools.partial(_rwprep_kernel, n_ctx_tiles, nt),
        out_shape=(o3, o3, o3, o1, o1),
        grid=(bsz, nt),
        in_specs=[pl.BlockSpec((1, ROW_TILE, Z_RW), row),
                  pl.BlockSpec((1, 8, Z_RW), lambda b_, i: (b_, jnp.maximum(i * hb - 1, 0), 0)),
                  pl.BlockSpec((1, 8, Z_RW), lambda b_, i: (b_, jnp.minimum((i + 1) * hb, nt * hb - 1), 0)),
                  full((3, 3 * RW_W)), full((LANES, 2 * RW_W)), full((LANES, 2 * RW_W)),
                  full((RW_GATE_RANK, RW_W)), full((8, RW_W)), full((RW_W, RW_W))],
        out_specs=(pl.BlockSpec((1, ROW_TILE, 3 * RW_W), row),
                   pl.BlockSpec((1, ROW_TILE, 3 * RW_W), row),
                   pl.BlockSpec((1, ROW_TILE, 3 * RW_W), row),
                   pl.BlockSpec((1, ROW_TILE, RW_W), row),
                   pl.BlockSpec((1, ROW_TILE, RW_W), row)),
        compiler_params=_cparams(("arbitrary", "arbitrary")),
        name="rwkv_prep",
    )(zr, zr, zr, cw, w2p, a2p, g2, vec, _group_matrix(RW_W, RW_D))


def _block2(w):
    r, n = w.shape[1:]
    z = jnp.zeros((r, n), w.dtype)
    return jnp.concatenate([jnp.concatenate([w[0], z], 1), jnp.concatenate([z, w[1]], 1)], 0)


def _bmm(a, b):
    return lax.dot_general(a, b, (((2,), (1,)), ((0,), (0,))), preferred_element_type=F32)


def _bmm_nt(a, b):
    return lax.dot_general(a, b, (((2,), (2,)), ((0,), (0,))), preferred_element_type=F32)


def _bmm_tn(a, b):
    return lax.dot_general(a, b, (((1,), (1,)), ((0,), (0,))), preferred_element_type=F32)


def _rw_operands(sh, dd, reverse):
    c = CHUNK
    r, v, kk = sh[:, 0:RW_W], sh[:, RW_W:2 * RW_W], sh[:, 2 * RW_W:3 * RW_W]
    logw, beta, k = dd[:, 0:RW_W], dd[:, RW_W:2 * RW_W], dd[:, 2 * RW_W:3 * RW_W]
    b = _cumsum_rows(logw, reverse)
    last = 0 if reverse else c - 1
    e_b = jnp.exp(b)
    e_nb = jnp.exp(-b)
    e_last = jnp.exp(b[last:last + 1, :])
    e_tot = e_last * e_nb
    abar = (kk * jnp.exp(b - logw)).astype(BF16)
    rbar = (r * e_b).astype(BF16)
    kt = (k * e_nb).astype(BF16)
    bt = (beta * e_nb).astype(BF16)
    khat = (k * e_tot).astype(BF16)
    nbhat = (-(beta * e_tot)).astype(BF16)
    vb = v.astype(BF16)
    head_a = lax.broadcasted_iota(jnp.int32, (1, LANES), 1) < RW_D
    zero = jnp.zeros((c, LANES), BF16)

    def stack(*xs):
        rows = []
        for x in xs:
            rows += [jnp.where(head_a, x, zero), jnp.where(head_a, zero, x)]
        return jnp.concatenate(rows, axis=0)

    out = []
    for p in range(RW_HEADS // 2):
        sl = slice(p * LANES, (p + 1) * LANES)
        out.append(dict(xar=stack(abar[:, sl], rbar[:, sl]), yb=stack(bt[:, sl]), yk=stack(kt[:, sl]),
                        vs=stack(vb[:, sl]), kb=stack(khat[:, sl], nbhat[:, sl]),
                        e_col=jnp.broadcast_to(e_last[:, sl], (LANES, LANES)).T))
    return out


def _rw_kernel(sf_ref, df_ref, sb_ref, db_ref, yf_ref, yb_ref, h_ref):
    @pl.when(pl.program_id(1) == 0)
    def _():
        h_ref[...] = jnp.zeros_like(h_ref)

    c = CHUNK
    c2 = 2 * c
    npair = RW_HEADS // 2
    ops = _rw_operands(sf_ref[0], df_ref[0], False) + _rw_operands(sb_ref[0], db_ref[0], True)
    cat = lambda name: jnp.stack([o[name] for o in ops], axis=0)
    xar, yb_, yk, vs, kb, e_col = (cat(n) for n in ("xar", "yb", "yk", "vs", "kb", "e_col"))

    ti = lax.broadcasted_iota(jnp.int32, (c2, c2), 0)
    si = lax.broadcasted_iota(jnp.int32, (c2, c2), 1)
    same_head = (ti // c) == (si // c)
    both = lambda fwd, bwd: jnp.concatenate([jnp.broadcast_to(fwd[None], (npair, c2, c2)),
                                             jnp.broadcast_to(bwd[None], (npair, c2, c2))], axis=0)
    strict = both(same_head & (si < ti), same_head & (si > ti))
    incl = both(same_head & (si <= ti), same_head & (si >= ti))
    eye = jnp.where(ti == si, 1.0, 0.0).astype(F32)

    gb = _bmm_nt(xar, yb_)
    gk = _bmm_nt(xar, yk)
    l_ab = jnp.where(strict, gb[:, 0:c2], 0.0)
    l_rb = jnp.where(incl, gb[:, c2:2 * c2], 0.0).astype(BF16)
    l_ak = jnp.where(strict, gk[:, 0:c2], 0.0)
    l_rk = jnp.where(incl, gk[:, c2:2 * c2], 0.0)
    t_inv = None
    s = 1
    while s < c:
        same = (ti // (2 * s)) == (si // (2 * s))
        lo, hi = (ti // s) % 2, (si // s) % 2
        off = both(same & (lo == 1) & (hi == 0), same & (lo == 0) & (hi == 1))
        l_off = jnp.where(off, l_ab, 0.0)
        if t_inv is None:
            t_inv = eye[None] - l_off
        else:
            tb = t_inv.astype(BF16)
            t_inv = t_inv - _bmm(tb, _bmm(l_off.astype(BF16), tb).astype(BF16))
        s *= 2
    h0 = h_ref[...]
    xh = _bmm(xar, h0.astype(BF16))
    lv = _bmm(jnp.concatenate([l_ak, l_rk], axis=1).astype(BF16), vs)
    ub = _bmm(t_inv.astype(BF16), (xh[:, 0:c2] + lv[:, 0:c2]).astype(BF16)).astype(BF16)
    y2 = xh[:, c2:2 * c2] + lv[:, c2:2 * c2] - _bmm(l_rb, ub)
    y = y2[:, 0:c] + y2[:, c:c2]
    h_ref[...] = e_col * h0 + _bmm_tn(kb, jnp.concatenate([vs, ub], axis=1))
    yf_ref[0] = jnp.concatenate([y[p] for p in range(npair)], axis=1)
    yb_ref[0] = jnp.concatenate([y[npair + p] for p in range(npair)], axis=1)


def _rw_scan(shared, dfw, dbw, n_ctx_chunks):
    bsz, lt, _ = shared.shape
    nc = lt // CHUNK
    fwd = lambda b_, s: (b_, s, 0)
    bwd = lambda b_, s: (b_, _bwd_chunk(s, n_ctx_chunks, nc), 0)
    return pl.pallas_call(
        _rw_kernel,
        out_shape=(jax.ShapeDtypeStruct((bsz, lt, RW_W), F32),
                   jax.ShapeDtypeStruct((bsz, lt, RW_W), F32)),
        grid=(bsz, nc),
        in_specs=[pl.BlockSpec((1, CHUNK, 3 * RW_W), fwd),
                  pl.BlockSpec((1, CHUNK, 3 * RW_W), fwd),
                  pl.BlockSpec((1, CHUNK, 3 * RW_W), bwd),
                  pl.BlockSpec((1, CHUNK, 3 * RW_W), bwd)],
        out_specs=(pl.BlockSpec((1, CHUNK, RW_W), fwd),
                   pl.BlockSpec((1, CHUNK, RW_W), bwd)),
        scratch_shapes=[pltpu.VMEM((RW_HEADS, LANES, LANES), F32)],
        compiler_params=_cparams(("arbitrary", "arbitrary")),
        name="rwkv_scan",
    )(shared, dfw, shared, dbw)


def _rank_kernel(n, ti_ref, dest_ref, meta_ref, cnt_ref, run_ref, start_ref):
    ph = pl.program_id(0)
    i = pl.program_id(1)
    ti = ti_ref[...]
    lane = lax.broadcasted_iota(jnp.int32, (ROW_TILE, LANES), 1)
    ohs = [jnp.where(ti[:, j:j + 1] == lane, 1.0, 0.0).astype(F32) for j in range(TOP_K)]
    oh = ohs[0] + ohs[1] + ohs[2] + ohs[3]
    tile_cnt = jnp.sum(oh, axis=0, keepdims=True)

    @pl.when((ph == 0) & (i == 0))
    def _():
        cnt_ref[...] = jnp.zeros_like(cnt_ref)
        run_ref[...] = jnp.zeros_like(run_ref)

    @pl.when(ph == 0)
    def _():
        cnt_ref[...] += tile_cnt

    @pl.when((ph == 0) & (i == n - 1))
    def _():
        cnt = cnt_ref[...]
        shift = MOE_BLOCK.bit_length() - 1
        padded = jnp.left_shift(jnp.right_shift(cnt.astype(jnp.int32) + (MOE_BLOCK - 1), shift),
                                shift).astype(F32)
        e0 = lax.broadcasted_iota(jnp.int32, (LANES, LANES), 0)
        e1 = lax.broadcasted_iota(jnp.int32, (LANES, LANES), 1)
        before = jnp.where(e0 < e1, 1.0, 0.0).astype(F32)
        start = _dot_hi(jnp.broadcast_to(padded, (8, LANES)), before)[0:1]
        start_ref[...] = start
        meta_ref[0:1, :] = cnt
        meta_ref[1:2, :] = start
        meta_ref[2:3, :] = padded
        meta_ref[3:8, :] = jnp.zeros((5, LANES), F32)

    @pl.when(ph == 1)
    def _():
        t0 = lax.broadcasted_iota(jnp.int32, (ROW_TILE, ROW_TILE), 0)
        t1 = lax.broadcasted_iota(jnp.int32, (ROW_TILE, ROW_TILE), 1)
        earlier = jnp.where(t1 < t0, 1.0, 0.0).astype(BF16)
        pos = (jnp.dot(earlier, oh.astype(BF16), preferred_element_type=F32)
               + run_ref[...] + start_ref[...])
        dest = jnp.zeros((ROW_TILE, LANES), F32)
        for j in range(TOP_K):
            dj = jnp.sum(ohs[j] * pos, axis=-1, keepdims=True)
            dest = jnp.where(lane == j, dj, dest)
        dest_ref[0] = dest.T[0:8, :].astype(jnp.int32)
        run_ref[...] += tile_cnt


def _moe_rank(ti):
    n = ti.shape[0]
    nt = n // ROW_TILE
    return pl.pallas_call(
        functools.partial(_rank_kernel, nt),
        out_shape=(jax.ShapeDtypeStruct((nt, 8, ROW_TILE), jnp.int32),
                   jax.ShapeDtypeStruct((8, LANES), F32)),
        grid=(2, nt),
        in_specs=[pl.BlockSpec((ROW_TILE, LANES), lambda p, i: (i, 0))],
        out_specs=(pl.BlockSpec((1, 8, ROW_TILE), lambda p, i: (i * p, 0, 0)),
                   pl.BlockSpec((8, LANES), lambda p, i: (0, 0))),
        scratch_shapes=[pltpu.VMEM((1, LANES), F32), pltpu.VMEM((1, LANES), F32),
                        pltpu.VMEM((1, LANES), F32)],
        compiler_params=_cparams(("arbitrary", "arbitrary")),
        name="moe_rank",
    )(ti)


def _row_copy(src_ref, src_row, dst_ref, dst_row, sem):
    return pltpu.make_async_copy(src_ref.at[pl.ds(src_row, 1), :], dst_ref.at[pl.ds(dst_row, 1), :], sem)


def _dispatch_kernel(dest_ref, m_ref, xb_in_ref, xb_ref, sem):
    del xb_in_ref
    i = pl.program_id(0)

    def issue(t, c):
        for j in range(TOP_K):
            _row_copy(m_ref, t, xb_ref, dest_ref[i, j * ROW_TILE + t], sem).start()
        return c

    lax.fori_loop(0, ROW_TILE, issue, 0, unroll=8)

    def drain(t, c):
        for j in range(TOP_K):
            _row_copy(m_ref, 0, xb_ref, 0, sem).wait()
        return c

    lax.fori_loop(0, ROW_TILE, drain, 0, unroll=8)


def _moe_dispatch(dest2d, m, cap):
    n, d = m.shape
    nt = n // ROW_TILE
    return pl.pallas_call(
        _dispatch_kernel,
        out_shape=jax.ShapeDtypeStruct((cap, d), m.dtype),
        grid_spec=pltpu.PrefetchScalarGridSpec(
            num_scalar_prefetch=1,
            grid=(nt,),
            in_specs=[pl.BlockSpec((ROW_TILE, d), lambda i, dst: (i, 0)),
                      pl.BlockSpec(memory_space=pl.ANY)],
            out_specs=pl.BlockSpec(memory_space=pl.ANY),
            scratch_shapes=[pltpu.SemaphoreType.DMA(())]),
        input_output_aliases={2: 0},
        compiler_params=_cparams(("arbitrary",)),
        name="moe_dispatch",
    )(dest2d, m, jnp.zeros((cap, d), m.dtype))


def _expert_kernel(be_ref, nu_ref, x_ref, wg_ref, bg_ref, wu_ref, bu_ref, wd_ref, bd_ref, y_ref,
                   wg_s, wu_s, wd_s):
    b = pl.program_id(0)
    e = be_ref[b]
    changed = (b == 0) | (e != be_ref[jnp.maximum(b - 1, 0)])

    @pl.when(changed & (b < nu_ref[0]))
    def _():
        wg_s[...] = wg_ref[0, 0].astype(BF16)
        wu_s[...] = wu_ref[0, 0].astype(BF16)
        wd_s[...] = wd_ref[0, 0].astype(BF16)

    @pl.when(b < nu_ref[0])
    def _():
        x = x_ref[...].astype(BF16)
        gt = jnp.minimum(jnp.dot(x, wg_s[...], preferred_element_type=F32) + bg_ref[0, 0], SWIGLU_LIMIT)
        up = jnp.clip(jnp.dot(x, wu_s[...], preferred_element_type=F32) + bu_ref[0, 0],
                      -SWIGLU_LIMIT, SWIGLU_LIMIT)
        act = (up + 1.0) * gt * jax.nn.sigmoid(SWIGLU_ALPHA * gt)
        y_ref[...] = jnp.dot(act.astype(BF16), wd_s[...], preferred_element_type=F32) + bd_ref[0, 0]

    @pl.when(b >= nu_ref[0])
    def _():
        y_ref[...] = jnp.zeros_like(y_ref)


def _moe_experts(layer, block_expert, n_used, xb, wg, bg, wu, bu, wd, bd):
    cap, d = xb.shape
    nl, ne, _, f = wg.shape
    nb = cap // MOE_BLOCK
    wmap = lambda b, be, nu: (layer, be[jnp.maximum(jnp.minimum(b, nu[0] - 1), 0)], 0, 0)
    return pl.pallas_call(
        _expert_kernel,
        out_shape=jax.ShapeDtypeStruct((cap, d), F32),
        grid_spec=pltpu.PrefetchScalarGridSpec(
            num_scalar_prefetch=2,
            grid=(nb,),
            in_specs=[pl.BlockSpec((MOE_BLOCK, d), lambda b, be, nu: (b, 0)),
                      pl.BlockSpec((1, 1, d, f), wmap), pl.BlockSpec((1, 1, 1, f), wmap),
                      pl.BlockSpec((1, 1, d, f), wmap), pl.BlockSpec((1, 1, 1, f), wmap),
                      pl.BlockSpec((1, 1, f, d), wmap), pl.BlockSpec((1, 1, 1, d), wmap)],
            out_specs=pl.BlockSpec((MOE_BLOCK, d), lambda b, be, nu: (b, 0)),
            scratch_shapes=[pltpu.VMEM((d, f), BF16), pltpu.VMEM((d, f), BF16), pltpu.VMEM((f, d), BF16)]),
        compiler_params=_cparams(("arbitrary",)),
        name="moe_experts",
    )(block_expert, n_used, xb, wg, bg.reshape(nl, ne, 1, f), wu, bu.reshape(nl, ne, 1, f),
      wd, bd.reshape(nl, ne, 1, d))


def _combine_kernel(alpha, n_tiles,
                    dest_ref, gt_ref, h_ref, mod_ref, vec_ref, yb_ref, o_ref, buf, sem):
    bi = pl.program_id(0)
    i = pl.program_id(1)
    tile = bi * n_tiles + i

    def issue(t, c):
        for j in range(TOP_K):
            _row_copy(yb_ref, dest_ref[tile, j * ROW_TILE + t], buf.at[j], t, sem).start()
        return c

    lax.fori_loop(0, ROW_TILE, issue, 0, unroll=8)

    def drain(t, c):
        for j in range(TOP_K):
            _row_copy(yb_ref, 0, buf.at[j], 0, sem).wait()
        return c

    lax.fori_loop(0, ROW_TILE, drain, 0, unroll=8)
    gt = gt_ref[0]
    f = gt[:, 0:1] * buf[0]
    for j in range(1, TOP_K):
        f = f + gt[:, j:j + 1] * buf[j]
    g2 = mod_ref[0, 0, 0:1, :]
    o_ref[0] = _ln(alpha * h_ref[0] + g2 * f, vec_ref[0:1, :], vec_ref[1:2, :], LN_EPS)


def _moe_combine(alpha, n_ctx_tiles, tile0, dest2d, gates, h1, mod3, vec, yb):
    bsz, rows, d = h1.shape
    nt = rows // ROW_TILE
    row = lambda b_, i, dst: (b_, i, 0)
    return pl.pallas_call(
        functools.partial(_combine_kernel, alpha, nt),
        out_shape=jax.ShapeDtypeStruct((bsz, rows, d), F32),
        grid_spec=pltpu.PrefetchScalarGridSpec(
            num_scalar_prefetch=1,
            grid=(bsz, nt),
            in_specs=[pl.BlockSpec((1, ROW_TILE, LANES), row),
                      pl.BlockSpec((1, ROW_TILE, d), row),
                      pl.BlockSpec((1, 1, 1, d),
                                   lambda b_, i, dst: (b_, jnp.where(tile0 + i < n_ctx_tiles, 0, 1), 0, 0)),
                      pl.BlockSpec((2, d), lambda b_, i, dst: (0, 0)),
                      pl.BlockSpec(memory_space=pl.ANY)],
            out_specs=pl.BlockSpec((1, ROW_TILE, d), row),
            scratch_shapes=[pltpu.VMEM((TOP_K, ROW_TILE, d), F32), pltpu.SemaphoreType.DMA(())]),
        compiler_params=_cparams(("arbitrary", "arbitrary")),
        name="moe_combine",
    )(dest2d, gates, h1, mod3, vec, yb)


def _moe(layer, alpha, n_ctx_tiles, tile0, h1, m, ti, gates, mod3, ln2, wg, bg, wu, bu, wd, bd):
    bsz, rows, d = h1.shape
    n = bsz * rows
    dest, meta = _moe_rank(ti.reshape(n, LANES))
    dest2d = dest.reshape(n // ROW_TILE, 8 * ROW_TILE)
    n_blocks = -(-(n * TOP_K) // MOE_BLOCK) + N_EXPERTS
    cap = n_blocks * MOE_BLOCK
    pad_end = (meta[1, :N_EXPERTS] + meta[2, :N_EXPERTS]).astype(jnp.int32)
    block_row = jnp.arange(n_blocks, dtype=jnp.int32) * MOE_BLOCK
    block_expert = jnp.minimum(jnp.sum((pad_end[None, :] <= block_row[:, None]).astype(jnp.int32), axis=1),
                               N_EXPERTS - 1)
    n_used = (pad_end[-1:] // MOE_BLOCK).astype(jnp.int32)
    xb = _moe_dispatch(dest2d, m.reshape(n, d), cap)
    yb = _moe_experts(layer, block_expert, n_used, xb, wg, bg, wu, bu, wd, bd)
    return _moe_combine(alpha, n_ctx_tiles, tile0, dest2d, gates, h1, mod3, ln2, yb)


def kernel(x, c, ctx, c_ctx, ln_in_g, ln_in_b, ada_w, ada_b, w_in, lam_q1, lam_k1, lam_q2, lam_k2,
           da_subln_g, gla_gate_w2, gla_gate_b, gla_norm_g, rw_conv_w, rw_w2, rw_w0, rw_a2, rw_a0,
           rw_g2, rw_k_k, rw_k_a, rw_r_k, rw_lnx_g, rw_lnx_b, w_out, ln1_g, ln1_b, router_w, router_b,
           moe_w_gate, moe_b_gate, moe_w_up, moe_b_up, moe_w_down, moe_b_down, ln2_g, ln2_b):
    bsz, seq, d = x.shape
    n_ctx = ctx.shape[1]
    depth = w_in.shape[0]
    assert n_ctx % ROW_TILE == 0 and seq % ROW_TILE == 0 and seq % GRID_W == 0
    assert w_in.shape[2] == 3488 and bsz + 1 <= 8
    nct = n_ctx // ROW_TILE
    ncc = n_ctx // CHUNK
    nt = (n_ctx + seq) // ROW_TILE
    alpha = (2 * depth) ** 0.25

    c_all = jnp.concatenate([c, c_ctx[None], jnp.zeros((8 - bsz - 1, d), F32)], axis=0)
    mods = _ada_mod(c_all, ada_w, ada_b).reshape(depth, 8, 6, d)
    h = _ln_in(ctx, x, ln_in_g, ln_in_b)
    cos_t, sin_t = _rope_tables(n_ctx, seq)

    def pick(l, idx):
        mc = jnp.broadcast_to(mods[l, bsz][None, idx], (bsz, len(idx), d))
        return jnp.stack([mc, mods[l, :bsz][:, idx]], axis=1)

    for l in range(depth):
        last = l == depth - 1
        tile0 = nct if last else 0
        n_out_tiles = nt - tile0
        lam_init = 0.8 - 0.6 * math.exp(-0.3 * l)
        lam = (jnp.exp(jnp.sum(lam_q1[l] * lam_k1[l])) - jnp.exp(jnp.sum(lam_q2[l] * lam_k2[l])) + lam_init)

        qk, v, zg, zr = _inproj(h, pick(l, [1, 0]), _pack_w_in(w_in[l]), cos_t, sin_t, nct)
        ao = _attention(qk, v, lam, nct, nt - nct, n_ctx + seq)
        if not last:
            ao = jnp.concatenate([_attention(qk, v, lam, 0, nct, n_ctx), ao], axis=1)
        w2p, biasp = _pack_gla_gate(gla_gate_w2[l], gla_gate_b[l])
        gof, gob = _gla_scan(zg, w2p, biasp, ncc)
        rvec = jnp.stack([rw_k_k[l], rw_k_a[l], rw_r_k[l].reshape(-1), rw_w0[l, 0], rw_w0[l, 1],
                          rw_a0[l, 0], rw_a0[l, 1], jnp.zeros((RW_W,), F32)], axis=0)
        shared, dfw, dbw, rg, bonus = _rw_prep(zr, rw_conv_w[l], _block2(rw_w2[l]), _block2(rw_a2[l]),
                                               rw_g2[l], rvec, nct)
        yf, yb = _rw_scan(shared, dfw, dbw, ncc)

        nrm = jnp.stack([jnp.pad(jnp.tile(da_subln_g[l], DA_HEADS), (0, GLA_W - DA_W)),
                         jnp.tile(gla_norm_g[l], GLA_HEADS), rw_lnx_g[l], rw_lnx_b[l]], axis=0)
        rw_p = jnp.pad(router_w[l], ((0, 0), (0, LANES - N_EXPERTS)))
        rb_p = jnp.pad(router_b[l], (0, LANES - N_EXPERTS), constant_values=-1e30).reshape(1, LANES)
        h1, m, ti, gates = _mixout(alpha, 1.0 - lam_init, tile0, n_out_tiles, nct,
                                   ao, gof, gob, zg, yf, yb, bonus, rg, h, pick(l, [2, 4, 3]),
                                   w_out[l].astype(BF16), jnp.stack([ln1_g[l], ln1_b[l]], 0), nrm, rw_p, rb_p)
        h = _moe(l, alpha, nct, tile0, h1, m, ti, gates, pick(l, [5]), jnp.stack([ln2_g[l], ln2_b[l]], 0),
                 moe_w_gate, moe_b_gate, moe_w_up, moe_b_up, moe_w_down, moe_b_down)
    return h
```

```python
import functools
import math

import jax
import jax.numpy as jnp
import numpy as np
from jax import lax
from jax.experimental import pallas as pl
from jax.experimental.pallas import tpu as pltpu

F32 = jnp.float32
BF16 = jnp.bfloat16
HI = lax.Precision.HIGHEST

GRID_W = 64
DA_HEADS, DA_QK, DA_V = 4, 32, 64
GLA_HEADS, GLA_DK, GLA_DV, GLA_RANK, GLA_TAU = 6, 32, 64, 16, 16.0
RW_HEADS, RW_D, RW_DECAY_RANK, RW_A_RANK, RW_GATE_RANK = 6, 64, 64, 64, 128
RW_GN_EPS = 64e-5
N_EXPERTS, TOP_K = 32, 4
SWIGLU_LIMIT, SWIGLU_ALPHA = 7.0, 1.702
ROPE_BASE = 10000.0
LN_EPS = 1e-5

DA_W = DA_HEADS * DA_V
GLA_KW = GLA_HEADS * GLA_DK
GLA_W = GLA_HEADS * GLA_DV
RW_W = RW_HEADS * RW_D
MIX_W = DA_W + GLA_W + RW_W

LANES = 128
ROW_TILE = 256
CHUNK = 64
SCAN_BATCH = 4
MOE_BLOCK = 256
VMEM_LIMIT = 56 * 1024 * 1024
SUBLANES = 8

Z_ATTN = 3 * DA_W
GLA_KP = 256
Z_GLA = 2 * GLA_KP + 2 * GLA_W + LANES
Z_RW = 3 * RW_W + 3 * LANES
Z_ALL = Z_ATTN + Z_GLA + Z_RW


def _cparams(sem):
    return pltpu.CompilerParams(dimension_semantics=sem, vmem_limit_bytes=VMEM_LIMIT)


def _ln(x, g, b, eps):
    mu = jnp.mean(x, axis=-1, keepdims=True)
    xc = x - mu
    var = jnp.mean(xc * xc, axis=-1, keepdims=True)
    return xc * lax.rsqrt(var + eps) * g + b


def _silu(x):
    return x * jax.nn.sigmoid(x)


def _dot(a, b):
    return jnp.dot(a.astype(BF16), b.astype(BF16), preferred_element_type=F32)


def _dot_hi(a, b):
    return jnp.dot(a, b, precision=HI, preferred_element_type=F32)


def _group_sum(x, gmat):
    hi = x.astype(BF16)
    lo = (x - hi.astype(F32)).astype(BF16)
    return (jnp.dot(hi, gmat, preferred_element_type=F32)
            + jnp.dot(lo, gmat, preferred_element_type=F32))


def _group_matrix(width, group):
    idx = np.arange(width) // group
    return jnp.asarray((idx[:, None] == idx[None, :]).astype(np.float32), dtype=BF16)


def _ada_kernel(c_ref, w_ref, b_ref, o_ref):
    o_ref[0] = _dot_hi(_silu(c_ref[...]), w_ref[0]) + b_ref[0]


def _ada_mod(c_all, ada_w, ada_b):
    nl, d, n6 = ada_w.shape
    tn = 1536
    return pl.pallas_call(
        _ada_kernel,
        out_shape=jax.ShapeDtypeStruct((nl, c_all.shape[0], n6), F32),
        grid=(nl, n6 // tn),
        in_specs=[pl.BlockSpec((c_all.shape[0], d), lambda l, j: (0, 0)),
                  pl.BlockSpec((1, d, tn), lambda l, j: (l, 0, j)),
                  pl.BlockSpec((1, 1, tn), lambda l, j: (l, 0, j))],
        out_specs=pl.BlockSpec((1, c_all.shape[0], tn), lambda l, j: (l, 0, j)),
        compiler_params=_cparams(("arbitrary", "arbitrary")),
        name="ada_mod",
    )(c_all, ada_w, ada_b.reshape(nl, 1, n6))


def _ln_in_kernel(n_ctx_tiles, c_ref, x_ref, g_ref, b_ref, o_ref):
    @pl.when(pl.program_id(1) < n_ctx_tiles)
    def _():
        o_ref[0] = _ln(c_ref[0], g_ref[...], b_ref[...], LN_EPS)

    @pl.when(pl.program_id(1) >= n_ctx_tiles)
    def _():
        o_ref[0] = _ln(x_ref[0], g_ref[...], b_ref[...], LN_EPS)


def _ln_in(ctx, x, g, b):
    bsz, n_ctx, d = ctx.shape
    nct = n_ctx // ROW_TILE
    nt = nct + x.shape[1] // ROW_TILE
    return pl.pallas_call(
        functools.partial(_ln_in_kernel, nct),
        out_shape=jax.ShapeDtypeStruct((bsz, nt * ROW_TILE, d), F32),
        grid=(bsz, nt),
        in_specs=[pl.BlockSpec((1, ROW_TILE, d), lambda b_, i: (b_, jnp.minimum(i, nct - 1), 0)),
                  pl.BlockSpec((1, ROW_TILE, d), lambda b_, i: (b_, jnp.maximum(i - nct, 0), 0)),
                  pl.BlockSpec((1, d), lambda b_, i: (0, 0)),
                  pl.BlockSpec((1, d), lambda b_, i: (0, 0))],
        out_specs=pl.BlockSpec((1, ROW_TILE, d), lambda b_, i: (b_, i, 0)),
        compiler_params=_cparams(("arbitrary", "arbitrary")),
        name="ln_in",
    )(ctx, x, g.reshape(1, d), b.reshape(1, d))


def _inproj_kernel(h_ref, mod_ref, w_ref, cos_ref, sin_ref, qk_ref, v_ref, gla_ref, rw_ref):
    h = h_ref[0]
    sc = mod_ref[0, 0, 0:1, :]
    sh = mod_ref[0, 0, 1:2, :]
    xm = (h * (1.0 + sc) + sh).astype(BF16)
    qk = jnp.dot(xm, w_ref[:, 0:2 * DA_W], preferred_element_type=F32)
    cos = cos_ref[...]
    sin = sin_ref[...]
    lane = lax.broadcasted_iota(jnp.int32, (1, LANES), 1)
    first = (lane % 16) < 8
    qscale = DA_QK ** -0.5
    for j in range(4):
        x = qk[:, j * LANES:(j + 1) * LANES]
        rot = jnp.where(first, pltpu.roll(x, LANES - 8, 1), pltpu.roll(x, 8, 1))
        y = x * cos + rot * sin
        if j < 2:
            y = y * qscale
        qk_ref[0, :, j * LANES:(j + 1) * LANES] = y.astype(BF16)
    v_ref[0] = jnp.dot(xm, w_ref[:, 2 * DA_W:Z_ATTN], preferred_element_type=F32).astype(BF16)
    gla_ref[0] = jnp.dot(xm, w_ref[:, Z_ATTN:Z_ATTN + Z_GLA], preferred_element_type=F32)
    rw_ref[0] = jnp.dot(xm, w_ref[:, Z_ATTN + Z_GLA:Z_ALL], preferred_element_type=F32)


def _inproj(h, mod1, w_p, cos_t, sin_t, n_ctx_tiles):
    bsz, lt, d = h.shape
    nt = lt // ROW_TILE
    row = lambda b_, i: (b_, i, 0)
    return pl.pallas_call(
        _inproj_kernel,
        out_shape=(jax.ShapeDtypeStruct((bsz, lt, 2 * DA_W), BF16),
                   jax.ShapeDtypeStruct((bsz, lt, DA_W), BF16),
                   jax.ShapeDtypeStruct((bsz, lt, Z_GLA), F32),
                   jax.ShapeDtypeStruct((bsz, lt, Z_RW), F32)),
        grid=(bsz, nt),
        in_specs=[pl.BlockSpec((1, ROW_TILE, d), row),
                  pl.BlockSpec((1, 1, 2, d),
                               lambda b_, i: (b_, jnp.where(i < n_ctx_tiles, 0, 1), 0, 0)),
                  pl.BlockSpec((d, Z_ALL), lambda b_, i: (0, 0)),
                  pl.BlockSpec((ROW_TILE, LANES), lambda b_, i: (i, 0)),
                  pl.BlockSpec((ROW_TILE, LANES), lambda b_, i: (i, 0))],
        out_specs=(pl.BlockSpec((1, ROW_TILE, 2 * DA_W), row),
                   pl.BlockSpec((1, ROW_TILE, DA_W), row),
                   pl.BlockSpec((1, ROW_TILE, Z_GLA), row),
                   pl.BlockSpec((1, ROW_TILE, Z_RW), row)),
        compiler_params=_cparams(("arbitrary", "arbitrary")),
        name="inproj",
    )(h, mod1, w_p, cos_t, sin_t)


def _pack_w_in(w):
    d = w.shape[0]
    o = 0
    parts = {}
    for name, n in (("da_q", 256), ("da_k", 256), ("da_v", 256), ("gla_q", GLA_KW), ("gla_k", GLA_KW),
                    ("gla_v", GLA_W), ("gla_gf", GLA_RANK), ("gla_gb", GLA_RANK), ("gla_r", GLA_W),
                    ("rw_rkv", 3 * RW_W), ("rw_wf", 64), ("rw_wb", 64), ("rw_af", 64), ("rw_ab", 64),
                    ("rw_g", 128)):
        parts[name] = w[:, o:o + n]
        o += n
    z = lambda n: jnp.zeros((d, n), w.dtype)
    cols = [parts["da_q"], parts["da_k"], parts["da_v"],
            parts["gla_q"], z(GLA_KP - GLA_KW), parts["gla_k"], z(GLA_KP - GLA_KW),
            parts["gla_v"], parts["gla_r"], parts["gla_gf"], parts["gla_gb"], z(LANES - 2 * GLA_RANK),
            parts["rw_rkv"], parts["rw_wf"], parts["rw_wb"], parts["rw_af"], parts["rw_ab"], parts["rw_g"]]
    return jnp.concatenate(cols, axis=1).astype(BF16)


def _rope_tables(n_ctx, seq):
    t = np.arange(seq)
    row = (t // GRID_W).astype(np.float32)
    col = (t % GRID_W).astype(np.float32)
    quarter = DA_QK // 4
    inv = (ROPE_BASE ** (-np.arange(quarter, dtype=np.float32) / quarter)).astype(np.float32)
    ang_r = row[:, None] * inv
    ang_c = col[:, None] * inv
    ang = np.concatenate([ang_r, ang_r, ang_c, ang_c], -1).astype(np.float32)
    cos = np.tile(np.cos(ang), (1, LANES // DA_QK))
    sin = np.tile(np.sin(ang), (1, LANES // DA_QK))
    sign = np.where((np.arange(LANES) % 16) < 8, -1.0, 1.0).astype(np.float32)
    cos = np.concatenate([np.ones((n_ctx, LANES), np.float32), cos], 0)
    sin = np.concatenate([np.zeros((n_ctx, LANES), np.float32), sin * sign], 0)
    return jnp.asarray(cos, F32), jnp.asarray(sin, F32)


ATTN_KEY_CHUNK = 1280


def _attn_kernel(lam_ref, q_ref, k_ref, v_ref, o_ref):
    q = q_ref[0]
    lam = lam_ref[0, 0]
    tq = q.shape[0]
    n_keys = k_ref.shape[1]
    lane = lax.broadcasted_iota(jnp.int32, (1, LANES), 1)
    zero = jnp.zeros_like(q)
    outs = []
    for hh in range(2):
        q2 = jnp.concatenate(
            [jnp.where((lane >= hh * 64 + m * DA_QK) & (lane < hh * 64 + (m + 1) * DA_QK), q, zero)
             for m in range(2)], axis=0)
        m_run = l_run = acc = None
        for k0 in range(0, n_keys, ATTN_KEY_CHUNK):
            k1 = min(k0 + ATTN_KEY_CHUNK, n_keys)
            s = _bdot_nt(q2, k_ref[0, k0:k1, :])
            m_blk = jnp.max(s, axis=-1, keepdims=True)
            m_new = m_blk if m_run is None else jnp.maximum(m_run, m_blk)
            p = jnp.exp((s - m_new).astype(BF16))
            v_ext = jnp.concatenate([v_ref[0, k0:k1, :], jnp.ones((k1 - k0, LANES), BF16)], axis=1)
            pv_ext = jnp.dot(p, v_ext, preferred_element_type=F32)
            pv, p_sum = pv_ext[:, 0:LANES], pv_ext[:, LANES:2 * LANES]
            if m_run is None:
                l_run, acc = p_sum, pv
            else:
                scale = jnp.exp(m_run - m_new)
                l_run = scale * l_run + p_sum
                acc = scale * acc + pv
            m_run = m_new
        o = acc / l_run
        outs.append(o[0:tq] - lam * o[tq:2 * tq])
    o_ref[0] = jnp.where(lane < 64, outs[0], outs[1])


def _attention(qk, v, lam, q_tile0, n_q_tiles, n_k_rows):
    bsz, lt, _ = qk.shape
    out = pl.pallas_call(
        _attn_kernel,
        out_shape=jax.ShapeDtypeStruct((bsz, n_q_tiles * ROW_TILE, DA_W), F32),
        grid=(bsz, 2, n_q_tiles),
        in_specs=[pl.BlockSpec(memory_space=pltpu.SMEM),
                  pl.BlockSpec((1, ROW_TILE, LANES), lambda b_, p, i: (b_, q_tile0 + i, p)),
                  pl.BlockSpec((1, n_k_rows, LANES), lambda b_, p, i: (b_, 0, 2 + p)),
                  pl.BlockSpec((1, n_k_rows, LANES), lambda b_, p, i: (b_, 0, p))],
        out_specs=pl.BlockSpec((1, ROW_TILE, LANES), lambda b_, p, i: (b_, i, p)),
        compiler_params=_cparams(("arbitrary", "arbitrary", "arbitrary")),
        name="diff_attn",
    )(lam.reshape(1, 1), qk, qk, v)
    return out


def _mixout_kernel(alpha, sub_scale,
                   ao_ref, gof_ref, gob_ref, zg_ref, yf_ref, yb_ref, bonus_ref, rg_ref, h_ref, mod_ref,
                   wo_ref, vec_ref, nrm_ref, g256_ref, g384_ref, rw_ref, rb_ref,
                   h1_ref, m_ref, ti_ref, gt_ref):
    a = ao_ref[0]
    a = a * lax.rsqrt(_group_sum(a * a, g256_ref[...]) * (1.0 / DA_V) + LN_EPS) * nrm_ref[0:1, 0:DA_W] * sub_scale
    o = gof_ref[0] + gob_ref[0]
    r = zg_ref[0, :, 2 * GLA_KP + GLA_W:2 * GLA_KP + 2 * GLA_W]
    gl = (o * lax.rsqrt(_group_sum(o * o, g384_ref[...]) * (1.0 / GLA_DV) + LN_EPS)
          * nrm_ref[1:2, :] * _silu(r))
    y = yf_ref[0] + yb_ref[0]
    mu = _group_sum(y, g384_ref[...]) * (1.0 / RW_D)
    yc = y - mu
    var = _group_sum(yc * yc, g384_ref[...]) * (1.0 / RW_D)
    yn = yc * lax.rsqrt(var + RW_GN_EPS) * nrm_ref[2:3, :] + nrm_ref[3:4, :]
    rw = (yn + bonus_ref[0]) * rg_ref[0]
    mix = (_dot(a, wo_ref[0:DA_W, :]) + _dot(gl, wo_ref[DA_W:DA_W + GLA_W, :])
           + _dot(rw, wo_ref[DA_W + GLA_W:MIX_W, :]))
    g1 = mod_ref[0, 0, 0:1, :]
    sc2 = mod_ref[0, 0, 1:2, :]
    sh2 = mod_ref[0, 0, 2:3, :]
    h1 = _ln(alpha * h_ref[0] + g1 * mix, vec_ref[0:1, :], vec_ref[1:2, :], LN_EPS)
    h1_ref[0] = h1
    m = h1 * (1.0 + sc2) + sh2
    for c in range(m.shape[1] // LANES):
        m_ref[0, pl.ds(c, ROW_TILE, stride=SUBLANES), :] = m[:, c * LANES:(c + 1) * LANES]
    logits = _dot_hi(m, rw_ref[...]) + rb_ref[...]
    lane = lax.broadcasted_iota(jnp.int32, logits.shape, 1)
    ti = jnp.zeros(logits.shape, jnp.int32)
    tv = jnp.full(logits.shape, -1e30, F32)
    for j in range(TOP_K):
        mx = jnp.max(logits, axis=-1, keepdims=True)
        idx = jnp.min(jnp.where(logits == mx, lane, LANES), axis=-1, keepdims=True)
        ti = jnp.where(lane == j, idx, ti)
        tv = jnp.where(lane == j, mx, tv)
        logits = jnp.where(lane == idx, -jnp.inf, logits)
    e = jnp.exp(tv - jnp.max(tv, axis=-1, keepdims=True))
    ti_ref[0] = ti
    gt_ref[0] = e / jnp.sum(e, axis=-1, keepdims=True)


def _mixout(alpha, sub_scale, tile0, n_tiles, n_ctx_tiles,
            ao, gof, gob, zg, yf, yb, bonus, rg, h, mod2, wo, vec, nrm, rw_p, rb_p):
    bsz, lt, d = h.shape
    row = lambda b_, i: (b_, tile0 + i, 0)
    orow = lambda b_, i: (b_, i, 0)
    full = lambda shape: pl.BlockSpec(shape, lambda b_, i: tuple(0 for _ in shape))
    nrows = n_tiles * ROW_TILE
    return pl.pallas_call(
        functools.partial(_mixout_kernel, alpha, sub_scale),
        out_shape=(jax.ShapeDtypeStruct((bsz, nrows, d), F32),
                   jax.ShapeDtypeStruct((bsz, nrows * (d // LANES), LANES), F32),
                   jax.ShapeDtypeStruct((bsz, nrows, LANES), jnp.int32),
                   jax.ShapeDtypeStruct((bsz, nrows, LANES), F32)),
        grid=(bsz, n_tiles),
        in_specs=[pl.BlockSpec((1, ROW_TILE, DA_W), orow),
                  pl.BlockSpec((1, ROW_TILE, GLA_W), row),
                  pl.BlockSpec((1, ROW_TILE, GLA_W), row),
                  pl.BlockSpec((1, ROW_TILE, Z_GLA), row),
                  pl.BlockSpec((1, ROW_TILE, RW_W), row),
                  pl.BlockSpec((1, ROW_TILE, RW_W), row),
                  pl.BlockSpec((1, ROW_TILE, RW_W), row),
                  pl.BlockSpec((1, ROW_TILE, RW_W), row),
                  pl.BlockSpec((1, ROW_TILE, d), row),
                  pl.BlockSpec((1, 1, 3, d),
                               lambda b_, i: (b_, jnp.where(tile0 + i < n_ctx_tiles, 0, 1), 0, 0)),
                  full((MIX_W, d)), full((2, d)), full((4, GLA_W)),
                  full((DA_W, DA_W)), full((GLA_W, GLA_W)), full((d, LANES)), full((1, LANES))],
        out_specs=(pl.BlockSpec((1, ROW_TILE, d), orow),
                   pl.BlockSpec((1, ROW_TILE * (d // LANES), LANES), orow),
                   pl.BlockSpec((1, ROW_TILE, LANES), orow),
                   pl.BlockSpec((1, ROW_TILE, LANES), orow)),
        compiler_params=_cparams(("arbitrary", "arbitrary")),
        name="mix_out",
    )(ao, gof, gob, zg, yf, yb, bonus, rg, h, mod2, wo, vec, nrm,
      _group_matrix(DA_W, DA_V), _group_matrix(GLA_W, GLA_DV), rw_p, rb_p)


def _bwd_chunk(step, n_ctx_chunks, n_chunks):
    return jnp.where(step < n_ctx_chunks, n_ctx_chunks - 1 - step, n_chunks + n_ctx_chunks - 1 - step)


def _tri(n, reverse, strict):
    t = lax.broadcasted_iota(jnp.int32, (n, n), 0)
    s = lax.broadcasted_iota(jnp.int32, (n, n), 1)
    if reverse:
        return (s > t) if strict else (s >= t)
    return (s < t) if strict else (s <= t)


def _log_sigmoid(x):
    return jnp.minimum(x, 0.0) - jnp.log1p(jnp.exp(-jnp.abs(x)))


def _softplus(x):
    return jnp.maximum(x, 0.0) + jnp.log1p(jnp.exp(-jnp.abs(x)))


def _split3(x):
    t0 = x.astype(BF16)
    r1 = x - t0.astype(F32)
    t1 = r1.astype(BF16)
    t2 = (r1 - t1.astype(F32)).astype(BF16)
    return t0, t1, t2


def _cumsum_rows(x, reverse):
    tri = jnp.where(_tri(CHUNK, reverse, False), 1.0, 0.0).astype(BF16)
    return sum(jnp.dot(tri, t, preferred_element_type=F32) for t in _split3(x))


def _bdot_nt(a, b):
    return lax.dot_general(a, b, (((1,), (1,)), ((), ())), preferred_element_type=F32)


GLA_SUB = 16
GLA_EXP_CLAMP = 60.0


def _gla_kernel(zf_ref, zb_ref, w2_ref, bias_ref, of_ref, ob_ref, st_ref):
    @pl.when(pl.program_id(1) == 0)
    def _():
        st_ref[...] = jnp.zeros_like(st_ref)

    c = CHUNK
    nsub = c // GLA_SUB
    nb = zf_ref.shape[0]
    qs, ks, vs, bs, b_lasts = [], [], [], [], []
    for i, reverse, z_ref in [(i, rv, zr) for i in range(nb) for rv, zr in ((False, zf_ref), (True, zb_ref))]:
        zg = z_ref[i]
        col0 = GLA_KP if reverse else 0
        gpre = _dot(zg[:, 2 * GLA_KP + 2 * GLA_W:Z_GLA], w2_ref[:, col0:col0 + GLA_KP]) \
            + bias_ref[:, col0:col0 + GLA_KP]
        b = _cumsum_rows(_log_sigmoid(gpre) * (1.0 / GLA_TAU), reverse)
        last = 0 if reverse else c - 1
        qs.append(zg[:, 0:GLA_KP] * (GLA_DK ** -0.5))
        ks.append(zg[:, GLA_KP:2 * GLA_KP])
        vs.append(zg[:, 2 * GLA_KP:2 * GLA_KP + GLA_W])
        bs.append(b)
        b_lasts.append(b[last:last + 1, :])
    q, k, v, b, b_last = (jnp.stack(x, axis=0) for x in (qs, ks, vs, bs, b_lasts))
    vb = v.astype(BF16)
    st = st_ref[...]
    inter = _bmm_nt((q * jnp.exp(b)).astype(BF16), st.astype(BF16))
    kv = _bmm_tn(vb, (k * jnp.exp(b_last - b)).astype(BF16))
    vi = lax.broadcasted_iota(jnp.int32, (GLA_W, GLA_KP), 0) // GLA_DV
    ki = lax.broadcasted_iota(jnp.int32, (GLA_W, GLA_KP), 1) // GLA_DK
    st_ref[...] = st * jnp.exp(b_last) + jnp.where((vi == ki)[None], kv, 0.0)

    lane_head = lax.broadcasted_iota(jnp.int32, (1, 1, GLA_KP), 2) // GLA_DK
    per_head = [[None] * nsub for _ in range(GLA_HEADS)]
    for i in range(nsub):
        r0 = i * GLA_SUB
        b_ref = jnp.stack([b[p, r0:r0 + 1] if p % 2 == 0 else b[p, r0 + GLA_SUB - 1:r0 + GLA_SUB]
                           for p in range(2 * nb)], axis=0)
        qi = q[:, r0:r0 + GLA_SUB] * jnp.exp(b[:, r0:r0 + GLA_SUB] - b_ref)
        ki_ = (k * jnp.exp(jnp.minimum(b_ref - b, GLA_EXP_CLAMP))).astype(BF16)
        qh = jnp.concatenate([jnp.where(lane_head == h, qi, 0.0) for h in range(GLA_HEADS)], axis=1)
        a = _bmm_nt(qh.astype(BF16), ki_)
        for h in range(GLA_HEADS):
            per_head[h][i] = a[:, h * GLA_SUB:(h + 1) * GLA_SUB]
    causal = jnp.stack([_tri(c, False, False), _tri(c, True, False)] * nb, axis=0)
    lane = lax.broadcasted_iota(jnp.int32, (1, 1, LANES), 2)
    pieces = []
    for p in range(GLA_HEADS // 2):
        vp = vb[:, :, p * LANES:(p + 1) * LANES]
        halves = []
        for hh in range(2):
            a_h = jnp.where(causal, jnp.concatenate(per_head[2 * p + hh], axis=1), 0.0)
            halves.append(_bmm(a_h.astype(BF16), vp))
        pieces.append(jnp.where(lane < GLA_DV, halves[0], halves[1]))
    o = inter + jnp.concatenate(pieces, axis=2)
    for i in range(nb):
        of_ref[i] = o[2 * i]
        ob_ref[i] = o[2 * i + 1]


def _gla_scan(zg, w2p, biasp, n_ctx_chunks):
    bsz, lt, _ = zg.shape
    nc = lt // CHUNK
    nb = SCAN_BATCH if bsz % SCAN_BATCH == 0 else 1
    fwd = lambda b_, s: (b_, s, 0)
    bwd = lambda b_, s: (b_, _bwd_chunk(s, n_ctx_chunks, nc), 0)
    return pl.pallas_call(
        _gla_kernel,
        out_shape=(jax.ShapeDtypeStruct((bsz, lt, GLA_W), F32),
                   jax.ShapeDtypeStruct((bsz, lt, GLA_W), F32)),
        grid=(bsz // nb, nc),
        in_specs=[pl.BlockSpec((nb, CHUNK, Z_GLA), fwd),
                  pl.BlockSpec((nb, CHUNK, Z_GLA), bwd),
                  pl.BlockSpec((LANES, 2 * GLA_KP), lambda b_, s: (0, 0)),
                  pl.BlockSpec((1, 2 * GLA_KP), lambda b_, s: (0, 0))],
        out_specs=(pl.BlockSpec((nb, CHUNK, GLA_W), fwd),
                   pl.BlockSpec((nb, CHUNK, GLA_W), bwd)),
        scratch_shapes=[pltpu.VMEM((2 * nb, GLA_W, GLA_KP), F32)],
        compiler_params=_cparams(("arbitrary", "arbitrary")),
        name="gla_scan",
    )(zg, zg, w2p, biasp)


def _pack_gla_gate(w2, bias):
    w = jnp.zeros((LANES, 2 * GLA_KP), F32)
    w = w.at[0:GLA_RANK, 0:GLA_KW].set(w2[0]).at[GLA_RANK:2 * GLA_RANK, GLA_KP:GLA_KP + GLA_KW].set(w2[1])
    b = jnp.zeros((1, 2 * GLA_KP), F32)
    b = b.at[0, 0:GLA_KW].set(bias[0]).at[0, GLA_KP:GLA_KP + GLA_KW].set(bias[1])
    return w, b


def _rwprep_kernel(n_ctx_tiles, n_tiles,
                   z_ref, zp_ref, zn_ref, cw_ref, w2_ref, a2_ref, g2_ref, vec_ref, gm_ref,
                   sh_ref, df_ref, db_ref, g_ref, bonus_ref):
    i = pl.program_id(1)
    z = z_ref[0]
    x = z[:, 0:3 * RW_W]
    seg_first = (i == 0) | (i == n_ctx_tiles)
    seg_last = (i == n_ctx_tiles - 1) | (i == n_tiles - 1)
    prev_row = jnp.where(seg_first, 0.0, zp_ref[0, 7:8, 0:3 * RW_W])
    next_row = jnp.where(seg_last, 0.0, zn_ref[0, 0:1, 0:3 * RW_W])
    ridx = lax.broadcasted_iota(jnp.int32, (ROW_TILE, 1), 0)
    x_prev = jnp.where(ridx == 0, prev_row, pltpu.roll(x, 1, 0))
    x_next = jnp.where(ridx == ROW_TILE - 1, next_row, pltpu.roll(x, ROW_TILE - 1, 0))
    xc = x_prev * cw_ref[0:1, :] + x * cw_ref[1:2, :] + x_next * cw_ref[2:3, :]
    r = xc[:, 0:RW_W]
    k = xc[:, RW_W:2 * RW_W]
    v = xc[:, 2 * RW_W:3 * RW_W]
    gm = gm_ref[...]
    kk = k * vec_ref[0:1, :]
    kk = kk / jnp.maximum(jnp.sqrt(_group_sum(kk * kk, gm)), 1e-12)
    k_a = vec_ref[1:2, :]
    r_k = vec_ref[2:3, :]
    w_raw = _dot_hi(jnp.tanh(z[:, 3 * RW_W:3 * RW_W + LANES]), w2_ref[...])
    a_raw = _dot_hi(z[:, 3 * RW_W + LANES:3 * RW_W + 2 * LANES], a2_ref[...])
    g_ref[0] = _dot_hi(jax.nn.sigmoid(z[:, 3 * RW_W + 2 * LANES:Z_RW]), g2_ref[...])
    sh_ref[0, :, 0:RW_W] = r
    sh_ref[0, :, RW_W:2 * RW_W] = v
    sh_ref[0, :, 2 * RW_W:3 * RW_W] = kk
    rk_sum = None
    for d, d_ref in enumerate((df_ref, db_ref)):
        wr = w_raw[:, d * RW_W:(d + 1) * RW_W] + vec_ref[3 + d:4 + d, :]
        logw = -jnp.exp(-_softplus(-wr) - 0.5)
        a = jax.nn.sigmoid(a_raw[:, d * RW_W:(d + 1) * RW_W] + vec_ref[5 + d:6 + d, :])
        k_mod = k * (1.0 + (a - 1.0) * k_a)
        d_ref[0, :, 0:RW_W] = logw
        d_ref[0, :, RW_W:2 * RW_W] = kk * a
        d_ref[0, :, 2 * RW_W:3 * RW_W] = k_mod
        s = _group_sum(r * k_mod * r_k, gm)
        rk_sum = s if d == 0 else rk_sum + s
    bonus_ref[0] = rk_sum * v


def _rw_prep(zr, cw, w2p, a2p, g2, vec, n_ctx_tiles):
    bsz, lt, _ = zr.shape
    nt = lt // ROW_TILE
    hb = ROW_TILE // 8
    row = lambda b_, i: (b_, i, 0)
    full = lambda shape: pl.BlockSpec(shape, lambda b_, i: tuple(0 for _ in shape))
    o3 = jax.ShapeDtypeStruct((bsz, lt, 3 * RW_W), F32)
    o1 = jax.ShapeDtypeStruct((bsz, lt, RW_W), F32)
    return pl.pallas_call(
        functools.partial(_rwprep_kernel, n_ctx_tiles, nt),
        out_shape=(o3, o3, o3, o1, o1),
        grid=(bsz, nt),
        in_specs=[pl.BlockSpec((1, ROW_TILE, Z_RW), row),
                  pl.BlockSpec((1, 8, Z_RW), lambda b_, i: (b_, jnp.maximum(i * hb - 1, 0), 0)),
                  pl.BlockSpec((1, 8, Z_RW), lambda b_, i: (b_, jnp.minimum((i + 1) * hb, nt * hb - 1), 0)),
                  full((3, 3 * RW_W)), full((LANES, 2 * RW_W)), full((LANES, 2 * RW_W)),
                  full((RW_GATE_RANK, RW_W)), full((8, RW_W)), full((RW_W, RW_W))],
        out_specs=(pl.BlockSpec((1, ROW_TILE, 3 * RW_W), row),
                   pl.BlockSpec((1, ROW_TILE, 3 * RW_W), row),
                   pl.BlockSpec((1, ROW_TILE, 3 * RW_W), row),
                   pl.BlockSpec((1, ROW_TILE, RW_W), row),
                   pl.BlockSpec((1, ROW_TILE, RW_W), row)),
        compiler_params=_cparams(("arbitrary", "arbitrary")),
        name="rwkv_prep",
    )(zr, zr, zr, cw, w2p, a2p, g2, vec, _group_matrix(RW_W, RW_D))


def _block2(w):
    r, n = w.shape[1:]
    z = jnp.zeros((r, n), w.dtype)
    return jnp.concatenate([jnp.concatenate([w[0], z], 1), jnp.concatenate([z, w[1]], 1)], 0)


def _bmm(a, b):
    return lax.dot_general(a, b, (((2,), (1,)), ((0,), (0,))), preferred_element_type=F32)


def _bmm_nt(a, b):
    return lax.dot_general(a, b, (((2,), (2,)), ((0,), (0,))), preferred_element_type=F32)


def _bmm_tn(a, b):
    return lax.dot_general(a, b, (((1,), (1,)), ((0,), (0,))), preferred_element_type=F32)


def _rw_operands(sh, dd, reverse):
    c = CHUNK
    r, v, kk = sh[:, 0:RW_W], sh[:, RW_W:2 * RW_W], sh[:, 2 * RW_W:3 * RW_W]
    logw, beta, k = dd[:, 0:RW_W], dd[:, RW_W:2 * RW_W], dd[:, 2 * RW_W:3 * RW_W]
    b = _cumsum_rows(logw, reverse)
    last = 0 if reverse else c - 1
    e_b = jnp.exp(b)
    e_nb = jnp.exp(-b)
    e_last = jnp.exp(b[last:last + 1, :])
    e_tot = e_last * e_nb
    abar = (kk * jnp.exp(b - logw)).astype(BF16)
    rbar = (r * e_b).astype(BF16)
    kt = (k * e_nb).astype(BF16)
    bt = (beta * e_nb).astype(BF16)
    khat = (k * e_tot).astype(BF16)
    nbhat = (-(beta * e_tot)).astype(BF16)
    vb = v.astype(BF16)
    head_a = lax.broadcasted_iota(jnp.int32, (1, LANES), 1) < RW_D
    zero = jnp.zeros((c, LANES), BF16)

    def stack(*xs):
        rows = []
        for x in xs:
            rows += [jnp.where(head_a, x, zero), jnp.where(head_a, zero, x)]
        return jnp.concatenate(rows, axis=0)

    out = []
    for p in range(RW_HEADS // 2):
        sl = slice(p * LANES, (p + 1) * LANES)
        out.append(dict(xar=stack(abar[:, sl], rbar[:, sl]), yb=stack(bt[:, sl]), yk=stack(kt[:, sl]),
                        vs=stack(vb[:, sl]), kb=stack(khat[:, sl], nbhat[:, sl]),
                        e_col=jnp.broadcast_to(e_last[:, sl], (LANES, LANES)).T))
    return out


def _rw_kernel(sf_ref, df_ref, sb_ref, db_ref, yf_ref, yb_ref, h_ref):
    @pl.when(pl.program_id(1) == 0)
    def _():
        h_ref[...] = jnp.zeros_like(h_ref)

    c = CHUNK
    c2 = 2 * c
    npair = RW_HEADS // 2
    nb = sf_ref.shape[0]
    ops = []
    for i in range(nb):
        ops += _rw_operands(sf_ref[i], df_ref[i], False) + _rw_operands(sb_ref[i], db_ref[i], True)
    cat = lambda name: jnp.stack([o[name] for o in ops], axis=0)
    xar, yb_, yk, vs, kb, e_col = (cat(n) for n in ("xar", "yb", "yk", "vs", "kb", "e_col"))

    ti = lax.broadcasted_iota(jnp.int32, (c2, c2), 0)
    si = lax.broadcasted_iota(jnp.int32, (c2, c2), 1)
    same_head = (ti // c) == (si // c)
    both = lambda fwd, bwd: jnp.concatenate([jnp.broadcast_to(fwd[None], (npair, c2, c2)),
                                             jnp.broadcast_to(bwd[None], (npair, c2, c2))] * nb, axis=0)
    strict = both(same_head & (si < ti), same_head & (si > ti))
    incl = both(same_head & (si <= ti), same_head & (si >= ti))
    eye = jnp.where(ti == si, 1.0, 0.0).astype(F32)

    gb = _bmm_nt(xar, yb_)
    gk = _bmm_nt(xar, yk)
    l_ab = jnp.where(strict, gb[:, 0:c2], 0.0)
    l_rb = jnp.where(incl, gb[:, c2:2 * c2], 0.0).astype(BF16)
    l_ak = jnp.where(strict, gk[:, 0:c2], 0.0)
    l_rk = jnp.where(incl, gk[:, c2:2 * c2], 0.0)
    t_inv = None
    s = 1
    while s < c:
        same = (ti // (2 * s)) == (si // (2 * s))
        lo, hi = (ti // s) % 2, (si // s) % 2
        off = both(same & (lo == 1) & (hi == 0), same & (lo == 0) & (hi == 1))
        l_off = jnp.where(off, l_ab, 0.0)
        if t_inv is None:
            t_inv = eye[None] - l_off
        else:
            tb = t_inv.astype(BF16)
            t_inv = t_inv - _bmm(tb, _bmm(l_off.astype(BF16), tb).astype(BF16))
        s *= 2
    h0 = h_ref[...]
    xh = _bmm(xar, h0.astype(BF16))
    lv = _bmm(jnp.concatenate([l_ak, l_rk], axis=1).astype(BF16), vs)
    ub = _bmm(t_inv.astype(BF16), (xh[:, 0:c2] + lv[:, 0:c2]).astype(BF16)).astype(BF16)
    y2 = xh[:, c2:2 * c2] + lv[:, c2:2 * c2] - _bmm(l_rb, ub)
    y = y2[:, 0:c] + y2[:, c:c2]
    h_ref[...] = e_col * h0 + _bmm_tn(kb, jnp.concatenate([vs, ub], axis=1))
    for i in range(nb):
        o = 2 * npair * i
        yf_ref[i] = jnp.concatenate([y[o + p] for p in range(npair)], axis=1)
        yb_ref[i] = jnp.concatenate([y[o + npair + p] for p in range(npair)], axis=1)


def _rw_scan(shared, dfw, dbw, n_ctx_chunks):
    bsz, lt, _ = shared.shape
    nc = lt // CHUNK
    nb = SCAN_BATCH if bsz % SCAN_BATCH == 0 else 1
    fwd = lambda b_, s: (b_, s, 0)
    bwd = lambda b_, s: (b_, _bwd_chunk(s, n_ctx_chunks, nc), 0)
    return pl.pallas_call(
        _rw_kernel,
        out_shape=(jax.ShapeDtypeStruct((bsz, lt, RW_W), F32),
                   jax.ShapeDtypeStruct((bsz, lt, RW_W), F32)),
        grid=(bsz // nb, nc),
        in_specs=[pl.BlockSpec((nb, CHUNK, 3 * RW_W), fwd),
                  pl.BlockSpec((nb, CHUNK, 3 * RW_W), fwd),
                  pl.BlockSpec((nb, CHUNK, 3 * RW_W), bwd),
                  pl.BlockSpec((nb, CHUNK, 3 * RW_W), bwd)],
        out_specs=(pl.BlockSpec((nb, CHUNK, RW_W), fwd),
                   pl.BlockSpec((nb, CHUNK, RW_W), bwd)),
        scratch_shapes=[pltpu.VMEM((nb * RW_HEADS, LANES, LANES), F32)],
        compiler_params=_cparams(("arbitrary", "arbitrary")),
        name="rwkv_scan",
    )(shared, dfw, shared, dbw)


def _rank_kernel(n, ti_ref, dest_ref, meta_ref, cnt_ref, run_ref, start_ref):
    ph = pl.program_id(0)
    i = pl.program_id(1)
    ti = ti_ref[...]
    lane = lax.broadcasted_iota(jnp.int32, (ROW_TILE, LANES), 1)
    ohs = [jnp.where(ti[:, j:j + 1] == lane, 1.0, 0.0).astype(F32) for j in range(TOP_K)]
    oh = ohs[0] + ohs[1] + ohs[2] + ohs[3]
    tile_cnt = jnp.sum(oh, axis=0, keepdims=True)

    @pl.when((ph == 0) & (i == 0))
    def _():
        cnt_ref[...] = jnp.zeros_like(cnt_ref)
        run_ref[...] = jnp.zeros_like(run_ref)

    @pl.when(ph == 0)
    def _():
        cnt_ref[...] += tile_cnt

    @pl.when((ph == 0) & (i == n - 1))
    def _():
        cnt = cnt_ref[...]
        shift = MOE_BLOCK.bit_length() - 1
        padded = jnp.left_shift(jnp.right_shift(cnt.astype(jnp.int32) + (MOE_BLOCK - 1), shift),
                                shift).astype(F32)
        e0 = lax.broadcasted_iota(jnp.int32, (LANES, LANES), 0)
        e1 = lax.broadcasted_iota(jnp.int32, (LANES, LANES), 1)
        before = jnp.where(e0 < e1, 1.0, 0.0).astype(F32)
        start = _dot_hi(jnp.broadcast_to(padded, (8, LANES)), before)[0:1]
        start_ref[...] = start
        meta_ref[0:1, :] = cnt
        meta_ref[1:2, :] = start
        meta_ref[2:3, :] = padded
        meta_ref[3:8, :] = jnp.zeros((5, LANES), F32)

    @pl.when(ph == 1)
    def _():
        t0 = lax.broadcasted_iota(jnp.int32, (ROW_TILE, ROW_TILE), 0)
        t1 = lax.broadcasted_iota(jnp.int32, (ROW_TILE, ROW_TILE), 1)
        earlier = jnp.where(t1 < t0, 1.0, 0.0).astype(BF16)
        pos = (jnp.dot(earlier, oh.astype(BF16), preferred_element_type=F32)
               + run_ref[...] + start_ref[...])
        dest = jnp.zeros((ROW_TILE, LANES), F32)
        for j in range(TOP_K):
            dj = jnp.sum(ohs[j] * pos, axis=-1, keepdims=True)
            dest = jnp.where(lane == j, dj, dest)
        dest_ref[0] = dest.T[0:8, :].astype(jnp.int32)
        run_ref[...] += tile_cnt


def _moe_rank(ti):
    n = ti.shape[0]
    nt = n // ROW_TILE
    return pl.pallas_call(
        functools.partial(_rank_kernel, nt),
        out_shape=(jax.ShapeDtypeStruct((nt, 8, ROW_TILE), jnp.int32),
                   jax.ShapeDtypeStruct((8, LANES), F32)),
        grid=(2, nt),
        in_specs=[pl.BlockSpec((ROW_TILE, LANES), lambda p, i: (i, 0))],
        out_specs=(pl.BlockSpec((1, 8, ROW_TILE), lambda p, i: (i * p, 0, 0)),
                   pl.BlockSpec((8, LANES), lambda p, i: (0, 0))),
        scratch_shapes=[pltpu.VMEM((1, LANES), F32), pltpu.VMEM((1, LANES), F32),
                        pltpu.VMEM((1, LANES), F32)],
        compiler_params=_cparams(("arbitrary", "arbitrary")),
        name="moe_rank",
    )(ti)


MOE_TOK_BITS = 15


def _invert_kernel(n_blocks, plane_rows, dest_hbm, code_ref, idx_s, sem):
    s = pl.program_id(0)

    @pl.when(s < n_blocks)
    def _():
        def fill(r, c):
            code_ref[s * MOE_BLOCK + r] = jnp.left_shift(TOP_K * plane_rows + r, MOE_TOK_BITS)
            return c

        lax.fori_loop(0, MOE_BLOCK, fill, 0, unroll=8)

    @pl.when(s >= n_blocks)
    def _():
        i = s - n_blocks
        cp = pltpu.make_async_copy(dest_hbm.at[i], idx_s, sem)
        cp.start()
        cp.wait()

        def tok(t, c):
            token = i * ROW_TILE + t
            for j in range(TOP_K):
                code_ref[idx_s[0, j * ROW_TILE + t]] = jnp.bitwise_or(
                    jnp.left_shift(j * plane_rows + token, MOE_TOK_BITS), token)
            return c

        lax.fori_loop(0, ROW_TILE, tok, 0, unroll=4)


def _moe_invert(dest2d, n_blocks, plane_rows):
    nt = dest2d.shape[0]
    assert nt * ROW_TILE < 2 ** MOE_TOK_BITS and TOP_K * plane_rows + MOE_BLOCK <= 2 ** (32 - MOE_TOK_BITS)
    return pl.pallas_call(
        functools.partial(_invert_kernel, n_blocks, plane_rows),
        out_shape=jax.ShapeDtypeStruct((n_blocks * MOE_BLOCK,), jnp.int32),
        grid=(n_blocks + nt,),
        in_specs=[pl.BlockSpec(memory_space=pl.ANY)],
        out_specs=pl.BlockSpec(memory_space=pltpu.SMEM),
        scratch_shapes=[pltpu.SMEM((1, 8 * ROW_TILE), jnp.int32), pltpu.SemaphoreType.DMA(())],
        compiler_params=_cparams(("arbitrary",)),
        name="moe_invert",
    )(dest2d.reshape(nt, 1, 8 * ROW_TILE))


def _fused_expert_kernel(plane_rows,
                         be_ref, nu_ref, code_hbm, m_hbm, wg_ref, bg_ref, wu_ref, bu_ref, wd_ref, bd_ref,
                         out_hbm, wg_s, wu_s, wd_s, xbuf0, xbuf1, ybuf0, ybuf1, idx_s, sem_i, sem_g, sem_s):
    b = pl.program_id(0)
    nu = nu_ref[0]
    blk = MOE_BLOCK
    xbufs = (xbuf0, xbuf1)
    ybufs = (ybuf0, ybuf1)

    def idx_copy(block, slot):
        return pltpu.make_async_copy(code_hbm.at[block], idx_s.at[slot], sem_i.at[slot])

    def gather(slot, r, buf):
        tok = jnp.bitwise_and(idx_s[slot, 0, r], 2 ** MOE_TOK_BITS - 1)
        return pltpu.make_async_copy(m_hbm.at[tok], xbufs[buf].at[pl.ds(r * SUBLANES, SUBLANES), :], sem_g.at[buf])

    def scatter(slot, r, buf):
        row = lax.shift_right_logical(idx_s[slot, 0, r], MOE_TOK_BITS)
        return pltpu.make_async_copy(ybufs[buf].at[pl.ds(r * SUBLANES, SUBLANES), :], out_hbm.at[row], sem_s.at[buf])

    def wait_rows(buf, sem):
        pltpu.make_async_copy(m_hbm.at[pl.ds(0, blk)], xbufs[buf].reshape(blk, SUBLANES, LANES), sem.at[buf]).wait()

    def step(cur):
        oth = 1 - cur
        s_prev, s_cur, s_next = (b + 2) % 3, b % 3, (b + 1) % 3
        nxt = jnp.minimum(b + 1, nu - 1)
        idx_copy(nxt, s_next).start()
        wait_rows(cur, sem_g)
        idx_copy(nxt, s_next).wait()
        for r in range(blk):
            gather(s_next, r, oth).start()
            scatter(s_prev, r, oth).start()
        nch = wg_s.shape[0] // LANES
        x = jnp.concatenate([xbufs[cur][pl.ds(c, blk, stride=SUBLANES), :] for c in range(nch)],
                            axis=1).astype(BF16)
        gt = jnp.minimum(jnp.dot(x, wg_s[...], preferred_element_type=F32) + bg_ref[0, 0], SWIGLU_LIMIT)
        up = jnp.clip(jnp.dot(x, wu_s[...], preferred_element_type=F32) + bu_ref[0, 0],
                      -SWIGLU_LIMIT, SWIGLU_LIMIT)
        act = (up + 1.0) * gt * jax.nn.sigmoid(SWIGLU_ALPHA * gt)
        y = jnp.dot(act.astype(BF16), wd_s[...], preferred_element_type=F32) + bd_ref[0, 0]
        for c in range(nch):
            ybufs[cur][pl.ds(c, blk, stride=SUBLANES), :] = y[:, c * LANES:(c + 1) * LANES]
        wait_rows(oth, sem_s)

        @pl.when(b == nu - 1)
        def _():
            wait_rows(oth, sem_g)

            def last(r, c):
                scatter(s_cur, r, cur).start()
                return c

            lax.fori_loop(0, blk, last, 0, unroll=8)
            wait_rows(cur, sem_s)

    @pl.when(b < nu)
    def _():
        e = be_ref[b]
        changed = (b == 0) | (e != be_ref[jnp.maximum(b - 1, 0)])

        @pl.when(changed)
        def _():
            wg_s[...] = wg_ref[0, 0].astype(BF16)
            wu_s[...] = wu_ref[0, 0].astype(BF16)
            wd_s[...] = wd_ref[0, 0].astype(BF16)

        @pl.when(b == 0)
        def _():
            ybuf0[...] = jnp.zeros_like(ybuf0)
            ybuf1[...] = jnp.zeros_like(ybuf1)
            cp = idx_copy(0, 0)
            cp.start()
            cp.wait()

            def spare(r, c):
                idx_s[2, 0, r] = jnp.left_shift(TOP_K * plane_rows + r, MOE_TOK_BITS)
                return c

            lax.fori_loop(0, blk, spare, 0, unroll=8)

            def first(r, c):
                gather(0, r, 0).start()
                return c

            lax.fori_loop(0, blk, first, 0, unroll=8)

        @pl.when(b % 2 == 0)
        def _():
            step(0)

        @pl.when(b % 2 == 1)
        def _():
            step(1)


def _moe_experts_fused(layer, block_expert, n_used, codes, m, plane_rows, wg, bg, wu, bu, wd, bd):
    nch = m.shape[1]
    d = nch * LANES
    nl, ne, _, f = wg.shape
    nb = codes.shape[0]
    wmap = lambda b, be, nu: (layer, be[jnp.maximum(jnp.minimum(b, nu[0] - 1), 0)], 0, 0)
    any_spec = pl.BlockSpec(memory_space=pl.ANY)
    return pl.pallas_call(
        functools.partial(_fused_expert_kernel, plane_rows),
        out_shape=jax.ShapeDtypeStruct((TOP_K * plane_rows + MOE_BLOCK, nch, LANES), F32),
        grid_spec=pltpu.PrefetchScalarGridSpec(
            num_scalar_prefetch=2,
            grid=(nb,),
            in_specs=[any_spec, any_spec,
                      pl.BlockSpec((1, 1, d, f), wmap), pl.BlockSpec((1, 1, 1, f), wmap),
                      pl.BlockSpec((1, 1, d, f), wmap), pl.BlockSpec((1, 1, 1, f), wmap),
                      pl.BlockSpec((1, 1, f, d), wmap), pl.BlockSpec((1, 1, 1, d), wmap)],
            out_specs=any_spec,
            scratch_shapes=[pltpu.VMEM((d, f), BF16), pltpu.VMEM((d, f), BF16), pltpu.VMEM((f, d), BF16),
                            pltpu.VMEM((MOE_BLOCK * nch, LANES), F32), pltpu.VMEM((MOE_BLOCK * nch, LANES), F32),
                            pltpu.VMEM((MOE_BLOCK * nch, LANES), F32), pltpu.VMEM((MOE_BLOCK * nch, LANES), F32),
                            pltpu.SMEM((3, 1, MOE_BLOCK), jnp.int32),
                            pltpu.SemaphoreType.DMA((3,)), pltpu.SemaphoreType.DMA((2,)),
                            pltpu.SemaphoreType.DMA((2,))]),
        compiler_params=_cparams(("arbitrary",)),
        name="moe_experts",
    )(block_expert, n_used, codes.reshape(nb, 1, MOE_BLOCK), m, wg, bg.reshape(nl, ne, 1, f),
      wu, bu.reshape(nl, ne, 1, f), wd, bd.reshape(nl, ne, 1, d))


def _combine2_kernel(alpha, gt_ref, h_ref, mod_ref, vec_ref, y0_ref, y1_ref, y2_ref, y3_ref, o_ref):
    gt = gt_ref[0]
    d = h_ref.shape[2]
    xs = []
    for c in range(d // LANES):
        rows = pl.ds(c, ROW_TILE, stride=SUBLANES)
        f = (gt[:, 0:1] * y0_ref[rows, :] + gt[:, 1:2] * y1_ref[rows, :]
             + gt[:, 2:3] * y2_ref[rows, :] + gt[:, 3:4] * y3_ref[rows, :])
        cols = slice(c * LANES, (c + 1) * LANES)
        xs.append(alpha * h_ref[0, :, cols] + mod_ref[0, 0, 0:1, cols] * f)
    mu = sum(jnp.sum(x, axis=-1, keepdims=True) for x in xs) * (1.0 / d)
    xs = [x - mu for x in xs]
    var = sum(jnp.sum(x * x, axis=-1, keepdims=True) for x in xs) * (1.0 / d)
    inv = lax.rsqrt(var + LN_EPS)
    for c, x in enumerate(xs):
        cols = slice(c * LANES, (c + 1) * LANES)
        o_ref[0, :, cols] = x * inv * vec_ref[0:1, cols] + vec_ref[1:2, cols]


def _moe_combine2(alpha, n_ctx_tiles, tile0, plane_rows, gates, h1, mod3, vec, out4):
    bsz, rows, d = h1.shape
    nch = d // LANES
    nt = rows // ROW_TILE
    pt = plane_rows // ROW_TILE
    row = lambda b_, i: (b_, i, 0)
    plane = lambda j: pl.BlockSpec((ROW_TILE * nch, LANES), lambda b_, i: (j * pt + b_ * nt + i, 0))
    out4 = out4.reshape((TOP_K * plane_rows + MOE_BLOCK) * nch, LANES)
    return pl.pallas_call(
        functools.partial(_combine2_kernel, alpha),
        out_shape=jax.ShapeDtypeStruct((bsz, rows, d), F32),
        grid=(bsz, nt),
        in_specs=[pl.BlockSpec((1, ROW_TILE, LANES), row),
                  pl.BlockSpec((1, ROW_TILE, d), row),
                  pl.BlockSpec((1, 1, 1, d), lambda b_, i: (b_, jnp.where(tile0 + i < n_ctx_tiles, 0, 1), 0, 0)),
                  pl.BlockSpec((2, d), lambda b_, i: (0, 0)),
                  plane(0), plane(1), plane(2), plane(3)],
        out_specs=pl.BlockSpec((1, ROW_TILE, d), row),
        compiler_params=_cparams(("arbitrary", "arbitrary")),
        name="moe_combine",
    )(gates, h1, mod3, vec, out4, out4, out4, out4)


def _moe(layer, alpha, n_ctx_tiles, tile0, h1, m, ti, gates, mod3, ln2, wg, bg, wu, bu, wd, bd):
    bsz, rows, d = h1.shape
    n = bsz * rows
    dest, meta = _moe_rank(ti.reshape(n, LANES))
    dest2d = dest.reshape(n // ROW_TILE, 8 * ROW_TILE)
    n_blocks = -(-(n * TOP_K) // MOE_BLOCK) + N_EXPERTS
    pad_end = (meta[1, :N_EXPERTS] + meta[2, :N_EXPERTS]).astype(jnp.int32)
    block_row = jnp.arange(n_blocks, dtype=jnp.int32) * MOE_BLOCK
    block_expert = jnp.minimum(jnp.sum((pad_end[None, :] <= block_row[:, None]).astype(jnp.int32), axis=1),
                               N_EXPERTS - 1)
    n_used = (pad_end[-1:] // MOE_BLOCK).astype(jnp.int32)
    plane_rows = n
    codes = _moe_invert(dest2d, n_blocks, plane_rows).reshape(n_blocks, MOE_BLOCK)
    out4 = _moe_experts_fused(layer, block_expert, n_used, codes, m.reshape(n, d // LANES, LANES), plane_rows,
                              wg, bg, wu, bu, wd, bd)
    return _moe_combine2(alpha, n_ctx_tiles, tile0, plane_rows, gates, h1, mod3, ln2, out4)


def kernel(x, c, ctx, c_ctx, ln_in_g, ln_in_b, ada_w, ada_b, w_in, lam_q1, lam_k1, lam_q2, lam_k2,
           da_subln_g, gla_gate_w2, gla_gate_b, gla_norm_g, rw_conv_w, rw_w2, rw_w0, rw_a2, rw_a0,
           rw_g2, rw_k_k, rw_k_a, rw_r_k, rw_lnx_g, rw_lnx_b, w_out, ln1_g, ln1_b, router_w, router_b,
           moe_w_gate, moe_b_gate, moe_w_up, moe_b_up, moe_w_down, moe_b_down, ln2_g, ln2_b):
    bsz, seq, d = x.shape
    n_ctx = ctx.shape[1]
    depth = w_in.shape[0]
    assert n_ctx % ROW_TILE == 0 and seq % ROW_TILE == 0 and seq % GRID_W == 0
    assert w_in.shape[2] == 3488 and bsz + 1 <= 8
    nct = n_ctx // ROW_TILE
    ncc = n_ctx // CHUNK
    nt = (n_ctx + seq) // ROW_TILE
    alpha = (2 * depth) ** 0.25

    c_all = jnp.concatenate([c, c_ctx[None], jnp.zeros((8 - bsz - 1, d), F32)], axis=0)
    mods = _ada_mod(c_all, ada_w, ada_b).reshape(depth, 8, 6, d)
    h = _ln_in(ctx, x, ln_in_g, ln_in_b)
    cos_t, sin_t = _rope_tables(n_ctx, seq)

    def pick(l, idx):
        mc = jnp.broadcast_to(mods[l, bsz][None, idx], (bsz, len(idx), d))
        return jnp.stack([mc, mods[l, :bsz][:, idx]], axis=1)

    for l in range(depth):
        last = l == depth - 1
        tile0 = nct if last else 0
        n_out_tiles = nt - tile0
        lam_init = 0.8 - 0.6 * math.exp(-0.3 * l)
        lam = (jnp.exp(jnp.sum(lam_q1[l] * lam_k1[l])) - jnp.exp(jnp.sum(lam_q2[l] * lam_k2[l])) + lam_init)

        qk, v, zg, zr = _inproj(h, pick(l, [1, 0]), _pack_w_in(w_in[l]), cos_t, sin_t, nct)
        ao = _attention(qk, v, lam, nct, nt - nct, n_ctx + seq)
        if not last:
            ao = jnp.concatenate([_attention(qk, v, lam, 0, nct, n_ctx), ao], axis=1)
        w2p, biasp = _pack_gla_gate(gla_gate_w2[l], gla_gate_b[l])
        gof, gob = _gla_scan(zg, w2p, biasp, ncc)
        rvec = jnp.stack([rw_k_k[l], rw_k_a[l], rw_r_k[l].reshape(-1), rw_w0[l, 0], rw_w0[l, 1],
                          rw_a0[l, 0], rw_a0[l, 1], jnp.zeros((RW_W,), F32)], axis=0)
        shared, dfw, dbw, rg, bonus = _rw_prep(zr, rw_conv_w[l], _block2(rw_w2[l]), _block2(rw_a2[l]),
                                               rw_g2[l], rvec, nct)
        yf, yb = _rw_scan(shared, dfw, dbw, ncc)

        nrm = jnp.stack([jnp.pad(jnp.tile(da_subln_g[l], DA_HEADS), (0, GLA_W - DA_W)),
                         jnp.tile(gla_norm_g[l], GLA_HEADS), rw_lnx_g[l], rw_lnx_b[l]], axis=0)
        rw_p = jnp.pad(router_w[l], ((0, 0), (0, LANES - N_EXPERTS)))
        rb_p = jnp.pad(router_b[l], (0, LANES - N_EXPERTS), constant_values=-1e30).reshape(1, LANES)
        h1, m, ti, gates = _mixout(alpha, 1.0 - lam_init, tile0, n_out_tiles, nct,
                                   ao, gof, gob, zg, yf, yb, bonus, rg, h, pick(l, [2, 4, 3]),
                                   w_out[l].astype(BF16), jnp.stack([ln1_g[l], ln1_b[l]], 0), nrm, rw_p, rb_p)
        h = _moe(l, alpha, nct, tile0, h1, m, ti, gates, pick(l, [5]), jnp.stack([ln2_g[l], ln2_b[l]], 0),
                 moe_w_gate, moe_b_gate, moe_w_up, moe_b_up, moe_w_down, moe_b_down)
    return h
```

```python
import functools
import math

import jax
import jax.numpy as jnp
import numpy as np
from jax import lax
from jax.experimental import pallas as pl
from jax.experimental.pallas import tpu as pltpu

F32 = jnp.float32
BF16 = jnp.bfloat16
HI = lax.Precision.HIGHEST

GRID_W = 64
DA_HEADS, DA_QK, DA_V = 4, 32, 64
GLA_HEADS, GLA_DK, GLA_DV, GLA_RANK, GLA_TAU = 6, 32, 64, 16, 16.0
RW_HEADS, RW_D, RW_DECAY_RANK, RW_A_RANK, RW_GATE_RANK = 6, 64, 64, 64, 128
RW_GN_EPS = 64e-5
N_EXPERTS, TOP_K = 32, 4
SWIGLU_LIMIT, SWIGLU_ALPHA = 7.0, 1.702
ROPE_BASE = 10000.0
LN_EPS = 1e-5

DA_W = DA_HEADS * DA_V
GLA_KW = GLA_HEADS * GLA_DK
GLA_W = GLA_HEADS * GLA_DV
RW_W = RW_HEADS * RW_D
MIX_W = DA_W + GLA_W + RW_W

LANES = 128
ROW_TILE = 256
CHUNK = 64
SCAN_BATCH = 4
MOE_BLOCK = 256
VMEM_LIMIT = 56 * 1024 * 1024
SUBLANES = 8

Z_ATTN = 3 * DA_W
GLA_KP = 256
Z_GLA = 2 * GLA_KP + 2 * GLA_W + LANES
Z_RW = 3 * RW_W + 3 * LANES
Z_ALL = Z_ATTN + Z_GLA + Z_RW


def _cparams(sem):
    return pltpu.CompilerParams(dimension_semantics=sem, vmem_limit_bytes=VMEM_LIMIT)


def _ln(x, g, b, eps):
    mu = jnp.mean(x, axis=-1, keepdims=True)
    xc = x - mu
    var = jnp.mean(xc * xc, axis=-1, keepdims=True)
    return xc * lax.rsqrt(var + eps) * g + b


def _silu(x):
    return x * jax.nn.sigmoid(x)


def _dot(a, b):
    return jnp.dot(a.astype(BF16), b.astype(BF16), preferred_element_type=F32)


def _dot_hi(a, b):
    return jnp.dot(a, b, precision=HI, preferred_element_type=F32)


def _group_sum(x, gmat):
    hi = x.astype(BF16)
    lo = (x - hi.astype(F32)).astype(BF16)
    return (jnp.dot(hi, gmat, preferred_element_type=F32)
            + jnp.dot(lo, gmat, preferred_element_type=F32))


def _group_matrix(width, group):
    idx = np.arange(width) // group
    return jnp.asarray((idx[:, None] == idx[None, :]).astype(np.float32), dtype=BF16)


def _ada_kernel(c_ref, w_ref, b_ref, o_ref):
    o_ref[0] = _dot_hi(_silu(c_ref[...]), w_ref[0]) + b_ref[0]


def _ada_mod(c_all, ada_w, ada_b):
    nl, d, n6 = ada_w.shape
    tn = 1536
    return pl.pallas_call(
        _ada_kernel,
        out_shape=jax.ShapeDtypeStruct((nl, c_all.shape[0], n6), F32),
        grid=(nl, n6 // tn),
        in_specs=[pl.BlockSpec((c_all.shape[0], d), lambda l, j: (0, 0)),
                  pl.BlockSpec((1, d, tn), lambda l, j: (l, 0, j)),
                  pl.BlockSpec((1, 1, tn), lambda l, j: (l, 0, j))],
        out_specs=pl.BlockSpec((1, c_all.shape[0], tn), lambda l, j: (l, 0, j)),
        compiler_params=_cparams(("arbitrary", "arbitrary")),
        name="ada_mod",
    )(c_all, ada_w, ada_b.reshape(nl, 1, n6))


def _ln_in_kernel(n_ctx_tiles, c_ref, x_ref, g_ref, b_ref, o_ref):
    @pl.when(pl.program_id(1) < n_ctx_tiles)
    def _():
        o_ref[0] = _ln(c_ref[0], g_ref[...], b_ref[...], LN_EPS)

    @pl.when(pl.program_id(1) >= n_ctx_tiles)
    def _():
        o_ref[0] = _ln(x_ref[0], g_ref[...], b_ref[...], LN_EPS)


def _ln_in(ctx, x, g, b):
    bsz, n_ctx, d = ctx.shape
    nct = n_ctx // ROW_TILE
    nt = nct + x.shape[1] // ROW_TILE
    return pl.pallas_call(
        functools.partial(_ln_in_kernel, nct),
        out_shape=jax.ShapeDtypeStruct((bsz, nt * ROW_TILE, d), F32),
        grid=(bsz, nt),
        in_specs=[pl.BlockSpec((1, ROW_TILE, d), lambda b_, i: (b_, jnp.minimum(i, nct - 1), 0)),
                  pl.BlockSpec((1, ROW_TILE, d), lambda b_, i: (b_, jnp.maximum(i - nct, 0), 0)),
                  pl.BlockSpec((1, d), lambda b_, i: (0, 0)),
                  pl.BlockSpec((1, d), lambda b_, i: (0, 0))],
        out_specs=pl.BlockSpec((1, ROW_TILE, d), lambda b_, i: (b_, i, 0)),
        compiler_params=_cparams(("arbitrary", "arbitrary")),
        name="ln_in",
    )(ctx, x, g.reshape(1, d), b.reshape(1, d))


def _inproj_kernel(h_ref, mod_ref, w_ref, cos_ref, sin_ref, qk_ref, v_ref, gla_ref, rw_ref):
    h = h_ref[0]
    sc = mod_ref[0, 0, 0:1, :]
    sh = mod_ref[0, 0, 1:2, :]
    xm = (h * (1.0 + sc) + sh).astype(BF16)
    qk = jnp.dot(xm, w_ref[:, 0:2 * DA_W], preferred_element_type=F32)
    cos = cos_ref[...]
    sin = sin_ref[...]
    lane = lax.broadcasted_iota(jnp.int32, (1, LANES), 1)
    first = (lane % 16) < 8
    qscale = DA_QK ** -0.5
    for j in range(4):
        x = qk[:, j * LANES:(j + 1) * LANES]
        rot = jnp.where(first, pltpu.roll(x, LANES - 8, 1), pltpu.roll(x, 8, 1))
        y = x * cos + rot * sin
        if j < 2:
            y = y * qscale
        qk_ref[0, :, j * LANES:(j + 1) * LANES] = y.astype(BF16)
    v_ref[0] = jnp.dot(xm, w_ref[:, 2 * DA_W:Z_ATTN], preferred_element_type=F32).astype(BF16)
    gla_ref[0] = jnp.dot(xm, w_ref[:, Z_ATTN:Z_ATTN + Z_GLA], preferred_element_type=F32)
    rw_ref[0] = jnp.dot(xm, w_ref[:, Z_ATTN + Z_GLA:Z_ALL], preferred_element_type=F32)


def _inproj(h, mod1, w_p, cos_t, sin_t, n_ctx_tiles):
    bsz, lt, d = h.shape
    nt = lt // ROW_TILE
    row = lambda b_, i: (b_, i, 0)
    return pl.pallas_call(
        _inproj_kernel,
        out_shape=(jax.ShapeDtypeStruct((bsz, lt, 2 * DA_W), BF16),
                   jax.ShapeDtypeStruct((bsz, lt, DA_W), BF16),
                   jax.ShapeDtypeStruct((bsz, lt, Z_GLA), F32),
                   jax.ShapeDtypeStruct((bsz, lt, Z_RW), F32)),
        grid=(bsz, nt),
        in_specs=[pl.BlockSpec((1, ROW_TILE, d), row),
                  pl.BlockSpec((1, 1, 2, d),
                               lambda b_, i: (b_, jnp.where(i < n_ctx_tiles, 0, 1), 0, 0)),
                  pl.BlockSpec((d, Z_ALL), lambda b_, i: (0, 0)),
                  pl.BlockSpec((ROW_TILE, LANES), lambda b_, i: (i, 0)),
                  pl.BlockSpec((ROW_TILE, LANES), lambda b_, i: (i, 0))],
        out_specs=(pl.BlockSpec((1, ROW_TILE, 2 * DA_W), row),
                   pl.BlockSpec((1, ROW_TILE, DA_W), row),
                   pl.BlockSpec((1, ROW_TILE, Z_GLA), row),
                   pl.BlockSpec((1, ROW_TILE, Z_RW), row)),
        compiler_params=_cparams(("arbitrary", "arbitrary")),
        name="inproj",
    )(h, mod1, w_p, cos_t, sin_t)


def _pack_w_in(w):
    d = w.shape[0]
    o = 0
    parts = {}
    for name, n in (("da_q", 256), ("da_k", 256), ("da_v", 256), ("gla_q", GLA_KW), ("gla_k", GLA_KW),
                    ("gla_v", GLA_W), ("gla_gf", GLA_RANK), ("gla_gb", GLA_RANK), ("gla_r", GLA_W),
                    ("rw_rkv", 3 * RW_W), ("rw_wf", 64), ("rw_wb", 64), ("rw_af", 64), ("rw_ab", 64),
                    ("rw_g", 128)):
        parts[name] = w[:, o:o + n]
        o += n
    z = lambda n: jnp.zeros((d, n), w.dtype)
    cols = [parts["da_q"], parts["da_k"], parts["da_v"],
            parts["gla_q"], z(GLA_KP - GLA_KW), parts["gla_k"], z(GLA_KP - GLA_KW),
            parts["gla_v"], parts["gla_r"], parts["gla_gf"], parts["gla_gb"], z(LANES - 2 * GLA_RANK),
            parts["rw_rkv"], parts["rw_wf"], parts["rw_wb"], parts["rw_af"], parts["rw_ab"], parts["rw_g"]]
    return jnp.concatenate(cols, axis=1).astype(BF16)


def _rope_tables(n_ctx, seq):
    t = np.arange(seq)
    row = (t // GRID_W).astype(np.float32)
    col = (t % GRID_W).astype(np.float32)
    quarter = DA_QK // 4
    inv = (ROPE_BASE ** (-np.arange(quarter, dtype=np.float32) / quarter)).astype(np.float32)
    ang_r = row[:, None] * inv
    ang_c = col[:, None] * inv
    ang = np.concatenate([ang_r, ang_r, ang_c, ang_c], -1).astype(np.float32)
    cos = np.tile(np.cos(ang), (1, LANES // DA_QK))
    sin = np.tile(np.sin(ang), (1, LANES // DA_QK))
    sign = np.where((np.arange(LANES) % 16) < 8, -1.0, 1.0).astype(np.float32)
    cos = np.concatenate([np.ones((n_ctx, LANES), np.float32), cos], 0)
    sin = np.concatenate([np.zeros((n_ctx, LANES), np.float32), sin * sign], 0)
    return jnp.asarray(cos, F32), jnp.asarray(sin, F32)


ATTN_KEY_CHUNK = 1280


def _attn_kernel(lam_ref, q_ref, k_ref, v_ref, o_ref):
    q = q_ref[0]
    lam = lam_ref[0, 0]
    tq = q.shape[0]
    n_keys = k_ref.shape[1]
    lane = lax.broadcasted_iota(jnp.int32, (1, LANES), 1)
    zero = jnp.zeros_like(q)
    outs = []
    for hh in range(2):
        q2 = jnp.concatenate(
            [jnp.where((lane >= hh * 64 + m * DA_QK) & (lane < hh * 64 + (m + 1) * DA_QK), q, zero)
             for m in range(2)], axis=0)
        m_run = l_run = acc = None
        for k0 in range(0, n_keys, ATTN_KEY_CHUNK):
            k1 = min(k0 + ATTN_KEY_CHUNK, n_keys)
            s = _bdot_nt(q2, k_ref[0, k0:k1, :])
            m_blk = jnp.max(s, axis=-1, keepdims=True)
            m_new = m_blk if m_run is None else jnp.maximum(m_run, m_blk)
            p = jnp.exp((s - m_new).astype(BF16))
            v_ext = jnp.concatenate([v_ref[0, k0:k1, :], jnp.ones((k1 - k0, LANES), BF16)], axis=1)
            pv_ext = jnp.dot(p, v_ext, preferred_element_type=F32)
            pv, p_sum = pv_ext[:, 0:LANES], pv_ext[:, LANES:2 * LANES]
            if m_run is None:
                l_run, acc = p_sum, pv
            else:
                scale = jnp.exp(m_run - m_new)
                l_run = scale * l_run + p_sum
                acc = scale * acc + pv
            m_run = m_new
        o = acc / l_run
        outs.append(o[0:tq] - lam * o[tq:2 * tq])
    o_ref[0] = jnp.where(lane < 64, outs[0], outs[1])


def _attention(qk, v, lam, q_tile0, n_q_tiles, n_k_rows):
    bsz, lt, _ = qk.shape
    out = pl.pallas_call(
        _attn_kernel,
        out_shape=jax.ShapeDtypeStruct((bsz, n_q_tiles * ROW_TILE, DA_W), F32),
        grid=(bsz, 2, n_q_tiles),
        in_specs=[pl.BlockSpec(memory_space=pltpu.SMEM),
                  pl.BlockSpec((1, ROW_TILE, LANES), lambda b_, p, i: (b_, q_tile0 + i, p)),
                  pl.BlockSpec((1, n_k_rows, LANES), lambda b_, p, i: (b_, 0, 2 + p)),
                  pl.BlockSpec((1, n_k_rows, LANES), lambda b_, p, i: (b_, 0, p))],
        out_specs=pl.BlockSpec((1, ROW_TILE, LANES), lambda b_, p, i: (b_, i, p)),
        compiler_params=_cparams(("arbitrary", "arbitrary", "arbitrary")),
        name="diff_attn",
    )(lam.reshape(1, 1), qk, qk, v)
    return out


def _mixout_kernel(alpha, sub_scale,
                   ao_ref, gof_ref, gob_ref, zg_ref, yf_ref, yb_ref, bonus_ref, rg_ref, h_ref, mod_ref,
                   wo_ref, vec_ref, nrm_ref, g256_ref, g384_ref, rw_ref, rb_ref,
                   h1_ref, m_ref, ti_ref, gt_ref):
    a = ao_ref[0]
    a = a * lax.rsqrt(_group_sum(a * a, g256_ref[...]) * (1.0 / DA_V) + LN_EPS) * nrm_ref[0:1, 0:DA_W] * sub_scale
    o = gof_ref[0] + gob_ref[0]
    r = zg_ref[0, :, 2 * GLA_KP + GLA_W:2 * GLA_KP + 2 * GLA_W]
    gl = (o * lax.rsqrt(_group_sum(o * o, g384_ref[...]) * (1.0 / GLA_DV) + LN_EPS)
          * nrm_ref[1:2, :] * _silu(r))
    y = yf_ref[0] + yb_ref[0]
    mu = _group_sum(y, g384_ref[...]) * (1.0 / RW_D)
    yc = y - mu
    var = _group_sum(yc * yc, g384_ref[...]) * (1.0 / RW_D)
    yn = yc * lax.rsqrt(var + RW_GN_EPS) * nrm_ref[2:3, :] + nrm_ref[3:4, :]
    rw = (yn + bonus_ref[0]) * rg_ref[0]
    mix = (_dot(a, wo_ref[0:DA_W, :]) + _dot(gl, wo_ref[DA_W:DA_W + GLA_W, :])
           + _dot(rw, wo_ref[DA_W + GLA_W:MIX_W, :]))
    g1 = mod_ref[0, 0, 0:1, :]
    sc2 = mod_ref[0, 0, 1:2, :]
    sh2 = mod_ref[0, 0, 2:3, :]
    h1 = _ln(alpha * h_ref[0] + g1 * mix, vec_ref[0:1, :], vec_ref[1:2, :], LN_EPS)
    h1_ref[0] = h1
    m = h1 * (1.0 + sc2) + sh2
    for c in range(m.shape[1] // LANES):
        m_ref[0, pl.ds(c, ROW_TILE, stride=SUBLANES), :] = m[:, c * LANES:(c + 1) * LANES]
    logits = _dot_hi(m, rw_ref[...]) + rb_ref[...]
    lane = lax.broadcasted_iota(jnp.int32, logits.shape, 1)
    ti = jnp.zeros(logits.shape, jnp.int32)
    tv = jnp.full(logits.shape, -1e30, F32)
    for j in range(TOP_K):
        mx = jnp.max(logits, axis=-1, keepdims=True)
        idx = jnp.min(jnp.where(logits == mx, lane, LANES), axis=-1, keepdims=True)
        ti = jnp.where(lane == j, idx, ti)
        tv = jnp.where(lane == j, mx, tv)
        logits = jnp.where(lane == idx, -jnp.inf, logits)
    e = jnp.exp(tv - jnp.max(tv, axis=-1, keepdims=True))
    ti_ref[0] = ti
    gt_ref[0] = e / jnp.sum(e, axis=-1, keepdims=True)


def _mixout(alpha, sub_scale, tile0, n_tiles, n_ctx_tiles,
            ao, gof, gob, zg, yf, yb, bonus, rg, h, mod2, wo, vec, nrm, rw_p, rb_p):
    bsz, lt, d = h.shape
    row = lambda b_, i: (b_, tile0 + i, 0)
    orow = lambda b_, i: (b_, i, 0)
    full = lambda shape: pl.BlockSpec(shape, lambda b_, i: tuple(0 for _ in shape))
    nrows = n_tiles * ROW_TILE
    return pl.pallas_call(
        functools.partial(_mixout_kernel, alpha, sub_scale),
        out_shape=(jax.ShapeDtypeStruct((bsz, nrows, d), F32),
                   jax.ShapeDtypeStruct((bsz, nrows * (d // LANES), LANES), F32),
                   jax.ShapeDtypeStruct((bsz, nrows, LANES), jnp.int32),
                   jax.ShapeDtypeStruct((bsz, nrows, LANES), F32)),
        grid=(bsz, n_tiles),
        in_specs=[pl.BlockSpec((1, ROW_TILE, DA_W), orow),
                  pl.BlockSpec((1, ROW_TILE, GLA_W), row),
                  pl.BlockSpec((1, ROW_TILE, GLA_W), row),
                  pl.BlockSpec((1, ROW_TILE, Z_GLA), row),
                  pl.BlockSpec((1, ROW_TILE, RW_W), row),
                  pl.BlockSpec((1, ROW_TILE, RW_W), row),
                  pl.BlockSpec((1, ROW_TILE, RW_W), row),
                  pl.BlockSpec((1, ROW_TILE, RW_W), row),
                  pl.BlockSpec((1, ROW_TILE, d), row),
                  pl.BlockSpec((1, 1, 3, d),
                               lambda b_, i: (b_, jnp.where(tile0 + i < n_ctx_tiles, 0, 1), 0, 0)),
                  full((MIX_W, d)), full((2, d)), full((4, GLA_W)),
                  full((DA_W, DA_W)), full((GLA_W, GLA_W)), full((d, LANES)), full((1, LANES))],
        out_specs=(pl.BlockSpec((1, ROW_TILE, d), orow),
                   pl.BlockSpec((1, ROW_TILE * (d // LANES), LANES), orow),
                   pl.BlockSpec((1, ROW_TILE, LANES), orow),
                   pl.BlockSpec((1, ROW_TILE, LANES), orow)),
        compiler_params=_cparams(("arbitrary", "arbitrary")),
        name="mix_out",
    )(ao, gof, gob, zg, yf, yb, bonus, rg, h, mod2, wo, vec, nrm,
      _group_matrix(DA_W, DA_V), _group_matrix(GLA_W, GLA_DV), rw_p, rb_p)


def _bwd_chunk(step, n_ctx_chunks, n_chunks):
    return jnp.where(step < n_ctx_chunks, n_ctx_chunks - 1 - step, n_chunks + n_ctx_chunks - 1 - step)


def _tri(n, reverse, strict):
    t = lax.broadcasted_iota(jnp.int32, (n, n), 0)
    s = lax.broadcasted_iota(jnp.int32, (n, n), 1)
    if reverse:
        return (s > t) if strict else (s >= t)
    return (s < t) if strict else (s <= t)


def _log_sigmoid(x):
    return jnp.minimum(x, 0.0) - jnp.log1p(jnp.exp(-jnp.abs(x)))


def _softplus(x):
    return jnp.maximum(x, 0.0) + jnp.log1p(jnp.exp(-jnp.abs(x)))


def _split3(x):
    t0 = x.astype(BF16)
    r1 = x - t0.astype(F32)
    t1 = r1.astype(BF16)
    t2 = (r1 - t1.astype(F32)).astype(BF16)
    return t0, t1, t2


def _cumsum_rows(x, reverse):
    tri = jnp.where(_tri(CHUNK, reverse, False), 1.0, 0.0).astype(BF16)
    return sum(jnp.dot(tri, t, preferred_element_type=F32) for t in _split3(x))


def _bdot_nt(a, b):
    return lax.dot_general(a, b, (((1,), (1,)), ((), ())), preferred_element_type=F32)


GLA_SUB = 16
GLA_EXP_CLAMP = 60.0


def _gla_kernel(zf_ref, zb_ref, w2_ref, bias_ref, of_ref, ob_ref, st_ref):
    @pl.when(pl.program_id(1) == 0)
    def _():
        st_ref[...] = jnp.zeros_like(st_ref)

    c = CHUNK
    nsub = c // GLA_SUB
    nb = zf_ref.shape[0]
    qs, ks, vs, bs, b_lasts = [], [], [], [], []
    for i, reverse, z_ref in [(i, rv, zr) for i in range(nb) for rv, zr in ((False, zf_ref), (True, zb_ref))]:
        zg = z_ref[i]
        col0 = GLA_KP if reverse else 0
        gpre = _dot(zg[:, 2 * GLA_KP + 2 * GLA_W:Z_GLA], w2_ref[:, col0:col0 + GLA_KP]) \
            + bias_ref[:, col0:col0 + GLA_KP]
        b = _cumsum_rows(_log_sigmoid(gpre) * (1.0 / GLA_TAU), reverse)
        last = 0 if reverse else c - 1
        qs.append(zg[:, 0:GLA_KP] * (GLA_DK ** -0.5))
        ks.append(zg[:, GLA_KP:2 * GLA_KP])
        vs.append(zg[:, 2 * GLA_KP:2 * GLA_KP + GLA_W])
        bs.append(b)
        b_lasts.append(b[last:last + 1, :])
    q, k, v, b, b_last = (jnp.stack(x, axis=0) for x in (qs, ks, vs, bs, b_lasts))
    vb = v.astype(BF16)
    st = st_ref[...]
    inter = _bmm_nt((q * jnp.exp(b)).astype(BF16), st.astype(BF16))
    kv = _bmm_tn(vb, (k * jnp.exp(b_last - b)).astype(BF16))
    vi = lax.broadcasted_iota(jnp.int32, (GLA_W, GLA_KP), 0) // GLA_DV
    ki = lax.broadcasted_iota(jnp.int32, (GLA_W, GLA_KP), 1) // GLA_DK
    st_ref[...] = st * jnp.exp(b_last) + jnp.where((vi == ki)[None], kv, 0.0)

    lane_head = lax.broadcasted_iota(jnp.int32, (1, 1, GLA_KP), 2) // GLA_DK
    per_head = [[None] * nsub for _ in range(GLA_HEADS)]
    for i in range(nsub):
        r0 = i * GLA_SUB
        b_ref = jnp.stack([b[p, r0:r0 + 1] if p % 2 == 0 else b[p, r0 + GLA_SUB - 1:r0 + GLA_SUB]
                           for p in range(2 * nb)], axis=0)
        qi = q[:, r0:r0 + GLA_SUB] * jnp.exp(b[:, r0:r0 + GLA_SUB] - b_ref)
        ki_ = (k * jnp.exp(jnp.minimum(b_ref - b, GLA_EXP_CLAMP))).astype(BF16)
        qh = jnp.concatenate([jnp.where(lane_head == h, qi, 0.0) for h in range(GLA_HEADS)], axis=1)
        a = _bmm_nt(qh.astype(BF16), ki_)
        for h in range(GLA_HEADS):
            per_head[h][i] = a[:, h * GLA_SUB:(h + 1) * GLA_SUB]
    causal = jnp.stack([_tri(c, False, False), _tri(c, True, False)] * nb, axis=0)
    lane = lax.broadcasted_iota(jnp.int32, (1, 1, LANES), 2)
    pieces = []
    for p in range(GLA_HEADS // 2):
        vp = vb[:, :, p * LANES:(p + 1) * LANES]
        halves = []
        for hh in range(2):
            a_h = jnp.where(causal, jnp.concatenate(per_head[2 * p + hh], axis=1), 0.0)
            halves.append(_bmm(a_h.astype(BF16), vp))
        pieces.append(jnp.where(lane < GLA_DV, halves[0], halves[1]))
    o = inter + jnp.concatenate(pieces, axis=2)
    for i in range(nb):
        of_ref[i] = o[2 * i]
        ob_ref[i] = o[2 * i + 1]


def _gla_scan(zg, w2p, biasp, n_ctx_chunks):
    bsz, lt, _ = zg.shape
    nc = lt // CHUNK
    nb = SCAN_BATCH if bsz % SCAN_BATCH == 0 else 1
    fwd = lambda b_, s: (b_, s, 0)
    bwd = lambda b_, s: (b_, _bwd_chunk(s, n_ctx_chunks, nc), 0)
    return pl.pallas_call(
        _gla_kernel,
        out_shape=(jax.ShapeDtypeStruct((bsz, lt, GLA_W), F32),
                   jax.ShapeDtypeStruct((bsz, lt, GLA_W), F32)),
        grid=(bsz // nb, nc),
        in_specs=[pl.BlockSpec((nb, CHUNK, Z_GLA), fwd),
                  pl.BlockSpec((nb, CHUNK, Z_GLA), bwd),
                  pl.BlockSpec((LANES, 2 * GLA_KP), lambda b_, s: (0, 0)),
                  pl.BlockSpec((1, 2 * GLA_KP), lambda b_, s: (0, 0))],
        out_specs=(pl.BlockSpec((nb, CHUNK, GLA_W), fwd),
                   pl.BlockSpec((nb, CHUNK, GLA_W), bwd)),
        scratch_shapes=[pltpu.VMEM((2 * nb, GLA_W, GLA_KP), F32)],
        compiler_params=_cparams(("arbitrary", "arbitrary")),
        name="gla_scan",
    )(zg, zg, w2p, biasp)


def _pack_gla_gate(w2, bias):
    w = jnp.zeros((LANES, 2 * GLA_KP), F32)
    w = w.at[0:GLA_RANK, 0:GLA_KW].set(w2[0]).at[GLA_RANK:2 * GLA_RANK, GLA_KP:GLA_KP + GLA_KW].set(w2[1])
    b = jnp.zeros((1, 2 * GLA_KP), F32)
    b = b.at[0, 0:GLA_KW].set(bias[0]).at[0, GLA_KP:GLA_KP + GLA_KW].set(bias[1])
    return w, b


def _rwprep_kernel(n_ctx_tiles, n_tiles,
                   z_ref, zp_ref, zn_ref, cw_ref, w2_ref, a2_ref, g2_ref, vec_ref, gm_ref,
                   sh_ref, df_ref, db_ref, g_ref, bonus_ref):
    i = pl.program_id(1)
    z = z_ref[0]
    x = z[:, 0:3 * RW_W]
    seg_first = (i == 0) | (i == n_ctx_tiles)
    seg_last = (i == n_ctx_tiles - 1) | (i == n_tiles - 1)
    prev_row = jnp.where(seg_first, 0.0, zp_ref[0, 7:8, 0:3 * RW_W])
    next_row = jnp.where(seg_last, 0.0, zn_ref[0, 0:1, 0:3 * RW_W])
    ridx = lax.broadcasted_iota(jnp.int32, (ROW_TILE, 1), 0)
    x_prev = jnp.where(ridx == 0, prev_row, pltpu.roll(x, 1, 0))
    x_next = jnp.where(ridx == ROW_TILE - 1, next_row, pltpu.roll(x, ROW_TILE - 1, 0))
    xc = x_prev * cw_ref[0:1, :] + x * cw_ref[1:2, :] + x_next * cw_ref[2:3, :]
    r = xc[:, 0:RW_W]
    k = xc[:, RW_W:2 * RW_W]
    v = xc[:, 2 * RW_W:3 * RW_W]
    gm = gm_ref[...]
    kk = k * vec_ref[0:1, :]
    kk = kk / jnp.maximum(jnp.sqrt(_group_sum(kk * kk, gm)), 1e-12)
    k_a = vec_ref[1:2, :]
    r_k = vec_ref[2:3, :]
    w_raw = _dot_hi(jnp.tanh(z[:, 3 * RW_W:3 * RW_W + LANES]), w2_ref[...])
    a_raw = _dot_hi(z[:, 3 * RW_W + LANES:3 * RW_W + 2 * LANES], a2_ref[...])
    g_ref[0] = _dot_hi(jax.nn.sigmoid(z[:, 3 * RW_W + 2 * LANES:Z_RW]), g2_ref[...])
    sh_ref[0, :, 0:RW_W] = r
    sh_ref[0, :, RW_W:2 * RW_W] = v
    sh_ref[0, :, 2 * RW_W:3 * RW_W] = kk
    rk_sum = None
    for d, d_ref in enumerate((df_ref, db_ref)):
        wr = w_raw[:, d * RW_W:(d + 1) * RW_W] + vec_ref[3 + d:4 + d, :]
        logw = -jnp.exp(-_softplus(-wr) - 0.5)
        a = jax.nn.sigmoid(a_raw[:, d * RW_W:(d + 1) * RW_W] + vec_ref[5 + d:6 + d, :])
        k_mod = k * (1.0 + (a - 1.0) * k_a)
        d_ref[0, :, 0:RW_W] = logw
        d_ref[0, :, RW_W:2 * RW_W] = kk * a
        d_ref[0, :, 2 * RW_W:3 * RW_W] = k_mod
        s = _group_sum(r * k_mod * r_k, gm)
        rk_sum = s if d == 0 else rk_sum + s
    bonus_ref[0] = rk_sum * v


def _rw_prep(zr, cw, w2p, a2p, g2, vec, n_ctx_tiles):
    bsz, lt, _ = zr.shape
    nt = lt // ROW_TILE
    hb = ROW_TILE // 8
    row = lambda b_, i: (b_, i, 0)
    full = lambda shape: pl.BlockSpec(shape, lambda b_, i: tuple(0 for _ in shape))
    o3 = jax.ShapeDtypeStruct((bsz, lt, 3 * RW_W), F32)
    o1 = jax.ShapeDtypeStruct((bsz, lt, RW_W), F32)
    return pl.pallas_call(
        functools.partial(_rwprep_kernel, n_ctx_tiles, nt),
        out_shape=(o3, o3, o3, o1, o1),
        grid=(bsz, nt),
        in_specs=[pl.BlockSpec((1, ROW_TILE, Z_RW), row),
                  pl.BlockSpec((1, 8, Z_RW), lambda b_, i: (b_, jnp.maximum(i * hb - 1, 0), 0)),
                  pl.BlockSpec((1, 8, Z_RW), lambda b_, i: (b_, jnp.minimum((i + 1) * hb, nt * hb - 1), 0)),
                  full((3, 3 * RW_W)), full((LANES, 2 * RW_W)), full((LANES, 2 * RW_W)),
                  full((RW_GATE_RANK, RW_W)), full((8, RW_W)), full((RW_W, RW_W))],
        out_specs=(pl.BlockSpec((1, ROW_TILE, 3 * RW_W), row),
                   pl.BlockSpec((1, ROW_TILE, 3 * RW_W), row),
                   pl.BlockSpec((1, ROW_TILE, 3 * RW_W), row),
                   pl.BlockSpec((1, ROW_TILE, RW_W), row),
                   pl.BlockSpec((1, ROW_TILE, RW_W), row)),
        compiler_params=_cparams(("arbitrary", "arbitrary")),
        name="rwkv_prep",
    )(zr, zr, zr, cw, w2p, a2p, g2, vec, _group_matrix(RW_W, RW_D))


def _block2(w):
    r, n = w.shape[1:]
    z = jnp.zeros((r, n), w.dtype)
    return jnp.concatenate([jnp.concatenate([w[0], z], 1), jnp.concatenate([z, w[1]], 1)], 0)


def _bmm(a, b):
    return lax.dot_general(a, b, (((2,), (1,)), ((0,), (0,))), preferred_element_type=F32)


def _bmm_nt(a, b):
    return lax.dot_general(a, b, (((2,), (2,)), ((0,), (0,))), preferred_element_type=F32)


def _bmm_tn(a, b):
    return lax.dot_general(a, b, (((1,), (1,)), ((0,), (0,))), preferred_element_type=F32)


def _rw_operands(sh, dd, reverse):
    c = CHUNK
    r, v, kk = sh[:, 0:RW_W], sh[:, RW_W:2 * RW_W], sh[:, 2 * RW_W:3 * RW_W]
    logw, beta, k = dd[:, 0:RW_W], dd[:, RW_W:2 * RW_W], dd[:, 2 * RW_W:3 * RW_W]
    b = _cumsum_rows(logw, reverse)
    last = 0 if reverse else c - 1
    e_b = jnp.exp(b)
    e_nb = jnp.exp(-b)
    e_last = jnp.exp(b[last:last + 1, :])
    e_tot = e_last * e_nb
    abar = (kk * jnp.exp(b - logw)).astype(BF16)
    rbar = (r * e_b).astype(BF16)
    kt = (k * e_nb).astype(BF16)
    bt = (beta * e_nb).astype(BF16)
    khat = (k * e_tot).astype(BF16)
    nbhat = (-(beta * e_tot)).astype(BF16)
    vb = v.astype(BF16)
    head_a = lax.broadcasted_iota(jnp.int32, (1, LANES), 1) < RW_D
    zero = jnp.zeros((c, LANES), BF16)

    def stack(*xs):
        rows = []
        for x in xs:
            rows += [jnp.where(head_a, x, zero), jnp.where(head_a, zero, x)]
        return jnp.concatenate(rows, axis=0)

    out = []
    for p in range(RW_HEADS // 2):
        sl = slice(p * LANES, (p + 1) * LANES)
        out.append(dict(xar=stack(abar[:, sl], rbar[:, sl]), yb=stack(bt[:, sl]), yk=stack(kt[:, sl]),
                        vs=stack(vb[:, sl]), kb=stack(khat[:, sl], nbhat[:, sl]),
                        e_col=jnp.broadcast_to(e_last[:, sl], (LANES, LANES)).T))
    return out


def _rw_kernel(sf_ref, df_ref, sb_ref, db_ref, yf_ref, yb_ref, h_ref):
    @pl.when(pl.program_id(1) == 0)
    def _():
        h_ref[...] = jnp.zeros_like(h_ref)

    c = CHUNK
    c2 = 2 * c
    npair = RW_HEADS // 2
    nb = sf_ref.shape[0]
    ops = []
    for i in range(nb):
        ops += _rw_operands(sf_ref[i], df_ref[i], False) + _rw_operands(sb_ref[i], db_ref[i], True)
    cat = lambda name: jnp.stack([o[name] for o in ops], axis=0)
    xar, yb_, yk, vs, kb, e_col = (cat(n) for n in ("xar", "yb", "yk", "vs", "kb", "e_col"))

    ti = lax.broadcasted_iota(jnp.int32, (c2, c2), 0)
    si = lax.broadcasted_iota(jnp.int32, (c2, c2), 1)
    same_head = (ti // c) == (si // c)
    both = lambda fwd, bwd: jnp.concatenate([jnp.broadcast_to(fwd[None], (npair, c2, c2)),
                                             jnp.broadcast_to(bwd[None], (npair, c2, c2))] * nb, axis=0)
    strict = both(same_head & (si < ti), same_head & (si > ti))
    incl = both(same_head & (si <= ti), same_head & (si >= ti))
    eye = jnp.where(ti == si, 1.0, 0.0).astype(F32)

    gb = _bmm_nt(xar, yb_)
    gk = _bmm_nt(xar, yk)
    l_ab = jnp.where(strict, gb[:, 0:c2], 0.0)
    l_rb = jnp.where(incl, gb[:, c2:2 * c2], 0.0).astype(BF16)
    l_ak = jnp.where(strict, gk[:, 0:c2], 0.0)
    l_rk = jnp.where(incl, gk[:, c2:2 * c2], 0.0)
    t_inv = None
    s = 1
    while s < c:
        same = (ti // (2 * s)) == (si // (2 * s))
        lo, hi = (ti // s) % 2, (si // s) % 2
        off = both(same & (lo == 1) & (hi == 0), same & (lo == 0) & (hi == 1))
        l_off = jnp.where(off, l_ab, 0.0)
        if t_inv is None:
            t_inv = eye[None] - l_off
        else:
            tb = t_inv.astype(BF16)
            t_inv = t_inv - _bmm(tb, _bmm(l_off.astype(BF16), tb).astype(BF16))
        s *= 2
    h0 = h_ref[...]
    xh = _bmm(xar, h0.astype(BF16))
    lv = _bmm(jnp.concatenate([l_ak, l_rk], axis=1).astype(BF16), vs)
    ub = _bmm(t_inv.astype(BF16), (xh[:, 0:c2] + lv[:, 0:c2]).astype(BF16)).astype(BF16)
    y2 = xh[:, c2:2 * c2] + lv[:, c2:2 * c2] - _bmm(l_rb, ub)
    y = y2[:, 0:c] + y2[:, c:c2]
    h_ref[...] = e_col * h0 + _bmm_tn(kb, jnp.concatenate([vs, ub], axis=1))
    for i in range(nb):
        o = 2 * npair * i
        yf_ref[i] = jnp.concatenate([y[o + p] for p in range(npair)], axis=1)
        yb_ref[i] = jnp.concatenate([y[o + npair + p] for p in range(npair)], axis=1)


def _rw_scan(shared, dfw, dbw, n_ctx_chunks):
    bsz, lt, _ = shared.shape
    nc = lt // CHUNK
    nb = SCAN_BATCH if bsz % SCAN_BATCH == 0 else 1
    fwd = lambda b_, s: (b_, s, 0)
    bwd = lambda b_, s: (b_, _bwd_chunk(s, n_ctx_chunks, nc), 0)
    return pl.pallas_call(
        _rw_kernel,
        out_shape=(jax.ShapeDtypeStruct((bsz, lt, RW_W), F32),
                   jax.ShapeDtypeStruct((bsz, lt, RW_W), F32)),
        grid=(bsz // nb, nc),
        in_specs=[pl.BlockSpec((nb, CHUNK, 3 * RW_W), fwd),
                  pl.BlockSpec((nb, CHUNK, 3 * RW_W), fwd),
                  pl.BlockSpec((nb, CHUNK, 3 * RW_W), bwd),
                  pl.BlockSpec((nb, CHUNK, 3 * RW_W), bwd)],
        out_specs=(pl.BlockSpec((nb, CHUNK, RW_W), fwd),
                   pl.BlockSpec((nb, CHUNK, RW_W), bwd)),
        scratch_shapes=[pltpu.VMEM((nb * RW_HEADS, LANES, LANES), F32)],
        compiler_params=_cparams(("arbitrary", "arbitrary")),
        name="rwkv_scan",
    )(shared, dfw, shared, dbw)


def _rank_kernel(n, ti_ref, dest_ref, meta_ref, cnt_ref, run_ref, start_ref):
    ph = pl.program_id(0)
    i = pl.program_id(1)
    ti = ti_ref[...]
    lane = lax.broadcasted_iota(jnp.int32, (ROW_TILE, LANES), 1)
    ohs = [jnp.where(ti[:, j:j + 1] == lane, 1.0, 0.0).astype(F32) for j in range(TOP_K)]
    oh = ohs[0] + ohs[1] + ohs[2] + ohs[3]
    tile_cnt = jnp.sum(oh, axis=0, keepdims=True)

    @pl.when((ph == 0) & (i == 0))
    def _():
        cnt_ref[...] = jnp.zeros_like(cnt_ref)
        run_ref[...] = jnp.zeros_like(run_ref)

    @pl.when(ph == 0)
    def _():
        cnt_ref[...] += tile_cnt

    @pl.when((ph == 0) & (i == n - 1))
    def _():
        cnt = cnt_ref[...]
        shift = MOE_BLOCK.bit_length() - 1
        padded = jnp.left_shift(jnp.right_shift(cnt.astype(jnp.int32) + (MOE_BLOCK - 1), shift),
                                shift).astype(F32)
        e0 = lax.broadcasted_iota(jnp.int32, (LANES, LANES), 0)
        e1 = lax.broadcasted_iota(jnp.int32, (LANES, LANES), 1)
        before = jnp.where(e0 < e1, 1.0, 0.0).astype(F32)
        start = _dot_hi(jnp.broadcast_to(padded, (8, LANES)), before)[0:1]
        start_ref[...] = start
        meta_ref[0:1, :] = cnt
        meta_ref[1:2, :] = start
        meta_ref[2:3, :] = padded
        meta_ref[3:8, :] = jnp.zeros((5, LANES), F32)

    @pl.when(ph == 1)
    def _():
        t0 = lax.broadcasted_iota(jnp.int32, (ROW_TILE, ROW_TILE), 0)
        t1 = lax.broadcasted_iota(jnp.int32, (ROW_TILE, ROW_TILE), 1)
        earlier = jnp.where(t1 < t0, 1.0, 0.0).astype(BF16)
        pos = (jnp.dot(earlier, oh.astype(BF16), preferred_element_type=F32)
               + run_ref[...] + start_ref[...])
        dest = jnp.zeros((ROW_TILE, LANES), F32)
        for j in range(TOP_K):
            dj = jnp.sum(ohs[j] * pos, axis=-1, keepdims=True)
            dest = jnp.where(lane == j, dj, dest)
        dest_ref[0] = dest.T[0:8, :].astype(jnp.int32)
        run_ref[...] += tile_cnt


def _moe_rank(ti):
    n = ti.shape[0]
    nt = n // ROW_TILE
    return pl.pallas_call(
        functools.partial(_rank_kernel, nt),
        out_shape=(jax.ShapeDtypeStruct((nt, 8, ROW_TILE), jnp.int32),
                   jax.ShapeDtypeStruct((8, LANES), F32)),
        grid=(2, nt),
        in_specs=[pl.BlockSpec((ROW_TILE, LANES), lambda p, i: (i, 0))],
        out_specs=(pl.BlockSpec((1, 8, ROW_TILE), lambda p, i: (i * p, 0, 0)),
                   pl.BlockSpec((8, LANES), lambda p, i: (0, 0))),
        scratch_shapes=[pltpu.VMEM((1, LANES), F32), pltpu.VMEM((1, LANES), F32),
                        pltpu.VMEM((1, LANES), F32)],
        compiler_params=_cparams(("arbitrary", "arbitrary")),
        name="moe_rank",
    )(ti)


MOE_TOK_BITS = 15


def _invert_kernel(n_tiles, plane_rows, lo_ref, hi_ref, dest_hbm, code_ref, idx_s, sem):
    s = pl.program_id(0)

    def fetch(i, slot):
        return pltpu.make_async_copy(dest_hbm.at[i], idx_s.at[slot], sem.at[slot])

    @pl.when(s < N_EXPERTS)
    def _():
        def fill(p, c):
            code_ref[p] = jnp.left_shift(TOP_K * plane_rows + jnp.bitwise_and(p, MOE_BLOCK - 1), MOE_TOK_BITS)
            return c

        lax.fori_loop(lo_ref[s], hi_ref[s], fill, 0)

    @pl.when(s == N_EXPERTS)
    def _():
        fetch(0, 0).start()

    @pl.when(s >= N_EXPERTS)
    def _():
        i = s - N_EXPERTS
        slot = i % 2

        @pl.when(i + 1 < n_tiles)
        def _():
            fetch(i + 1, 1 - slot).start()

        fetch(i, slot).wait()

        def tok(t, c):
            token = i * ROW_TILE + t
            for j in range(TOP_K):
                code_ref[idx_s[slot, 0, j * ROW_TILE + t]] = jnp.bitwise_or(
                    jnp.left_shift(j * plane_rows + token, MOE_TOK_BITS), token)
            return c

        lax.fori_loop(0, ROW_TILE, tok, 0, unroll=4)


def _moe_invert(dest2d, n_blocks, plane_rows, pad_lo, pad_hi):
    nt = dest2d.shape[0]
    assert nt * ROW_TILE < 2 ** MOE_TOK_BITS and TOP_K * plane_rows + MOE_BLOCK <= 2 ** (32 - MOE_TOK_BITS)
    return pl.pallas_call(
        functools.partial(_invert_kernel, nt, plane_rows),
        out_shape=jax.ShapeDtypeStruct((n_blocks * MOE_BLOCK,), jnp.int32),
        grid_spec=pltpu.PrefetchScalarGridSpec(
            num_scalar_prefetch=2,
            grid=(N_EXPERTS + nt,),
            in_specs=[pl.BlockSpec(memory_space=pl.ANY)],
            out_specs=pl.BlockSpec(memory_space=pltpu.SMEM),
            scratch_shapes=[pltpu.SMEM((2, 1, 8 * ROW_TILE), jnp.int32), pltpu.SemaphoreType.DMA((2,))]),
        compiler_params=_cparams(("arbitrary",)),
        name="moe_invert",
    )(pad_lo, pad_hi, dest2d.reshape(nt, 1, 8 * ROW_TILE))


def _fused_expert_kernel(plane_rows,
                         be_ref, nu_ref, code_hbm, m_hbm, wg_ref, bg_ref, wu_ref, bu_ref, wd_ref, bd_ref,
                         out_hbm, wg_s, wu_s, wd_s, xbuf0, xbuf1, ybuf0, ybuf1, idx_s, sem_i, sem_g, sem_s):
    b = pl.program_id(0)
    nu = nu_ref[0]
    blk = MOE_BLOCK
    xbufs = (xbuf0, xbuf1)
    ybufs = (ybuf0, ybuf1)

    def idx_copy(block, slot):
        return pltpu.make_async_copy(code_hbm.at[block], idx_s.at[slot], sem_i.at[slot])

    def gather(slot, r, buf):
        tok = jnp.bitwise_and(idx_s[slot, 0, r], 2 ** MOE_TOK_BITS - 1)
        return pltpu.make_async_copy(m_hbm.at[tok], xbufs[buf].at[pl.ds(r * SUBLANES, SUBLANES), :], sem_g.at[buf])

    def scatter(slot, r, buf):
        row = lax.shift_right_logical(idx_s[slot, 0, r], MOE_TOK_BITS)
        return pltpu.make_async_copy(ybufs[buf].at[pl.ds(r * SUBLANES, SUBLANES), :], out_hbm.at[row], sem_s.at[buf])

    def wait_rows(buf, sem):
        pltpu.make_async_copy(m_hbm.at[pl.ds(0, blk)], xbufs[buf].reshape(blk, SUBLANES, LANES), sem.at[buf]).wait()

    def step(cur):
        oth = 1 - cur
        s_prev, s_cur, s_next, s_far = (b + 3) % 4, b % 4, (b + 1) % 4, (b + 2) % 4
        idx_copy(jnp.minimum(b + 2, nu - 1), s_far).start()
        idx_copy(0, s_next).wait()
        for r in range(blk):
            gather(s_next, r, oth).start()
            scatter(s_prev, r, oth).start()
        wait_rows(cur, sem_g)
        nch = wg_s.shape[0] // LANES
        x = jnp.concatenate([xbufs[cur][pl.ds(c, blk, stride=SUBLANES), :] for c in range(nch)],
                            axis=1).astype(BF16)
        gt = jnp.minimum(jnp.dot(x, wg_s[...], preferred_element_type=F32) + bg_ref[0, 0], SWIGLU_LIMIT)
        up = jnp.clip(jnp.dot(x, wu_s[...], preferred_element_type=F32) + bu_ref[0, 0],
                      -SWIGLU_LIMIT, SWIGLU_LIMIT)
        act = (up + 1.0) * gt * jax.nn.sigmoid(SWIGLU_ALPHA * gt)
        y = jnp.dot(act.astype(BF16), wd_s[...], preferred_element_type=F32) + bd_ref[0, 0]
        for c in range(nch):
            ybufs[cur][pl.ds(c, blk, stride=SUBLANES), :] = y[:, c * LANES:(c + 1) * LANES]
        wait_rows(oth, sem_s)

        @pl.when(b == nu - 1)
        def _():
            wait_rows(oth, sem_g)
            idx_copy(0, s_far).wait()

            def last(r, c):
                scatter(s_cur, r, cur).start()
                return c

            lax.fori_loop(0, blk, last, 0, unroll=8)
            wait_rows(cur, sem_s)

    @pl.when(b < nu)
    def _():
        e = be_ref[b]
        changed = (b == 0) | (e != be_ref[jnp.maximum(b - 1, 0)])

        @pl.when(changed)
        def _():
            wg_s[...] = wg_ref[0, 0].astype(BF16)
            wu_s[...] = wu_ref[0, 0].astype(BF16)
            wd_s[...] = wd_ref[0, 0].astype(BF16)

        @pl.when(b == 0)
        def _():
            ybuf0[...] = jnp.zeros_like(ybuf0)
            ybuf1[...] = jnp.zeros_like(ybuf1)
            cp = idx_copy(0, 0)
            cp.start()
            cp.wait()
            idx_copy(jnp.minimum(1, nu - 1), 1).start()

            def spare(r, c):
                idx_s[3, 0, r] = jnp.left_shift(TOP_K * plane_rows + r, MOE_TOK_BITS)
                return c

            lax.fori_loop(0, blk, spare, 0, unroll=8)

            def first(r, c):
                gather(0, r, 0).start()
                return c

            lax.fori_loop(0, blk, first, 0, unroll=8)

        @pl.when(b % 2 == 0)
        def _():
            step(0)

        @pl.when(b % 2 == 1)
        def _():
            step(1)


def _moe_experts_fused(layer, block_expert, n_used, codes, m, plane_rows, wg, bg, wu, bu, wd, bd):
    nch = m.shape[1]
    d = nch * LANES
    nl, ne, _, f = wg.shape
    nb = codes.shape[0]
    wmap = lambda b, be, nu: (layer, be[jnp.maximum(jnp.minimum(b, nu[0] - 1), 0)], 0, 0)
    any_spec = pl.BlockSpec(memory_space=pl.ANY)
    return pl.pallas_call(
        functools.partial(_fused_expert_kernel, plane_rows),
        out_shape=jax.ShapeDtypeStruct((TOP_K * plane_rows + MOE_BLOCK, nch, LANES), F32),
        grid_spec=pltpu.PrefetchScalarGridSpec(
            num_scalar_prefetch=2,
            grid=(nb,),
            in_specs=[any_spec, any_spec,
                      pl.BlockSpec((1, 1, d, f), wmap), pl.BlockSpec((1, 1, 1, f), wmap),
                      pl.BlockSpec((1, 1, d, f), wmap), pl.BlockSpec((1, 1, 1, f), wmap),
                      pl.BlockSpec((1, 1, f, d), wmap), pl.BlockSpec((1, 1, 1, d), wmap)],
            out_specs=any_spec,
            scratch_shapes=[pltpu.VMEM((d, f), BF16), pltpu.VMEM((d, f), BF16), pltpu.VMEM((f, d), BF16),
                            pltpu.VMEM((MOE_BLOCK * nch, LANES), F32), pltpu.VMEM((MOE_BLOCK * nch, LANES), F32),
                            pltpu.VMEM((MOE_BLOCK * nch, LANES), F32), pltpu.VMEM((MOE_BLOCK * nch, LANES), F32),
                            pltpu.SMEM((4, 1, MOE_BLOCK), jnp.int32),
                            pltpu.SemaphoreType.DMA((4,)), pltpu.SemaphoreType.DMA((2,)),
                            pltpu.SemaphoreType.DMA((2,))]),
        compiler_params=_cparams(("arbitrary",)),
        name="moe_experts",
    )(block_expert, n_used, codes.reshape(nb, 1, MOE_BLOCK), m, wg, bg.reshape(nl, ne, 1, f),
      wu, bu.reshape(nl, ne, 1, f), wd, bd.reshape(nl, ne, 1, d))


def _combine2_kernel(alpha, gt_ref, h_ref, mod_ref, vec_ref, y0_ref, y1_ref, y2_ref, y3_ref, o_ref):
    gt = gt_ref[0]
    d = h_ref.shape[2]
    xs = []
    for c in range(d // LANES):
        rows = pl.ds(c, ROW_TILE, stride=SUBLANES)
        f = (gt[:, 0:1] * y0_ref[rows, :] + gt[:, 1:2] * y1_ref[rows, :]
             + gt[:, 2:3] * y2_ref[rows, :] + gt[:, 3:4] * y3_ref[rows, :])
        cols = slice(c * LANES, (c + 1) * LANES)
        xs.append(alpha * h_ref[0, :, cols] + mod_ref[0, 0, 0:1, cols] * f)
    mu = sum(jnp.sum(x, axis=-1, keepdims=True) for x in xs) * (1.0 / d)
    xs = [x - mu for x in xs]
    var = sum(jnp.sum(x * x, axis=-1, keepdims=True) for x in xs) * (1.0 / d)
    inv = lax.rsqrt(var + LN_EPS)
    for c, x in enumerate(xs):
        cols = slice(c * LANES, (c + 1) * LANES)
        o_ref[0, :, cols] = x * inv * vec_ref[0:1, cols] + vec_ref[1:2, cols]


def _moe_combine2(alpha, n_ctx_tiles, tile0, plane_rows, gates, h1, mod3, vec, out4):
    bsz, rows, d = h1.shape
    nch = d // LANES
    nt = rows // ROW_TILE
    pt = plane_rows // ROW_TILE
    row = lambda b_, i: (b_, i, 0)
    plane = lambda j: pl.BlockSpec((ROW_TILE * nch, LANES), lambda b_, i: (j * pt + b_ * nt + i, 0))
    out4 = out4.reshape((TOP_K * plane_rows + MOE_BLOCK) * nch, LANES)
    return pl.pallas_call(
        functools.partial(_combine2_kernel, alpha),
        out_shape=jax.ShapeDtypeStruct((bsz, rows, d), F32),
        grid=(bsz, nt),
        in_specs=[pl.BlockSpec((1, ROW_TILE, LANES), row),
                  pl.BlockSpec((1, ROW_TILE, d), row),
                  pl.BlockSpec((1, 1, 1, d), lambda b_, i: (b_, jnp.where(tile0 + i < n_ctx_tiles, 0, 1), 0, 0)),
                  pl.BlockSpec((2, d), lambda b_, i: (0, 0)),
                  plane(0), plane(1), plane(2), plane(3)],
        out_specs=pl.BlockSpec((1, ROW_TILE, d), row),
        compiler_params=_cparams(("arbitrary", "arbitrary")),
        name="moe_combine",
    )(gates, h1, mod3, vec, out4, out4, out4, out4)


def _moe(layer, alpha, n_ctx_tiles, tile0, h1, m, ti, gates, mod3, ln2, wg, bg, wu, bu, wd, bd):
    bsz, rows, d = h1.shape
    n = bsz * rows
    dest, meta = _moe_rank(ti.reshape(n, LANES))
    dest2d = dest.reshape(n // ROW_TILE, 8 * ROW_TILE)
    n_blocks = -(-(n * TOP_K) // MOE_BLOCK) + N_EXPERTS
    pad_end = (meta[1, :N_EXPERTS] + meta[2, :N_EXPERTS]).astype(jnp.int32)
    block_row = jnp.arange(n_blocks, dtype=jnp.int32) * MOE_BLOCK
    block_expert = jnp.minimum(jnp.sum((pad_end[None, :] <= block_row[:, None]).astype(jnp.int32), axis=1),
                               N_EXPERTS - 1)
    n_used = (pad_end[-1:] // MOE_BLOCK).astype(jnp.int32)
    plane_rows = n
    pad_lo = (meta[1, :N_EXPERTS] + meta[0, :N_EXPERTS]).astype(jnp.int32)
    pad_hi = jnp.concatenate([pad_end[:-1], jnp.full((1,), n_blocks * MOE_BLOCK, jnp.int32)])
    codes = _moe_invert(dest2d, n_blocks, plane_rows, pad_lo, pad_hi).reshape(n_blocks, MOE_BLOCK)
    out4 = _moe_experts_fused(layer, block_expert, n_used, codes, m.reshape(n, d // LANES, LANES), plane_rows,
                              wg, bg, wu, bu, wd, bd)
    return _moe_combine2(alpha, n_ctx_tiles, tile0, plane_rows, gates, h1, mod3, ln2, out4)


def kernel(x, c, ctx, c_ctx, ln_in_g, ln_in_b, ada_w, ada_b, w_in, lam_q1, lam_k1, lam_q2, lam_k2,
           da_subln_g, gla_gate_w2, gla_gate_b, gla_norm_g, rw_conv_w, rw_w2, rw_w0, rw_a2, rw_a0,
           rw_g2, rw_k_k, rw_k_a, rw_r_k, rw_lnx_g, rw_lnx_b, w_out, ln1_g, ln1_b, router_w, router_b,
           moe_w_gate, moe_b_gate, moe_w_up, moe_b_up, moe_w_down, moe_b_down, ln2_g, ln2_b):
    bsz, seq, d = x.shape
    n_ctx = ctx.shape[1]
    depth = w_in.shape[0]
    assert n_ctx % ROW_TILE == 0 and seq % ROW_TILE == 0 and seq % GRID_W == 0
    assert w_in.shape[2] == 3488 and bsz + 1 <= 8
    nct = n_ctx // ROW_TILE
    ncc = n_ctx // CHUNK
    nt = (n_ctx + seq) // ROW_TILE
    alpha = (2 * depth) ** 0.25

    c_all = jnp.concatenate([c, c_ctx[None], jnp.zeros((8 - bsz - 1, d), F32)], axis=0)
    mods = _ada_mod(c_all, ada_w, ada_b).reshape(depth, 8, 6, d)
    h = _ln_in(ctx, x, ln_in_g, ln_in_b)
    cos_t, sin_t = _rope_tables(n_ctx, seq)

    def pick(l, idx):
        mc = jnp.broadcast_to(mods[l, bsz][None, idx], (bsz, len(idx), d))
        return jnp.stack([mc, mods[l, :bsz][:, idx]], axis=1)

    for l in range(depth):
        last = l == depth - 1
        tile0 = nct if last else 0
        n_out_tiles = nt - tile0
        lam_init = 0.8 - 0.6 * math.exp(-0.3 * l)
        lam = (jnp.exp(jnp.sum(lam_q1[l] * lam_k1[l])) - jnp.exp(jnp.sum(lam_q2[l] * lam_k2[l])) + lam_init)

        qk, v, zg, zr = _inproj(h, pick(l, [1, 0]), _pack_w_in(w_in[l]), cos_t, sin_t, nct)
        ao = _attention(qk, v, lam, nct, nt - nct, n_ctx + seq)
        if not last:
            ao = jnp.concatenate([_attention(qk, v, lam, 0, nct, n_ctx), ao], axis=1)
        w2p, biasp = _pack_gla_gate(gla_gate_w2[l], gla_gate_b[l])
        gof, gob = _gla_scan(zg, w2p, biasp, ncc)
        rvec = jnp.stack([rw_k_k[l], rw_k_a[l], rw_r_k[l].reshape(-1), rw_w0[l, 0], rw_w0[l, 1],
                          rw_a0[l, 0], rw_a0[l, 1], jnp.zeros((RW_W,), F32)], axis=0)
        shared, dfw, dbw, rg, bonus = _rw_prep(zr, rw_conv_w[l], _block2(rw_w2[l]), _block2(rw_a2[l]),
                                               rw_g2[l], rvec, nct)
        yf, yb = _rw_scan(shared, dfw, dbw, ncc)

        nrm = jnp.stack([jnp.pad(jnp.tile(da_subln_g[l], DA_HEADS), (0, GLA_W - DA_W)),
                         jnp.tile(gla_norm_g[l], GLA_HEADS), rw_lnx_g[l], rw_lnx_b[l]], axis=0)
        rw_p = jnp.pad(router_w[l], ((0, 0), (0, LANES - N_EXPERTS)))
        rb_p = jnp.pad(router_b[l], (0, LANES - N_EXPERTS), constant_values=-1e30).reshape(1, LANES)
        h1, m, ti, gates = _mixout(alpha, 1.0 - lam_init, tile0, n_out_tiles, nct,
                                   ao, gof, gob, zg, yf, yb, bonus, rg, h, pick(l, [2, 4, 3]),
                                   w_out[l].astype(BF16), jnp.stack([ln1_g[l], ln1_b[l]], 0), nrm, rw_p, rb_p)
        h = _moe(l, alpha, nct, tile0, h1, m, ti, gates, pick(l, [5]), jnp.stack([ln2_g[l], ln2_b[l]], 0),
                 moe_w_gate, moe_b_gate, moe_w_up, moe_b_up, moe_w_down, moe_b_down)
    return h
```

```python
import functools
import math

import jax
import jax.numpy as jnp
import numpy as np
from jax import lax
from jax.experimental import pallas as pl
from jax.experimental.pallas import tpu as pltpu

F32 = jnp.float32
BF16 = jnp.bfloat16
HI = lax.Precision.HIGHEST

GRID_W = 64
DA_HEADS, DA_QK, DA_V = 4, 32, 64
GLA_HEADS, GLA_DK, GLA_DV, GLA_RANK, GLA_TAU = 6, 32, 64, 16, 16.0
RW_HEADS, RW_D, RW_DECAY_RANK, RW_A_RANK, RW_GATE_RANK = 6, 64, 64, 64, 128
RW_GN_EPS = 64e-5
N_EXPERTS, TOP_K = 32, 4
SWIGLU_LIMIT, SWIGLU_ALPHA = 7.0, 1.702
ROPE_BASE = 10000.0
LN_EPS = 1e-5

DA_W = DA_HEADS * DA_V
GLA_KW = GLA_HEADS * GLA_DK
GLA_W = GLA_HEADS * GLA_DV
RW_W = RW_HEADS * RW_D
MIX_W = DA_W + GLA_W + RW_W

LANES = 128
ROW_TILE = 256
CHUNK = 64
SCAN_BATCH = 4
MOE_BLOCK = 256
VMEM_LIMIT = 56 * 1024 * 1024
SUBLANES = 8

Z_ATTN = 3 * DA_W
GLA_KP = 256
Z_GLA = 2 * GLA_KP + 2 * GLA_W + LANES
Z_RW = 3 * RW_W + 3 * LANES
Z_ALL = Z_ATTN + Z_GLA + Z_RW


def _cparams(sem):
    return pltpu.CompilerParams(dimension_semantics=sem, vmem_limit_bytes=VMEM_LIMIT)


def _ln(x, g, b, eps):
    mu = jnp.mean(x, axis=-1, keepdims=True)
    xc = x - mu
    var = jnp.mean(xc * xc, axis=-1, keepdims=True)
    return xc * lax.rsqrt(var + eps) * g + b


def _silu(x):
    return x * jax.nn.sigmoid(x)


def _dot(a, b):
    return jnp.dot(a.astype(BF16), b.astype(BF16), preferred_element_type=F32)


def _dot_hi(a, b):
    return jnp.dot(a, b, precision=HI, preferred_element_type=F32)


def _group_sum(x, gmat):
    hi = x.astype(BF16)
    lo = (x - hi.astype(F32)).astype(BF16)
    return (jnp.dot(hi, gmat, preferred_element_type=F32)
            + jnp.dot(lo, gmat, preferred_element_type=F32))


def _group_matrix(width, group):
    idx = np.arange(width) // group
    return jnp.asarray((idx[:, None] == idx[None, :]).astype(np.float32), dtype=BF16)


def _ada_kernel(c_ref, w_ref, b_ref, o_ref):
    o_ref[0] = _dot_hi(_silu(c_ref[...]), w_ref[0]) + b_ref[0]


def _ada_mod(c_all, ada_w, ada_b):
    nl, d, n6 = ada_w.shape
    tn = 1536
    return pl.pallas_call(
        _ada_kernel,
        out_shape=jax.ShapeDtypeStruct((nl, c_all.shape[0], n6), F32),
        grid=(nl, n6 // tn),
        in_specs=[pl.BlockSpec((c_all.shape[0], d), lambda l, j: (0, 0)),
                  pl.BlockSpec((1, d, tn), lambda l, j: (l, 0, j)),
                  pl.BlockSpec((1, 1, tn), lambda l, j: (l, 0, j))],
        out_specs=pl.BlockSpec((1, c_all.shape[0], tn), lambda l, j: (l, 0, j)),
        compiler_params=_cparams(("arbitrary", "arbitrary")),
        name="ada_mod",
    )(c_all, ada_w, ada_b.reshape(nl, 1, n6))


def _ln_in_kernel(n_ctx_tiles, c_ref, x_ref, g_ref, b_ref, o_ref):
    @pl.when(pl.program_id(1) < n_ctx_tiles)
    def _():
        o_ref[0] = _ln(c_ref[0], g_ref[...], b_ref[...], LN_EPS)

    @pl.when(pl.program_id(1) >= n_ctx_tiles)
    def _():
        o_ref[0] = _ln(x_ref[0], g_ref[...], b_ref[...], LN_EPS)


def _ln_in(ctx, x, g, b):
    bsz, n_ctx, d = ctx.shape
    nct = n_ctx // ROW_TILE
    nt = nct + x.shape[1] // ROW_TILE
    return pl.pallas_call(
        functools.partial(_ln_in_kernel, nct),
        out_shape=jax.ShapeDtypeStruct((bsz, nt * ROW_TILE, d), F32),
        grid=(bsz, nt),
        in_specs=[pl.BlockSpec((1, ROW_TILE, d), lambda b_, i: (b_, jnp.minimum(i, nct - 1), 0)),
                  pl.BlockSpec((1, ROW_TILE, d), lambda b_, i: (b_, jnp.maximum(i - nct, 0), 0)),
                  pl.BlockSpec((1, d), lambda b_, i: (0, 0)),
                  pl.BlockSpec((1, d), lambda b_, i: (0, 0))],
        out_specs=pl.BlockSpec((1, ROW_TILE, d), lambda b_, i: (b_, i, 0)),
        compiler_params=_cparams(("arbitrary", "arbitrary")),
        name="ln_in",
    )(ctx, x, g.reshape(1, d), b.reshape(1, d))


def _inproj_kernel(h_ref, mod_ref, w_ref, cos_ref, sin_ref, qk_ref, v_ref, gla_ref, rw_ref):
    h = h_ref[0]
    sc = mod_ref[0, 0, 0:1, :]
    sh = mod_ref[0, 0, 1:2, :]
    xm = (h * (1.0 + sc) + sh).astype(BF16)
    qk = jnp.dot(xm, w_ref[:, 0:2 * DA_W], preferred_element_type=F32)
    cos = cos_ref[...]
    sin = sin_ref[...]
    lane = lax.broadcasted_iota(jnp.int32, (1, LANES), 1)
    first = (lane % 16) < 8
    qscale = DA_QK ** -0.5
    for j in range(4):
        x = qk[:, j * LANES:(j + 1) * LANES]
        rot = jnp.where(first, pltpu.roll(x, LANES - 8, 1), pltpu.roll(x, 8, 1))
        y = x * cos + rot * sin
        if j < 2:
            y = y * qscale
        qk_ref[0, :, j * LANES:(j + 1) * LANES] = y.astype(BF16)
    v_ref[0] = jnp.dot(xm, w_ref[:, 2 * DA_W:Z_ATTN], preferred_element_type=F32).astype(BF16)
    gla_ref[0] = jnp.dot(xm, w_ref[:, Z_ATTN:Z_ATTN + Z_GLA], preferred_element_type=F32)
    rw_ref[0] = jnp.dot(xm, w_ref[:, Z_ATTN + Z_GLA:Z_ALL], preferred_element_type=F32)


def _inproj(h, mod1, w_p, cos_t, sin_t, n_ctx_tiles):
    bsz, lt, d = h.shape
    nt = lt // ROW_TILE
    row = lambda b_, i: (b_, i, 0)
    return pl.pallas_call(
        _inproj_kernel,
        out_shape=(jax.ShapeDtypeStruct((bsz, lt, 2 * DA_W), BF16),
                   jax.ShapeDtypeStruct((bsz, lt, DA_W), BF16),
                   jax.ShapeDtypeStruct((bsz, lt, Z_GLA), F32),
                   jax.ShapeDtypeStruct((bsz, lt, Z_RW), F32)),
        grid=(bsz, nt),
        in_specs=[pl.BlockSpec((1, ROW_TILE, d), row),
                  pl.BlockSpec((1, 1, 2, d),
                               lambda b_, i: (b_, jnp.where(i < n_ctx_tiles, 0, 1), 0, 0)),
                  pl.BlockSpec((d, Z_ALL), lambda b_, i: (0, 0)),
                  pl.BlockSpec((ROW_TILE, LANES), lambda b_, i: (i, 0)),
                  pl.BlockSpec((ROW_TILE, LANES), lambda b_, i: (i, 0))],
        out_specs=(pl.BlockSpec((1, ROW_TILE, 2 * DA_W), row),
                   pl.BlockSpec((1, ROW_TILE, DA_W), row),
                   pl.BlockSpec((1, ROW_TILE, Z_GLA), row),
                   pl.BlockSpec((1, ROW_TILE, Z_RW), row)),
        compiler_params=_cparams(("arbitrary", "arbitrary")),
        name="inproj",
    )(h, mod1, w_p, cos_t, sin_t)


def _pack_w_in(w):
    d = w.shape[0]
    o = 0
    parts = {}
    for name, n in (("da_q", 256), ("da_k", 256), ("da_v", 256), ("gla_q", GLA_KW), ("gla_k", GLA_KW),
                    ("gla_v", GLA_W), ("gla_gf", GLA_RANK), ("gla_gb", GLA_RANK), ("gla_r", GLA_W),
                    ("rw_rkv", 3 * RW_W), ("rw_wf", 64), ("rw_wb", 64), ("rw_af", 64), ("rw_ab", 64),
                    ("rw_g", 128)):
        parts[name] = w[:, o:o + n]
        o += n
    z = lambda n: jnp.zeros((d, n), w.dtype)
    cols = [parts["da_q"], parts["da_k"], parts["da_v"],
            parts["gla_q"], z(GLA_KP - GLA_KW), parts["gla_k"], z(GLA_KP - GLA_KW),
            parts["gla_v"], parts["gla_r"], parts["gla_gf"], parts["gla_gb"], z(LANES - 2 * GLA_RANK),
            parts["rw_rkv"], parts["rw_wf"], parts["rw_wb"], parts["rw_af"], parts["rw_ab"], parts["rw_g"]]
    return jnp.concatenate(cols, axis=1).astype(BF16)


def _rope_tables(n_ctx, seq):
    t = np.arange(seq)
    row = (t // GRID_W).astype(np.float32)
    col = (t % GRID_W).astype(np.float32)
    quarter = DA_QK // 4
    inv = (ROPE_BASE ** (-np.arange(quarter, dtype=np.float32) / quarter)).astype(np.float32)
    ang_r = row[:, None] * inv
    ang_c = col[:, None] * inv
    ang = np.concatenate([ang_r, ang_r, ang_c, ang_c], -1).astype(np.float32)
    cos = np.tile(np.cos(ang), (1, LANES // DA_QK))
    sin = np.tile(np.sin(ang), (1, LANES // DA_QK))
    sign = np.where((np.arange(LANES) % 16) < 8, -1.0, 1.0).astype(np.float32)
    cos = np.concatenate([np.ones((n_ctx, LANES), np.float32), cos], 0)
    sin = np.concatenate([np.zeros((n_ctx, LANES), np.float32), sin * sign], 0)
    return jnp.asarray(cos, F32), jnp.asarray(sin, F32)


ATTN_KEY_CHUNK = 1280


def _attn_kernel(lam_ref, q_ref, k_ref, v_ref, o_ref):
    q = q_ref[0]
    lam = lam_ref[0, 0]
    tq = q.shape[0]
    n_keys = k_ref.shape[1]
    lane = lax.broadcasted_iota(jnp.int32, (1, LANES), 1)
    zero = jnp.zeros_like(q)
    outs = []
    for hh in range(2):
        q2 = jnp.concatenate(
            [jnp.where((lane >= hh * 64 + m * DA_QK) & (lane < hh * 64 + (m + 1) * DA_QK), q, zero)
             for m in range(2)], axis=0)
        m_run = l_run = acc = None
        for k0 in range(0, n_keys, ATTN_KEY_CHUNK):
            k1 = min(k0 + ATTN_KEY_CHUNK, n_keys)
            s = _bdot_nt(q2, k_ref[0, k0:k1, :])
            m_blk = jnp.max(s, axis=-1, keepdims=True)
            m_new = m_blk if m_run is None else jnp.maximum(m_run, m_blk)
            p = jnp.exp((s - m_new).astype(BF16))
            v_ext = jnp.concatenate([v_ref[0, k0:k1, :], jnp.ones((k1 - k0, LANES), BF16)], axis=1)
            pv_ext = jnp.dot(p, v_ext, preferred_element_type=F32)
            pv, p_sum = pv_ext[:, 0:LANES], pv_ext[:, LANES:2 * LANES]
            if m_run is None:
                l_run, acc = p_sum, pv
            else:
                scale = jnp.exp(m_run - m_new)
                l_run = scale * l_run + p_sum
                acc = scale * acc + pv
            m_run = m_new
        o = acc / l_run
        outs.append(o[0:tq] - lam * o[tq:2 * tq])
    o_ref[0] = jnp.where(lane < 64, outs[0], outs[1])


def _attention(qk, v, lam, q_tile0, n_q_tiles, n_k_rows):
    bsz, lt, _ = qk.shape
    out = pl.pallas_call(
        _attn_kernel,
        out_shape=jax.ShapeDtypeStruct((bsz, n_q_tiles * ROW_TILE, DA_W), F32),
        grid=(bsz, 2, n_q_tiles),
        in_specs=[pl.BlockSpec(memory_space=pltpu.SMEM),
                  pl.BlockSpec((1, ROW_TILE, LANES), lambda b_, p, i: (b_, q_tile0 + i, p)),
                  pl.BlockSpec((1, n_k_rows, LANES), lambda b_, p, i: (b_, 0, 2 + p)),
                  pl.BlockSpec((1, n_k_rows, LANES), lambda b_, p, i: (b_, 0, p))],
        out_specs=pl.BlockSpec((1, ROW_TILE, LANES), lambda b_, p, i: (b_, i, p)),
        compiler_params=_cparams(("arbitrary", "arbitrary", "arbitrary")),
        name="diff_attn",
    )(lam.reshape(1, 1), qk, qk, v)
    return out


def _mixout_kernel(alpha, sub_scale,
                   ao_ref, gof_ref, gob_ref, zg_ref, yf_ref, yb_ref, bonus_ref, rg_ref, h_ref, mod_ref,
                   wo_ref, vec_ref, nrm_ref, g256_ref, g384_ref, rw_ref, rb_ref,
                   h1_ref, m_ref, ti_ref, gt_ref):
    a = ao_ref[0]
    a = a * lax.rsqrt(_group_sum(a * a, g256_ref[...]) * (1.0 / DA_V) + LN_EPS) * nrm_ref[0:1, 0:DA_W] * sub_scale
    o = gof_ref[0] + gob_ref[0]
    r = zg_ref[0, :, 2 * GLA_KP + GLA_W:2 * GLA_KP + 2 * GLA_W]
    gl = (o * lax.rsqrt(_group_sum(o * o, g384_ref[...]) * (1.0 / GLA_DV) + LN_EPS)
          * nrm_ref[1:2, :] * _silu(r))
    y = yf_ref[0] + yb_ref[0]
    mu = _group_sum(y, g384_ref[...]) * (1.0 / RW_D)
    yc = y - mu
    var = _group_sum(yc * yc, g384_ref[...]) * (1.0 / RW_D)
    yn = yc * lax.rsqrt(var + RW_GN_EPS) * nrm_ref[2:3, :] + nrm_ref[3:4, :]
    rw = (yn + bonus_ref[0]) * rg_ref[0]
    mix = (_dot(a, wo_ref[0:DA_W, :]) + _dot(gl, wo_ref[DA_W:DA_W + GLA_W, :])
           + _dot(rw, wo_ref[DA_W + GLA_W:MIX_W, :]))
    g1 = mod_ref[0, 0, 0:1, :]
    sc2 = mod_ref[0, 0, 1:2, :]
    sh2 = mod_ref[0, 0, 2:3, :]
    h1 = _ln(alpha * h_ref[0] + g1 * mix, vec_ref[0:1, :], vec_ref[1:2, :], LN_EPS)
    h1_ref[0] = h1
    m = h1 * (1.0 + sc2) + sh2
    for c in range(m.shape[1] // LANES):
        m_ref[0, pl.ds(c, ROW_TILE, stride=SUBLANES), :] = m[:, c * LANES:(c + 1) * LANES]
    logits = _dot_hi(m, rw_ref[...]) + rb_ref[...]
    lane = lax.broadcasted_iota(jnp.int32, logits.shape, 1)
    ti = jnp.zeros(logits.shape, jnp.int32)
    tv = jnp.full(logits.shape, -1e30, F32)
    for j in range(TOP_K):
        mx = jnp.max(logits, axis=-1, keepdims=True)
        idx = jnp.min(jnp.where(logits == mx, lane, LANES), axis=-1, keepdims=True)
        ti = jnp.where(lane == j, idx, ti)
        tv = jnp.where(lane == j, mx, tv)
        logits = jnp.where(lane == idx, -jnp.inf, logits)
    e = jnp.exp(tv - jnp.max(tv, axis=-1, keepdims=True))
    ti_ref[0] = ti
    gt_ref[0] = e / jnp.sum(e, axis=-1, keepdims=True)


def _mixout(alpha, sub_scale, tile0, n_tiles, n_ctx_tiles,
            ao, gof, gob, zg, yf, yb, bonus, rg, h, mod2, wo, vec, nrm, rw_p, rb_p):
    bsz, lt, d = h.shape
    row = lambda b_, i: (b_, tile0 + i, 0)
    orow = lambda b_, i: (b_, i, 0)
    full = lambda shape: pl.BlockSpec(shape, lambda b_, i: tuple(0 for _ in shape))
    nrows = n_tiles * ROW_TILE
    return pl.pallas_call(
        functools.partial(_mixout_kernel, alpha, sub_scale),
        out_shape=(jax.ShapeDtypeStruct((bsz, nrows, d), F32),
                   jax.ShapeDtypeStruct((bsz, nrows * (d // LANES), LANES), F32),
                   jax.ShapeDtypeStruct((bsz, nrows, LANES), jnp.int32),
                   jax.ShapeDtypeStruct((bsz, nrows, LANES), F32)),
        grid=(bsz, n_tiles),
        in_specs=[pl.BlockSpec((1, ROW_TILE, DA_W), orow),
                  pl.BlockSpec((1, ROW_TILE, GLA_W), row),
                  pl.BlockSpec((1, ROW_TILE, GLA_W), row),
                  pl.BlockSpec((1, ROW_TILE, Z_GLA), row),
                  pl.BlockSpec((1, ROW_TILE, RW_W), row),
                  pl.BlockSpec((1, ROW_TILE, RW_W), row),
                  pl.BlockSpec((1, ROW_TILE, RW_W), row),
                  pl.BlockSpec((1, ROW_TILE, RW_W), row),
                  pl.BlockSpec((1, ROW_TILE, d), row),
                  pl.BlockSpec((1, 1, 3, d),
                               lambda b_, i: (b_, jnp.where(tile0 + i < n_ctx_tiles, 0, 1), 0, 0)),
                  full((MIX_W, d)), full((2, d)), full((4, GLA_W)),
                  full((DA_W, DA_W)), full((GLA_W, GLA_W)), full((d, LANES)), full((1, LANES))],
        out_specs=(pl.BlockSpec((1, ROW_TILE, d), orow),
                   pl.BlockSpec((1, ROW_TILE * (d // LANES), LANES), orow),
                   pl.BlockSpec((1, ROW_TILE, LANES), orow),
                   pl.BlockSpec((1, ROW_TILE, LANES), orow)),
        compiler_params=_cparams(("arbitrary", "arbitrary")),
        name="mix_out",
    )(ao, gof, gob, zg, yf, yb, bonus, rg, h, mod2, wo, vec, nrm,
      _group_matrix(DA_W, DA_V), _group_matrix(GLA_W, GLA_DV), rw_p, rb_p)


def _bwd_chunk(step, n_ctx_chunks, n_chunks):
    return jnp.where(step < n_ctx_chunks, n_ctx_chunks - 1 - step, n_chunks + n_ctx_chunks - 1 - step)


def _tri(n, reverse, strict):
    t = lax.broadcasted_iota(jnp.int32, (n, n), 0)
    s = lax.broadcasted_iota(jnp.int32, (n, n), 1)
    if reverse:
        return (s > t) if strict else (s >= t)
    return (s < t) if strict else (s <= t)


def _log_sigmoid(x):
    return jnp.minimum(x, 0.0) - jnp.log1p(jnp.exp(-jnp.abs(x)))


def _softplus(x):
    return jnp.maximum(x, 0.0) + jnp.log1p(jnp.exp(-jnp.abs(x)))


def _split3(x):
    t0 = x.astype(BF16)
    r1 = x - t0.astype(F32)
    t1 = r1.astype(BF16)
    t2 = (r1 - t1.astype(F32)).astype(BF16)
    return t0, t1, t2


def _cumsum_rows(x, reverse):
    tri = jnp.where(_tri(CHUNK, reverse, False), 1.0, 0.0).astype(BF16)
    return sum(jnp.dot(tri, t, preferred_element_type=F32) for t in _split3(x))


def _bdot_nt(a, b):
    return lax.dot_general(a, b, (((1,), (1,)), ((), ())), preferred_element_type=F32)


GLA_SUB = 16
GLA_EXP_CLAMP = 60.0


def _gla_kernel(zf_ref, zb_ref, w2_ref, bias_ref, of_ref, ob_ref, st_ref):
    @pl.when(pl.program_id(1) == 0)
    def _():
        st_ref[...] = jnp.zeros_like(st_ref)

    c = CHUNK
    nsub = c // GLA_SUB
    nb = zf_ref.shape[0]
    qs, ks, vs, bs, b_lasts = [], [], [], [], []
    for i, reverse, z_ref in [(i, rv, zr) for i in range(nb) for rv, zr in ((False, zf_ref), (True, zb_ref))]:
        zg = z_ref[i]
        col0 = GLA_KP if reverse else 0
        gpre = _dot(zg[:, 2 * GLA_KP + 2 * GLA_W:Z_GLA], w2_ref[:, col0:col0 + GLA_KP]) \
            + bias_ref[:, col0:col0 + GLA_KP]
        b = _cumsum_rows(_log_sigmoid(gpre) * (1.0 / GLA_TAU), reverse)
        last = 0 if reverse else c - 1
        qs.append(zg[:, 0:GLA_KP] * (GLA_DK ** -0.5))
        ks.append(zg[:, GLA_KP:2 * GLA_KP])
        vs.append(zg[:, 2 * GLA_KP:2 * GLA_KP + GLA_W])
        bs.append(b)
        b_lasts.append(b[last:last + 1, :])
    q, k, v, b, b_last = (jnp.stack(x, axis=0) for x in (qs, ks, vs, bs, b_lasts))
    vb = v.astype(BF16)
    st = st_ref[...]
    inter = _bmm_nt((q * jnp.exp(b)).astype(BF16), st.astype(BF16))
    kv = _bmm_tn(vb, (k * jnp.exp(b_last - b)).astype(BF16))
    vi = lax.broadcasted_iota(jnp.int32, (GLA_W, GLA_KP), 0) // GLA_DV
    ki = lax.broadcasted_iota(jnp.int32, (GLA_W, GLA_KP), 1) // GLA_DK
    st_ref[...] = st * jnp.exp(b_last) + jnp.where((vi == ki)[None], kv, 0.0)

    lane_head = lax.broadcasted_iota(jnp.int32, (1, 1, GLA_KP), 2) // GLA_DK
    per_head = [[None] * nsub for _ in range(GLA_HEADS)]
    for i in range(nsub):
        r0 = i * GLA_SUB
        b_ref = jnp.stack([b[p, r0:r0 + 1] if p % 2 == 0 else b[p, r0 + GLA_SUB - 1:r0 + GLA_SUB]
                           for p in range(2 * nb)], axis=0)
        qi = q[:, r0:r0 + GLA_SUB] * jnp.exp(b[:, r0:r0 + GLA_SUB] - b_ref)
        ki_ = (k * jnp.exp(jnp.minimum(b_ref - b, GLA_EXP_CLAMP))).astype(BF16)
        qh = jnp.concatenate([jnp.where(lane_head == h, qi, 0.0) for h in range(GLA_HEADS)], axis=1)
        a = _bmm_nt(qh.astype(BF16), ki_)
        for h in range(GLA_HEADS):
            per_head[h][i] = a[:, h * GLA_SUB:(h + 1) * GLA_SUB]
    causal = jnp.stack([_tri(c, False, False), _tri(c, True, False)] * nb, axis=0)
    lane = lax.broadcasted_iota(jnp.int32, (1, 1, LANES), 2)
    pieces = []
    for p in range(GLA_HEADS // 2):
        vp = vb[:, :, p * LANES:(p + 1) * LANES]
        halves = []
        for hh in range(2):
            a_h = jnp.where(causal, jnp.concatenate(per_head[2 * p + hh], axis=1), 0.0)
            halves.append(_bmm(a_h.astype(BF16), vp))
        pieces.append(jnp.where(lane < GLA_DV, halves[0], halves[1]))
    o = inter + jnp.concatenate(pieces, axis=2)
    for i in range(nb):
        of_ref[i] = o[2 * i]
        ob_ref[i] = o[2 * i + 1]


def _gla_scan(zg, w2p, biasp, n_ctx_chunks):
    bsz, lt, _ = zg.shape
    nc = lt // CHUNK
    nb = SCAN_BATCH if bsz % SCAN_BATCH == 0 else 1
    fwd = lambda b_, s: (b_, s, 0)
    bwd = lambda b_, s: (b_, _bwd_chunk(s, n_ctx_chunks, nc), 0)
    return pl.pallas_call(
        _gla_kernel,
        out_shape=(jax.ShapeDtypeStruct((bsz, lt, GLA_W), F32),
                   jax.ShapeDtypeStruct((bsz, lt, GLA_W), F32)),
        grid=(bsz // nb, nc),
        in_specs=[pl.BlockSpec((nb, CHUNK, Z_GLA), fwd),
                  pl.BlockSpec((nb, CHUNK, Z_GLA), bwd),
                  pl.BlockSpec((LANES, 2 * GLA_KP), lambda b_, s: (0, 0)),
                  pl.BlockSpec((1, 2 * GLA_KP), lambda b_, s: (0, 0))],
        out_specs=(pl.BlockSpec((nb, CHUNK, GLA_W), fwd),
                   pl.BlockSpec((nb, CHUNK, GLA_W), bwd)),
        scratch_shapes=[pltpu.VMEM((2 * nb, GLA_W, GLA_KP), F32)],
        compiler_params=_cparams(("arbitrary", "arbitrary")),
        name="gla_scan",
    )(zg, zg, w2p, biasp)


def _pack_gla_gate(w2, bias):
    w = jnp.zeros((LANES, 2 * GLA_KP), F32)
    w = w.at[0:GLA_RANK, 0:GLA_KW].set(w2[0]).at[GLA_RANK:2 * GLA_RANK, GLA_KP:GLA_KP + GLA_KW].set(w2[1])
    b = jnp.zeros((1, 2 * GLA_KP), F32)
    b = b.at[0, 0:GLA_KW].set(bias[0]).at[0, GLA_KP:GLA_KP + GLA_KW].set(bias[1])
    return w, b


def _rwprep_kernel(n_ctx_tiles, n_tiles,
                   z_ref, zp_ref, zn_ref, cw_ref, w2_ref, a2_ref, g2_ref, vec_ref, gm_ref,
                   sh_ref, df_ref, db_ref, g_ref, bonus_ref):
    i = pl.program_id(1)
    z = z_ref[0]
    x = z[:, 0:3 * RW_W]
    seg_first = (i == 0) | (i == n_ctx_tiles)
    seg_last = (i == n_ctx_tiles - 1) | (i == n_tiles - 1)
    prev_row = jnp.where(seg_first, 0.0, zp_ref[0, 7:8, 0:3 * RW_W])
    next_row = jnp.where(seg_last, 0.0, zn_ref[0, 0:1, 0:3 * RW_W])
    ridx = lax.broadcasted_iota(jnp.int32, (ROW_TILE, 1), 0)
    x_prev = jnp.where(ridx == 0, prev_row, pltpu.roll(x, 1, 0))
    x_next = jnp.where(ridx == ROW_TILE - 1, next_row, pltpu.roll(x, ROW_TILE - 1, 0))
    xc = x_prev * cw_ref[0:1, :] + x * cw_ref[1:2, :] + x_next * cw_ref[2:3, :]
    r = xc[:, 0:RW_W]
    k = xc[:, RW_W:2 * RW_W]
    v = xc[:, 2 * RW_W:3 * RW_W]
    gm = gm_ref[...]
    kk = k * vec_ref[0:1, :]
    kk = kk / jnp.maximum(jnp.sqrt(_group_sum(kk * kk, gm)), 1e-12)
    k_a = vec_ref[1:2, :]
    r_k = vec_ref[2:3, :]
    w_raw = _dot_hi(jnp.tanh(z[:, 3 * RW_W:3 * RW_W + LANES]), w2_ref[...])
    a_raw = _dot_hi(z[:, 3 * RW_W + LANES:3 * RW_W + 2 * LANES], a2_ref[...])
    g_ref[0] = _dot_hi(jax.nn.sigmoid(z[:, 3 * RW_W + 2 * LANES:Z_RW]), g2_ref[...])
    sh_ref[0, :, 0:RW_W] = r
    sh_ref[0, :, RW_W:2 * RW_W] = v
    sh_ref[0, :, 2 * RW_W:3 * RW_W] = kk
    rk_sum = None
    for d, d_ref in enumerate((df_ref, db_ref)):
        wr = w_raw[:, d * RW_W:(d + 1) * RW_W] + vec_ref[3 + d:4 + d, :]
        logw = -jnp.exp(-_softplus(-wr) - 0.5)
        a = jax.nn.sigmoid(a_raw[:, d * RW_W:(d + 1) * RW_W] + vec_ref[5 + d:6 + d, :])
        k_mod = k * (1.0 + (a - 1.0) * k_a)
        d_ref[0, :, 0:RW_W] = logw
        d_ref[0, :, RW_W:2 * RW_W] = kk * a
        d_ref[0, :, 2 * RW_W:3 * RW_W] = k_mod
        s = _group_sum(r * k_mod * r_k, gm)
        rk_sum = s if d == 0 else rk_sum + s
    bonus_ref[0] = rk_sum * v


def _rw_prep(zr, cw, w2p, a2p, g2, vec, n_ctx_tiles):
    bsz, lt, _ = zr.shape
    nt = lt // ROW_TILE
    hb = ROW_TILE // 8
    row = lambda b_, i: (b_, i, 0)
    full = lambda shape: pl.BlockSpec(shape, lambda b_, i: tuple(0 for _ in shape))
    o3 = jax.ShapeDtypeStruct((bsz, lt, 3 * RW_W), F32)
    o1 = jax.ShapeDtypeStruct((bsz, lt, RW_W), F32)
    return pl.pallas_call(
        functools.partial(_rwprep_kernel, n_ctx_tiles, nt),
        out_shape=(o3, o3, o3, o1, o1),
        grid=(bsz, nt),
        in_specs=[pl.BlockSpec((1, ROW_TILE, Z_RW), row),
                  pl.BlockSpec((1, 8, Z_RW), lambda b_, i: (b_, jnp.maximum(i * hb - 1, 0), 0)),
                  pl.BlockSpec((1, 8, Z_RW), lambda b_, i: (b_, jnp.minimum((i + 1) * hb, nt * hb - 1), 0)),
                  full((3, 3 * RW_W)), full((LANES, 2 * RW_W)), full((LANES, 2 * RW_W)),
                  full((RW_GATE_RANK, RW_W)), full((8, RW_W)), full((RW_W, RW_W))],
        out_specs=(pl.BlockSpec((1, ROW_TILE, 3 * RW_W), row),
                   pl.BlockSpec((1, ROW_TILE, 3 * RW_W), row),
                   pl.BlockSpec((1, ROW_TILE, 3 * RW_W), row),
                   pl.BlockSpec((1, ROW_TILE, RW_W), row),
                   pl.BlockSpec((1, ROW_TILE, RW_W), row)),
        compiler_params=_cparams(("arbitrary", "arbitrary")),
        name="rwkv_prep",
    )(zr, zr, zr, cw, w2p, a2p, g2, vec, _group_matrix(RW_W, RW_D))


def _block2(w):
    r, n = w.shape[1:]
    z = jnp.zeros((r, n), w.dtype)
    return jnp.concatenate([jnp.concatenate([w[0], z], 1), jnp.concatenate([z, w[1]], 1)], 0)


def _bmm(a, b):
    return lax.dot_general(a, b, (((2,), (1,)), ((0,), (0,))), preferred_element_type=F32)


def _bmm_nt(a, b):
    return lax.dot_general(a, b, (((2,), (2,)), ((0,), (0,))), preferred_element_type=F32)


def _bmm_tn(a, b):
    return lax.dot_general(a, b, (((1,), (1,)), ((0,), (0,))), preferred_element_type=F32)


def _rw_operands(sh, dd, reverse):
    c = CHUNK
    r, v, kk = sh[:, 0:RW_W], sh[:, RW_W:2 * RW_W], sh[:, 2 * RW_W:3 * RW_W]
    logw, beta, k = dd[:, 0:RW_W], dd[:, RW_W:2 * RW_W], dd[:, 2 * RW_W:3 * RW_W]
    b = _cumsum_rows(logw, reverse)
    last = 0 if reverse else c - 1
    e_b = jnp.exp(b)
    e_nb = jnp.exp(-b)
    e_last = jnp.exp(b[last:last + 1, :])
    e_tot = e_last * e_nb
    abar = (kk * jnp.exp(b - logw)).astype(BF16)
    rbar = (r * e_b).astype(BF16)
    kt = (k * e_nb).astype(BF16)
    bt = (beta * e_nb).astype(BF16)
    khat = (k * e_tot).astype(BF16)
    nbhat = (-(beta * e_tot)).astype(BF16)
    vb = v.astype(BF16)
    head_a = lax.broadcasted_iota(jnp.int32, (1, LANES), 1) < RW_D
    zero = jnp.zeros((c, LANES), BF16)

    def stack(*xs):
        rows = []
        for x in xs:
            rows += [jnp.where(head_a, x, zero), jnp.where(head_a, zero, x)]
        return jnp.concatenate(rows, axis=0)

    out = []
    for p in range(RW_HEADS // 2):
        sl = slice(p * LANES, (p + 1) * LANES)
        out.append(dict(xar=stack(abar[:, sl], rbar[:, sl]), yb=stack(bt[:, sl]), yk=stack(kt[:, sl]),
                        vs=stack(vb[:, sl]), kb=stack(khat[:, sl], nbhat[:, sl]),
                        e_col=jnp.broadcast_to(e_last[:, sl], (LANES, LANES)).T))
    return out


def _rw_kernel(sf_ref, df_ref, sb_ref, db_ref, yf_ref, yb_ref, h_ref):
    @pl.when(pl.program_id(1) == 0)
    def _():
        h_ref[...] = jnp.zeros_like(h_ref)

    c = CHUNK
    c2 = 2 * c
    npair = RW_HEADS // 2
    nb = sf_ref.shape[0]
    ops = []
    for i in range(nb):
        ops += _rw_operands(sf_ref[i], df_ref[i], False) + _rw_operands(sb_ref[i], db_ref[i], True)
    cat = lambda name: jnp.stack([o[name] for o in ops], axis=0)
    xar, yb_, yk, vs, kb, e_col = (cat(n) for n in ("xar", "yb", "yk", "vs", "kb", "e_col"))

    ti = lax.broadcasted_iota(jnp.int32, (c2, c2), 0)
    si = lax.broadcasted_iota(jnp.int32, (c2, c2), 1)
    same_head = (ti // c) == (si // c)
    both = lambda fwd, bwd: jnp.concatenate([jnp.broadcast_to(fwd[None], (npair, c2, c2)),
                                             jnp.broadcast_to(bwd[None], (npair, c2, c2))] * nb, axis=0)
    strict = both(same_head & (si < ti), same_head & (si > ti))
    incl = both(same_head & (si <= ti), same_head & (si >= ti))
    eye = jnp.where(ti == si, 1.0, 0.0).astype(F32)

    gb = _bmm_nt(xar, yb_)
    gk = _bmm_nt(xar, yk)
    l_ab = jnp.where(strict, gb[:, 0:c2], 0.0)
    l_rb = jnp.where(incl, gb[:, c2:2 * c2], 0.0).astype(BF16)
    l_ak = jnp.where(strict, gk[:, 0:c2], 0.0)
    l_rk = jnp.where(incl, gk[:, c2:2 * c2], 0.0)
    t_inv = None
    s = 1
    while s < c:
        same = (ti // (2 * s)) == (si // (2 * s))
        lo, hi = (ti // s) % 2, (si // s) % 2
        off = both(same & (lo == 1) & (hi == 0), same & (lo == 0) & (hi == 1))
        l_off = jnp.where(off, l_ab, 0.0)
        if t_inv is None:
            t_inv = eye[None] - l_off
        else:
            tb = t_inv.astype(BF16)
            t_inv = t_inv - _bmm(tb, _bmm(l_off.astype(BF16), tb).astype(BF16))
        s *= 2
    h0 = h_ref[...]
    xh = _bmm(xar, h0.astype(BF16))
    lv = _bmm(jnp.concatenate([l_ak, l_rk], axis=1).astype(BF16), vs)
    ub = _bmm(t_inv.astype(BF16), (xh[:, 0:c2] + lv[:, 0:c2]).astype(BF16)).astype(BF16)
    y2 = xh[:, c2:2 * c2] + lv[:, c2:2 * c2] - _bmm(l_rb, ub)
    y = y2[:, 0:c] + y2[:, c:c2]
    h_ref[...] = e_col * h0 + _bmm_tn(kb, jnp.concatenate([vs, ub], axis=1))
    for i in range(nb):
        o = 2 * npair * i
        yf_ref[i] = jnp.concatenate([y[o + p] for p in range(npair)], axis=1)
        yb_ref[i] = jnp.concatenate([y[o + npair + p] for p in range(npair)], axis=1)


def _rw_scan(shared, dfw, dbw, n_ctx_chunks):
    bsz, lt, _ = shared.shape
    nc = lt // CHUNK
    nb = SCAN_BATCH if bsz % SCAN_BATCH == 0 else 1
    fwd = lambda b_, s: (b_, s, 0)
    bwd = lambda b_, s: (b_, _bwd_chunk(s, n_ctx_chunks, nc), 0)
    return pl.pallas_call(
        _rw_kernel,
        out_shape=(jax.ShapeDtypeStruct((bsz, lt, RW_W), F32),
                   jax.ShapeDtypeStruct((bsz, lt, RW_W), F32)),
        grid=(bsz // nb, nc),
        in_specs=[pl.BlockSpec((nb, CHUNK, 3 * RW_W), fwd),
                  pl.BlockSpec((nb, CHUNK, 3 * RW_W), fwd),
                  pl.BlockSpec((nb, CHUNK, 3 * RW_W), bwd),
                  pl.BlockSpec((nb, CHUNK, 3 * RW_W), bwd)],
        out_specs=(pl.BlockSpec((nb, CHUNK, RW_W), fwd),
                   pl.BlockSpec((nb, CHUNK, RW_W), bwd)),
        scratch_shapes=[pltpu.VMEM((nb * RW_HEADS, LANES, LANES), F32)],
        compiler_params=_cparams(("arbitrary", "arbitrary")),
        name="rwkv_scan",
    )(shared, dfw, shared, dbw)


def _rank_kernel(n, ti_ref, dest_ref, meta_ref, cnt_ref, run_ref, start_ref):
    ph = pl.program_id(0)
    i = pl.program_id(1)
    ti = ti_ref[...]
    lane = lax.broadcasted_iota(jnp.int32, (ROW_TILE, LANES), 1)
    ohs = [jnp.where(ti[:, j:j + 1] == lane, 1.0, 0.0).astype(F32) for j in range(TOP_K)]
    oh = ohs[0] + ohs[1] + ohs[2] + ohs[3]
    tile_cnt = jnp.sum(oh, axis=0, keepdims=True)

    @pl.when((ph == 0) & (i == 0))
    def _():
        cnt_ref[...] = jnp.zeros_like(cnt_ref)
        run_ref[...] = jnp.zeros_like(run_ref)

    @pl.when(ph == 0)
    def _():
        cnt_ref[...] += tile_cnt

    @pl.when((ph == 0) & (i == n - 1))
    def _():
        cnt = cnt_ref[...]
        shift = MOE_BLOCK.bit_length() - 1
        padded = jnp.left_shift(jnp.right_shift(cnt.astype(jnp.int32) + (MOE_BLOCK - 1), shift),
                                shift).astype(F32)
        e0 = lax.broadcasted_iota(jnp.int32, (LANES, LANES), 0)
        e1 = lax.broadcasted_iota(jnp.int32, (LANES, LANES), 1)
        before = jnp.where(e0 < e1, 1.0, 0.0).astype(F32)
        start = _dot_hi(jnp.broadcast_to(padded, (8, LANES)), before)[0:1]
        start_ref[...] = start
        meta_ref[0:1, :] = cnt
        meta_ref[1:2, :] = start
        meta_ref[2:3, :] = padded
        meta_ref[3:8, :] = jnp.zeros((5, LANES), F32)

    @pl.when(ph == 1)
    def _():
        t0 = lax.broadcasted_iota(jnp.int32, (ROW_TILE, ROW_TILE), 0)
        t1 = lax.broadcasted_iota(jnp.int32, (ROW_TILE, ROW_TILE), 1)
        earlier = jnp.where(t1 < t0, 1.0, 0.0).astype(BF16)
        pos = (jnp.dot(earlier, oh.astype(BF16), preferred_element_type=F32)
               + run_ref[...] + start_ref[...])
        dest = jnp.zeros((ROW_TILE, LANES), F32)
        for j in range(TOP_K):
            dj = jnp.sum(ohs[j] * pos, axis=-1, keepdims=True)
            dest = jnp.where(lane == j, dj, dest)
        dest_ref[0] = dest.T[0:8, :].astype(jnp.int32)
        run_ref[...] += tile_cnt


def _moe_rank(ti):
    n = ti.shape[0]
    nt = n // ROW_TILE
    return pl.pallas_call(
        functools.partial(_rank_kernel, nt),
        out_shape=(jax.ShapeDtypeStruct((nt, 8, ROW_TILE), jnp.int32),
                   jax.ShapeDtypeStruct((8, LANES), F32)),
        grid=(2, nt),
        in_specs=[pl.BlockSpec((ROW_TILE, LANES), lambda p, i: (i, 0))],
        out_specs=(pl.BlockSpec((1, 8, ROW_TILE), lambda p, i: (i * p, 0, 0)),
                   pl.BlockSpec((8, LANES), lambda p, i: (0, 0))),
        scratch_shapes=[pltpu.VMEM((1, LANES), F32), pltpu.VMEM((1, LANES), F32),
                        pltpu.VMEM((1, LANES), F32)],
        compiler_params=_cparams(("arbitrary", "arbitrary")),
        name="moe_rank",
    )(ti)


MOE_TOK_BITS = 15


def _invert_kernel(n_tiles, plane_rows, lo_ref, hi_ref, dest_hbm, code_ref, idx_s, sem):
    s = pl.program_id(0)

    def fetch(i, slot):
        return pltpu.make_async_copy(dest_hbm.at[i], idx_s.at[slot], sem.at[slot])

    @pl.when(s < N_EXPERTS)
    def _():
        def fill(p, c):
            code_ref[p] = jnp.left_shift(TOP_K * plane_rows + jnp.bitwise_and(p, 2 * MOE_BLOCK - 1), MOE_TOK_BITS)
            return c

        lax.fori_loop(lo_ref[s], hi_ref[s], fill, 0)

    @pl.when(s == N_EXPERTS)
    def _():
        fetch(0, 0).start()

    @pl.when(s >= N_EXPERTS)
    def _():
        i = s - N_EXPERTS
        slot = i % 2

        @pl.when(i + 1 < n_tiles)
        def _():
            fetch(i + 1, 1 - slot).start()

        fetch(i, slot).wait()

        def tok(t, c):
            token = i * ROW_TILE + t
            for j in range(TOP_K):
                code_ref[idx_s[slot, 0, j * ROW_TILE + t]] = jnp.bitwise_or(
                    jnp.left_shift(j * plane_rows + token, MOE_TOK_BITS), token)
            return c

        lax.fori_loop(0, ROW_TILE, tok, 0, unroll=4)


def _moe_invert(dest2d, n_blocks, plane_rows, pad_lo, pad_hi):
    nt = dest2d.shape[0]
    assert nt * ROW_TILE < 2 ** MOE_TOK_BITS and TOP_K * plane_rows + 2 * MOE_BLOCK <= 2 ** (32 - MOE_TOK_BITS)
    return pl.pallas_call(
        functools.partial(_invert_kernel, nt, plane_rows),
        out_shape=jax.ShapeDtypeStruct((n_blocks * MOE_BLOCK,), jnp.int32),
        grid_spec=pltpu.PrefetchScalarGridSpec(
            num_scalar_prefetch=2,
            grid=(N_EXPERTS + nt,),
            in_specs=[pl.BlockSpec(memory_space=pl.ANY)],
            out_specs=pl.BlockSpec(memory_space=pltpu.SMEM),
            scratch_shapes=[pltpu.SMEM((2, 1, 8 * ROW_TILE), jnp.int32), pltpu.SemaphoreType.DMA((2,))]),
        compiler_params=_cparams(("arbitrary",)),
        name="moe_invert",
    )(pad_lo, pad_hi, dest2d.reshape(nt, 1, 8 * ROW_TILE))


def _fused_expert_kernel(plane_rows,
                         be_ref, nu_ref, code_hbm, m_hbm, wg_ref, bg_ref, wu_ref, bu_ref, wd_ref, bd_ref,
                         out_hbm, wg_s, wu_s, wd_s, xbuf0, xbuf1, ybuf0, ybuf1, idx_s, sem_i, sem_g, sem_s):
    b = pl.program_id(0)
    nu = nu_ref[0]
    blk = MOE_BLOCK
    xbufs = (xbuf0, xbuf1)
    ybufs = (ybuf0, ybuf1)

    def idx_copy(block, slot):
        return pltpu.make_async_copy(code_hbm.at[block], idx_s.at[slot], sem_i.at[slot])

    def gather(slot, r, buf):
        tok = jnp.bitwise_and(idx_s[slot, 0, r], 2 ** MOE_TOK_BITS - 1)
        return pltpu.make_async_copy(m_hbm.at[tok], xbufs[buf].at[pl.ds(r * SUBLANES, SUBLANES), :], sem_g.at[buf])

    def scatter(slot, r, buf):
        row = lax.shift_right_logical(idx_s[slot, 0, r], MOE_TOK_BITS)
        return pltpu.make_async_copy(ybufs[buf].at[pl.ds(r * SUBLANES, SUBLANES), :], out_hbm.at[row], sem_s.at[buf])

    def wait_rows(buf, sem):
        pltpu.make_async_copy(m_hbm.at[pl.ds(0, blk)], xbufs[buf].reshape(blk, SUBLANES, LANES), sem.at[buf]).wait()

    def step(cur):
        oth = 1 - cur
        s_prev, s_cur, s_next, s_far = (b + 3) % 4, b % 4, (b + 1) % 4, (b + 2) % 4
        idx_copy(jnp.minimum(b + 2, nu - 1), s_far).start()
        idx_copy(0, s_next).wait()
        for r in range(blk):
            gather(s_next, r, oth).start()
            scatter(s_prev, r, oth).start()
        wait_rows(cur, sem_g)
        nch = wg_s.shape[0] // LANES
        x = jnp.concatenate([xbufs[cur][pl.ds(c, blk, stride=SUBLANES), :] for c in range(nch)],
                            axis=1).astype(BF16)
        gt = jnp.minimum(jnp.dot(x, wg_s[...], preferred_element_type=F32) + bg_ref[0, 0], SWIGLU_LIMIT)
        up = jnp.clip(jnp.dot(x, wu_s[...], preferred_element_type=F32) + bu_ref[0, 0],
                      -SWIGLU_LIMIT, SWIGLU_LIMIT)
        act = (up + 1.0) * gt * jax.nn.sigmoid(SWIGLU_ALPHA * gt)
        y = jnp.dot(act.astype(BF16), wd_s[...], preferred_element_type=F32) + bd_ref[0, 0]

        @pl.when(b > 0)
        def _():
            wait_rows(cur, sem_s)

        for c in range(nch):
            ybufs[cur][pl.ds(c, blk, stride=SUBLANES), :] = y[:, c * LANES:(c + 1) * LANES]

        @pl.when(b == nu - 1)
        def _():
            wait_rows(oth, sem_g)
            idx_copy(0, s_far).wait()
            wait_rows(oth, sem_s)

            def last(r, c):
                scatter(s_cur, r, cur).start()
                return c

            lax.fori_loop(0, blk, last, 0, unroll=8)
            wait_rows(cur, sem_s)

    @pl.when(b < nu)
    def _():
        e = be_ref[b]
        changed = (b == 0) | (e != be_ref[jnp.maximum(b - 1, 0)])

        @pl.when(changed)
        def _():
            wg_s[...] = wg_ref[0, 0].astype(BF16)
            wu_s[...] = wu_ref[0, 0].astype(BF16)
            wd_s[...] = wd_ref[0, 0].astype(BF16)

        @pl.when(b == 0)
        def _():
            ybuf0[...] = jnp.zeros_like(ybuf0)
            ybuf1[...] = jnp.zeros_like(ybuf1)
            cp = idx_copy(0, 0)
            cp.start()
            cp.wait()
            idx_copy(jnp.minimum(1, nu - 1), 1).start()

            def spare(r, c):
                pltpu.make_async_copy(ybuf0.at[pl.ds(r * SUBLANES, SUBLANES), :],
                                      out_hbm.at[TOP_K * plane_rows + r], sem_s.at[0]).start()
                idx_s[3, 0, r] = jnp.left_shift(TOP_K * plane_rows + blk + r, MOE_TOK_BITS)
                return c

            lax.fori_loop(0, blk, spare, 0, unroll=8)
            wait_rows(0, sem_s)

            def first(r, c):
                gather(0, r, 0).start()
                return c

            lax.fori_loop(0, blk, first, 0, unroll=8)

        @pl.when(b % 2 == 0)
        def _():
            step(0)

        @pl.when(b % 2 == 1)
        def _():
            step(1)


def _moe_experts_fused(layer, block_expert, n_used, codes, m, plane_rows, wg, bg, wu, bu, wd, bd):
    nch = m.shape[1]
    d = nch * LANES
    nl, ne, _, f = wg.shape
    nb = codes.shape[0]
    wmap = lambda b, be, nu: (layer, be[jnp.maximum(jnp.minimum(b, nu[0] - 1), 0)], 0, 0)
    any_spec = pl.BlockSpec(memory_space=pl.ANY)
    return pl.pallas_call(
        functools.partial(_fused_expert_kernel, plane_rows),
        out_shape=jax.ShapeDtypeStruct((TOP_K * plane_rows + 2 * MOE_BLOCK, nch, LANES), F32),
        grid_spec=pltpu.PrefetchScalarGridSpec(
            num_scalar_prefetch=2,
            grid=(nb,),
            in_specs=[any_spec, any_spec,
                      pl.BlockSpec((1, 1, d, f), wmap), pl.BlockSpec((1, 1, 1, f), wmap),
                      pl.BlockSpec((1, 1, d, f), wmap), pl.BlockSpec((1, 1, 1, f), wmap),
                      pl.BlockSpec((1, 1, f, d), wmap), pl.BlockSpec((1, 1, 1, d), wmap)],
            out_specs=any_spec,
            scratch_shapes=[pltpu.VMEM((d, f), BF16), pltpu.VMEM((d, f), BF16), pltpu.VMEM((f, d), BF16),
                            pltpu.VMEM((MOE_BLOCK * nch, LANES), F32), pltpu.VMEM((MOE_BLOCK * nch, LANES), F32),
                            pltpu.VMEM((MOE_BLOCK * nch, LANES), F32), pltpu.VMEM((MOE_BLOCK * nch, LANES), F32),
                            pltpu.SMEM((4, 1, MOE_BLOCK), jnp.int32),
                            pltpu.SemaphoreType.DMA((4,)), pltpu.SemaphoreType.DMA((2,)),
                            pltpu.SemaphoreType.DMA((2,))]),
        compiler_params=_cparams(("arbitrary",)),
        name="moe_experts",
    )(block_expert, n_used, codes.reshape(nb, 1, MOE_BLOCK), m, wg, bg.reshape(nl, ne, 1, f),
      wu, bu.reshape(nl, ne, 1, f), wd, bd.reshape(nl, ne, 1, d))


def _combine2_kernel(alpha, gt_ref, h_ref, mod_ref, vec_ref, y0_ref, y1_ref, y2_ref, y3_ref, o_ref):
    gt = gt_ref[0]
    d = h_ref.shape[2]
    xs = []
    for c in range(d // LANES):
        rows = pl.ds(c, ROW_TILE, stride=SUBLANES)
        f = (gt[:, 0:1] * y0_ref[rows, :] + gt[:, 1:2] * y1_ref[rows, :]
             + gt[:, 2:3] * y2_ref[rows, :] + gt[:, 3:4] * y3_ref[rows, :])
        cols = slice(c * LANES, (c + 1) * LANES)
        xs.append(alpha * h_ref[0, :, cols] + mod_ref[0, 0, 0:1, cols] * f)
    mu = sum(jnp.sum(x, axis=-1, keepdims=True) for x in xs) * (1.0 / d)
    xs = [x - mu for x in xs]
    var = sum(jnp.sum(x * x, axis=-1, keepdims=True) for x in xs) * (1.0 / d)
    inv = lax.rsqrt(var + LN_EPS)
    for c, x in enumerate(xs):
        cols = slice(c * LANES, (c + 1) * LANES)
        o_ref[0, :, cols] = x * inv * vec_ref[0:1, cols] + vec_ref[1:2, cols]


def _moe_combine2(alpha, n_ctx_tiles, tile0, plane_rows, gates, h1, mod3, vec, out4):
    bsz, rows, d = h1.shape
    nch = d // LANES
    nt = rows // ROW_TILE
    pt = plane_rows // ROW_TILE
    row = lambda b_, i: (b_, i, 0)
    plane = lambda j: pl.BlockSpec((ROW_TILE * nch, LANES), lambda b_, i: (j * pt + b_ * nt + i, 0))
    out4 = out4.reshape((TOP_K * plane_rows + 2 * MOE_BLOCK) * nch, LANES)
    return pl.pallas_call(
        functools.partial(_combine2_kernel, alpha),
        out_shape=jax.ShapeDtypeStruct((bsz, rows, d), F32),
        grid=(bsz, nt),
        in_specs=[pl.BlockSpec((1, ROW_TILE, LANES), row),
                  pl.BlockSpec((1, ROW_TILE, d), row),
                  pl.BlockSpec((1, 1, 1, d), lambda b_, i: (b_, jnp.where(tile0 + i < n_ctx_tiles, 0, 1), 0, 0)),
                  pl.BlockSpec((2, d), lambda b_, i: (0, 0)),
                  plane(0), plane(1), plane(2), plane(3)],
        out_specs=pl.BlockSpec((1, ROW_TILE, d), row),
        compiler_params=_cparams(("arbitrary", "arbitrary")),
        name="moe_combine",
    )(gates, h1, mod3, vec, out4, out4, out4, out4)


def _moe(layer, alpha, n_ctx_tiles, tile0, h1, m, ti, gates, mod3, ln2, wg, bg, wu, bu, wd, bd):
    bsz, rows, d = h1.shape
    n = bsz * rows
    dest, meta = _moe_rank(ti.reshape(n, LANES))
    dest2d = dest.reshape(n // ROW_TILE, 8 * ROW_TILE)
    n_blocks = -(-(n * TOP_K) // MOE_BLOCK) + N_EXPERTS
    pad_end = (meta[1, :N_EXPERTS] + meta[2, :N_EXPERTS]).astype(jnp.int32)
    block_row = jnp.arange(n_blocks, dtype=jnp.int32) * MOE_BLOCK
    block_expert = jnp.minimum(jnp.sum((pad_end[None, :] <= block_row[:, None]).astype(jnp.int32), axis=1),
                               N_EXPERTS - 1)
    n_used = (pad_end[-1:] // MOE_BLOCK).astype(jnp.int32)
    plane_rows = n
    pad_lo = (meta[1, :N_EXPERTS] + meta[0, :N_EXPERTS]).astype(jnp.int32)
    pad_hi = jnp.concatenate([pad_end[:-1], jnp.full((1,), n_blocks * MOE_BLOCK, jnp.int32)])
    codes = _moe_invert(dest2d, n_blocks, plane_rows, pad_lo, pad_hi).reshape(n_blocks, MOE_BLOCK)
    out4 = _moe_experts_fused(layer, block_expert, n_used, codes, m.reshape(n, d // LANES, LANES), plane_rows,
                              wg, bg, wu, bu, wd, bd)
    return _moe_combine2(alpha, n_ctx_tiles, tile0, plane_rows, gates, h1, mod3, ln2, out4)


def kernel(x, c, ctx, c_ctx, ln_in_g, ln_in_b, ada_w, ada_b, w_in, lam_q1, lam_k1, lam_q2, lam_k2,
           da_subln_g, gla_gate_w2, gla_gate_b, gla_norm_g, rw_conv_w, rw_w2, rw_w0, rw_a2, rw_a0,
           rw_g2, rw_k_k, rw_k_a, rw_r_k, rw_lnx_g, rw_lnx_b, w_out, ln1_g, ln1_b, router_w, router_b,
           moe_w_gate, moe_b_gate, moe_w_up, moe_b_up, moe_w_down, moe_b_down, ln2_g, ln2_b):
    bsz, seq, d = x.shape
    n_ctx = ctx.shape[1]
    depth = w_in.shape[0]
    assert n_ctx % ROW_TILE == 0 and seq % ROW_TILE == 0 and seq % GRID_W == 0
    assert w_in.shape[2] == 3488 and bsz + 1 <= 8
    nct = n_ctx // ROW_TILE
    ncc = n_ctx // CHUNK
    nt = (n_ctx + seq) // ROW_TILE
    alpha = (2 * depth) ** 0.25

    c_all = jnp.concatenate([c, c_ctx[None], jnp.zeros((8 - bsz - 1, d), F32)], axis=0)
    mods = _ada_mod(c_all, ada_w, ada_b).reshape(depth, 8, 6, d)
    h = _ln_in(ctx, x, ln_in_g, ln_in_b)
    cos_t, sin_t = _rope_tables(n_ctx, seq)

    def pick(l, idx):
        mc = jnp.broadcast_to(mods[l, bsz][None, idx], (bsz, len(idx), d))
        return jnp.stack([mc, mods[l, :bsz][:, idx]], axis=1)

    for l in range(depth):
        last = l == depth - 1
        tile0 = nct if last else 0
        n_out_tiles = nt - tile0
        lam_init = 0.8 - 0.6 * math.exp(-0.3 * l)
        lam = (jnp.exp(jnp.sum(lam_q1[l] * lam_k1[l])) - jnp.exp(jnp.sum(lam_q2[l] * lam_k2[l])) + lam_init)

        qk, v, zg, zr = _inproj(h, pick(l, [1, 0]), _pack_w_in(w_in[l]), cos_t, sin_t, nct)
        ao = _attention(qk, v, lam, nct, nt - nct, n_ctx + seq)
        if not last:
            ao = jnp.concatenate([_attention(qk, v, lam, 0, nct, n_ctx), ao], axis=1)
        w2p, biasp = _pack_gla_gate(gla_gate_w2[l], gla_gate_b[l])
        gof, gob = _gla_scan(zg, w2p, biasp, ncc)
        rvec = jnp.stack([rw_k_k[l], rw_k_a[l], rw_r_k[l].reshape(-1), rw_w0[l, 0], rw_w0[l, 1],
                          rw_a0[l, 0], rw_a0[l, 1], jnp.zeros((RW_W,), F32)], axis=0)
        shared, dfw, dbw, rg, bonus = _rw_prep(zr, rw_conv_w[l], _block2(rw_w2[l]), _block2(rw_a2[l]),
                                               rw_g2[l], rvec, nct)
        yf, yb = _rw_scan(shared, dfw, dbw, ncc)

        nrm = jnp.stack([jnp.pad(jnp.tile(da_subln_g[l], DA_HEADS), (0, GLA_W - DA_W)),
                         jnp.tile(gla_norm_g[l], GLA_HEADS), rw_lnx_g[l], rw_lnx_b[l]], axis=0)
        rw_p = jnp.pad(router_w[l], ((0, 0), (0, LANES - N_EXPERTS)))
        rb_p = jnp.pad(router_b[l], (0, LANES - N_EXPERTS), constant_values=-1e30).reshape(1, LANES)
        h1, m, ti, gates = _mixout(alpha, 1.0 - lam_init, tile0, n_out_tiles, nct,
                                   ao, gof, gob, zg, yf, yb, bonus, rg, h, pick(l, [2, 4, 3]),
                                   w_out[l].astype(BF16), jnp.stack([ln1_g[l], ln1_b[l]], 0), nrm, rw_p, rb_p)
        h = _moe(l, alpha, nct, tile0, h1, m, ti, gates, pick(l, [5]), jnp.stack([ln2_g[l], ln2_b[l]], 0),
                 moe_w_gate, moe_b_gate, moe_w_up, moe_b_up, moe_w_down, moe_b_down)
    return h
```

```python
import functools
import math

import jax
import jax.numpy as jnp
import numpy as np
from jax import lax
from jax.experimental import pallas as pl
from jax.experimental.pallas import tpu as pltpu

F32 = jnp.float32
BF16 = jnp.bfloat16
HI = lax.Precision.HIGHEST

GRID_W = 64
DA_HEADS, DA_QK, DA_V = 4, 32, 64
GLA_HEADS, GLA_DK, GLA_DV, GLA_RANK, GLA_TAU = 6, 32, 64, 16, 16.0
RW_HEADS, RW_D, RW_DECAY_RANK, RW_A_RANK, RW_GATE_RANK = 6, 64, 64, 64, 128
RW_GN_EPS = 64e-5
N_EXPERTS, TOP_K = 32, 4
SWIGLU_LIMIT, SWIGLU_ALPHA = 7.0, 1.702
ROPE_BASE = 10000.0
LN_EPS = 1e-5

DA_W = DA_HEADS * DA_V
GLA_KW = GLA_HEADS * GLA_DK
GLA_W = GLA_HEADS * GLA_DV
RW_W = RW_HEADS * RW_D
MIX_W = DA_W + GLA_W + RW_W

LANES = 128
ROW_TILE = 256
CHUNK = 64
SCAN_BATCH = 4
MOE_BLOCK = 256
VMEM_LIMIT = 56 * 1024 * 1024
SUBLANES = 8

Z_ATTN = 3 * DA_W
GLA_KP = 256
Z_GLA = 2 * GLA_KP + 2 * GLA_W + LANES
Z_RW = 3 * RW_W + 3 * LANES
Z_ALL = Z_ATTN + Z_GLA + Z_RW


def _cparams(sem):
    return pltpu.CompilerParams(dimension_semantics=sem, vmem_limit_bytes=VMEM_LIMIT)


def _ln(x, g, b, eps):
    mu = jnp.mean(x, axis=-1, keepdims=True)
    xc = x - mu
    var = jnp.mean(xc * xc, axis=-1, keepdims=True)
    return xc * lax.rsqrt(var + eps) * g + b


def _silu(x):
    return x * jax.nn.sigmoid(x)


def _dot(a, b):
    return jnp.dot(a.astype(BF16), b.astype(BF16), preferred_element_type=F32)


def _dot_hi(a, b):
    return jnp.dot(a, b, precision=HI, preferred_element_type=F32)


def _split2(x):
    hi = x.astype(BF16)
    return hi, (x - hi.astype(F32)).astype(BF16)


def _dot3(a, b):
    a_hi, a_lo = _split2(a)
    b_hi, b_lo = _split2(b)
    return (jnp.dot(a_hi, b_hi, preferred_element_type=F32) + jnp.dot(a_hi, b_lo, preferred_element_type=F32)
            + jnp.dot(a_lo, b_hi, preferred_element_type=F32))


def _group_sum(x, gmat):
    hi = x.astype(BF16)
    lo = (x - hi.astype(F32)).astype(BF16)
    return (jnp.dot(hi, gmat, preferred_element_type=F32)
            + jnp.dot(lo, gmat, preferred_element_type=F32))


def _group_matrix(width, group):
    idx = np.arange(width) // group
    return jnp.asarray((idx[:, None] == idx[None, :]).astype(np.float32), dtype=BF16)


def _ada_kernel(c_ref, w_ref, b_ref, o_ref):
    o_ref[0] = _dot_hi(_silu(c_ref[...]), w_ref[0]) + b_ref[0]


def _ada_mod(c_all, ada_w, ada_b):
    nl, d, n6 = ada_w.shape
    tn = 1536
    return pl.pallas_call(
        _ada_kernel,
        out_shape=jax.ShapeDtypeStruct((nl, c_all.shape[0], n6), F32),
        grid=(nl, n6 // tn),
        in_specs=[pl.BlockSpec((c_all.shape[0], d), lambda l, j: (0, 0)),
                  pl.BlockSpec((1, d, tn), lambda l, j: (l, 0, j)),
                  pl.BlockSpec((1, 1, tn), lambda l, j: (l, 0, j))],
        out_specs=pl.BlockSpec((1, c_all.shape[0], tn), lambda l, j: (l, 0, j)),
        compiler_params=_cparams(("arbitrary", "arbitrary")),
        name="ada_mod",
    )(c_all, ada_w, ada_b.reshape(nl, 1, n6))


def _ln_in_kernel(n_ctx_tiles, c_ref, x_ref, g_ref, b_ref, o_ref):
    @pl.when(pl.program_id(1) < n_ctx_tiles)
    def _():
        o_ref[0] = _ln(c_ref[0], g_ref[...], b_ref[...], LN_EPS)

    @pl.when(pl.program_id(1) >= n_ctx_tiles)
    def _():
        o_ref[0] = _ln(x_ref[0], g_ref[...], b_ref[...], LN_EPS)


def _ln_in(ctx, x, g, b):
    bsz, n_ctx, d = ctx.shape
    nct = n_ctx // ROW_TILE
    nt = nct + x.shape[1] // ROW_TILE
    return pl.pallas_call(
        functools.partial(_ln_in_kernel, nct),
        out_shape=jax.ShapeDtypeStruct((bsz, nt * ROW_TILE, d), F32),
        grid=(bsz, nt),
        in_specs=[pl.BlockSpec((1, ROW_TILE, d), lambda b_, i: (b_, jnp.minimum(i, nct - 1), 0)),
                  pl.BlockSpec((1, ROW_TILE, d), lambda b_, i: (b_, jnp.maximum(i - nct, 0), 0)),
                  pl.BlockSpec((1, d), lambda b_, i: (0, 0)),
                  pl.BlockSpec((1, d), lambda b_, i: (0, 0))],
        out_specs=pl.BlockSpec((1, ROW_TILE, d), lambda b_, i: (b_, i, 0)),
        compiler_params=_cparams(("arbitrary", "arbitrary")),
        name="ln_in",
    )(ctx, x, g.reshape(1, d), b.reshape(1, d))


def _inproj_kernel(h_ref, mod_ref, w_ref, cos_ref, sin_ref, qk_ref, v_ref, gla_ref, rw_ref):
    h = h_ref[0]
    sc = mod_ref[0, 0, 0:1, :]
    sh = mod_ref[0, 0, 1:2, :]
    xm = (h * (1.0 + sc) + sh).astype(BF16)
    qk = jnp.dot(xm, w_ref[:, 0:2 * DA_W], preferred_element_type=F32)
    cos = cos_ref[...]
    sin = sin_ref[...]
    lane = lax.broadcasted_iota(jnp.int32, (1, LANES), 1)
    first = (lane % 16) < 8
    qscale = DA_QK ** -0.5
    for j in range(4):
        x = qk[:, j * LANES:(j + 1) * LANES]
        rot = jnp.where(first, pltpu.roll(x, LANES - 8, 1), pltpu.roll(x, 8, 1))
        y = x * cos + rot * sin
        if j < 2:
            y = y * qscale
        qk_ref[0, :, j * LANES:(j + 1) * LANES] = y.astype(BF16)
    v_ref[0] = jnp.dot(xm, w_ref[:, 2 * DA_W:Z_ATTN], preferred_element_type=F32).astype(BF16)
    gla_ref[0] = jnp.dot(xm, w_ref[:, Z_ATTN:Z_ATTN + Z_GLA], preferred_element_type=F32)
    rw_ref[0] = jnp.dot(xm, w_ref[:, Z_ATTN + Z_GLA:Z_ALL], preferred_element_type=F32)


def _inproj(h, mod1, w_p, cos_t, sin_t, n_ctx_tiles):
    bsz, lt, d = h.shape
    nt = lt // ROW_TILE
    row = lambda b_, i: (b_, i, 0)
    return pl.pallas_call(
        _inproj_kernel,
        out_shape=(jax.ShapeDtypeStruct((bsz, lt, 2 * DA_W), BF16),
                   jax.ShapeDtypeStruct((bsz, lt, DA_W), BF16),
                   jax.ShapeDtypeStruct((bsz, lt, Z_GLA), F32),
                   jax.ShapeDtypeStruct((bsz, lt, Z_RW), F32)),
        grid=(bsz, nt),
        in_specs=[pl.BlockSpec((1, ROW_TILE, d), row),
                  pl.BlockSpec((1, 1, 2, d),
                               lambda b_, i: (b_, jnp.where(i < n_ctx_tiles, 0, 1), 0, 0)),
                  pl.BlockSpec((d, Z_ALL), lambda b_, i: (0, 0)),
                  pl.BlockSpec((ROW_TILE, LANES), lambda b_, i: (i, 0)),
                  pl.BlockSpec((ROW_TILE, LANES), lambda b_, i: (i, 0))],
        out_specs=(pl.BlockSpec((1, ROW_TILE, 2 * DA_W), row),
                   pl.BlockSpec((1, ROW_TILE, DA_W), row),
                   pl.BlockSpec((1, ROW_TILE, Z_GLA), row),
                   pl.BlockSpec((1, ROW_TILE, Z_RW), row)),
        compiler_params=_cparams(("arbitrary", "arbitrary")),
        name="inproj",
    )(h, mod1, w_p, cos_t, sin_t)


def _pack_w_in(w):
    d = w.shape[0]
    o = 0
    parts = {}
    for name, n in (("da_q", 256), ("da_k", 256), ("da_v", 256), ("gla_q", GLA_KW), ("gla_k", GLA_KW),
                    ("gla_v", GLA_W), ("gla_gf", GLA_RANK), ("gla_gb", GLA_RANK), ("gla_r", GLA_W),
                    ("rw_rkv", 3 * RW_W), ("rw_wf", 64), ("rw_wb", 64), ("rw_af", 64), ("rw_ab", 64),
                    ("rw_g", 128)):
        parts[name] = w[:, o:o + n]
        o += n
    z = lambda n: jnp.zeros((d, n), w.dtype)
    cols = [parts["da_q"], parts["da_k"], parts["da_v"],
            parts["gla_q"], z(GLA_KP - GLA_KW), parts["gla_k"], z(GLA_KP - GLA_KW),
            parts["gla_v"], parts["gla_r"], parts["gla_gf"], parts["gla_gb"], z(LANES - 2 * GLA_RANK),
            parts["rw_rkv"], parts["rw_wf"], parts["rw_wb"], parts["rw_af"], parts["rw_ab"], parts["rw_g"]]
    return jnp.concatenate(cols, axis=1).astype(BF16)


def _rope_tables(n_ctx, seq):
    t = np.arange(seq)
    row = (t // GRID_W).astype(np.float32)
    col = (t % GRID_W).astype(np.float32)
    quarter = DA_QK // 4
    inv = (ROPE_BASE ** (-np.arange(quarter, dtype=np.float32) / quarter)).astype(np.float32)
    ang_r = row[:, None] * inv
    ang_c = col[:, None] * inv
    ang = np.concatenate([ang_r, ang_r, ang_c, ang_c], -1).astype(np.float32)
    cos = np.tile(np.cos(ang), (1, LANES // DA_QK))
    sin = np.tile(np.sin(ang), (1, LANES // DA_QK))
    sign = np.where((np.arange(LANES) % 16) < 8, -1.0, 1.0).astype(np.float32)
    cos = np.concatenate([np.ones((n_ctx, LANES), np.float32), cos], 0)
    sin = np.concatenate([np.zeros((n_ctx, LANES), np.float32), sin * sign], 0)
    return jnp.asarray(cos, F32), jnp.asarray(sin, F32)


ATTN_KEY_CHUNK = 1280


def _attn_kernel(lam_ref, q_ref, k_ref, v_ref, o_ref):
    q = q_ref[0]
    lam = lam_ref[0, 0]
    tq = q.shape[0]
    n_keys = k_ref.shape[1]
    lane = lax.broadcasted_iota(jnp.int32, (1, LANES), 1)
    zero = jnp.zeros_like(q)
    outs = []
    for hh in range(2):
        q2 = jnp.concatenate(
            [jnp.where((lane >= hh * 64 + m * DA_QK) & (lane < hh * 64 + (m + 1) * DA_QK), q, zero)
             for m in range(2)], axis=0)
        m_run = l_run = acc = None
        for k0 in range(0, n_keys, ATTN_KEY_CHUNK):
            k1 = min(k0 + ATTN_KEY_CHUNK, n_keys)
            s = _bdot_nt(q2, k_ref[0, k0:k1, :])
            m_blk = jnp.max(s, axis=-1, keepdims=True)
            m_new = m_blk if m_run is None else jnp.maximum(m_run, m_blk)
            p = jnp.exp((s - m_new).astype(BF16))
            v_ext = jnp.concatenate([v_ref[0, k0:k1, :], jnp.ones((k1 - k0, LANES), BF16)], axis=1)
            pv_ext = jnp.dot(p, v_ext, preferred_element_type=F32)
            pv, p_sum = pv_ext[:, 0:LANES], pv_ext[:, LANES:2 * LANES]
            if m_run is None:
                l_run, acc = p_sum, pv
            else:
                scale = jnp.exp(m_run - m_new)
                l_run = scale * l_run + p_sum
                acc = scale * acc + pv
            m_run = m_new
        o = acc / l_run
        outs.append(o[0:tq] - lam * o[tq:2 * tq])
    o_ref[0] = jnp.where(lane < 64, outs[0], outs[1])


def _attention(qk, v, lam, q_tile0, n_q_tiles, n_k_rows):
    bsz, lt, _ = qk.shape
    out = pl.pallas_call(
        _attn_kernel,
        out_shape=jax.ShapeDtypeStruct((bsz, n_q_tiles * ROW_TILE, DA_W), F32),
        grid=(bsz, 2, n_q_tiles),
        in_specs=[pl.BlockSpec(memory_space=pltpu.SMEM),
                  pl.BlockSpec((1, ROW_TILE, LANES), lambda b_, p, i: (b_, q_tile0 + i, p)),
                  pl.BlockSpec((1, n_k_rows, LANES), lambda b_, p, i: (b_, 0, 2 + p)),
                  pl.BlockSpec((1, n_k_rows, LANES), lambda b_, p, i: (b_, 0, p))],
        out_specs=pl.BlockSpec((1, ROW_TILE, LANES), lambda b_, p, i: (b_, i, p)),
        compiler_params=_cparams(("arbitrary", "arbitrary", "arbitrary")),
        name="diff_attn",
    )(lam.reshape(1, 1), qk, qk, v)
    return out


def _mixout_kernel(alpha, sub_scale,
                   ao_ref, gof_ref, gob_ref, zg_ref, yf_ref, yb_ref, bonus_ref, rg_ref, h_ref, mod_ref,
                   wo_ref, vec_ref, nrm_ref, g256_ref, g384_ref, rw_ref, rb_ref,
                   h1_ref, m_ref, ti_ref, gt_ref):
    a = ao_ref[0]
    a = a * lax.rsqrt(_group_sum(a * a, g256_ref[...]) * (1.0 / DA_V) + LN_EPS) * nrm_ref[0:1, 0:DA_W] * sub_scale
    o = gof_ref[0] + gob_ref[0]
    r = zg_ref[0, :, 2 * GLA_KP + GLA_W:2 * GLA_KP + 2 * GLA_W]
    gl = (o * lax.rsqrt(_group_sum(o * o, g384_ref[...]) * (1.0 / GLA_DV) + LN_EPS)
          * nrm_ref[1:2, :] * _silu(r))
    y = yf_ref[0] + yb_ref[0]
    mu = _group_sum(y, g384_ref[...]) * (1.0 / RW_D)
    yc = y - mu
    var = _group_sum(yc * yc, g384_ref[...]) * (1.0 / RW_D)
    yn = yc * lax.rsqrt(var + RW_GN_EPS) * nrm_ref[2:3, :] + nrm_ref[3:4, :]
    rw = (yn + bonus_ref[0]) * rg_ref[0]
    mix = (_dot(a, wo_ref[0:DA_W, :]) + _dot(gl, wo_ref[DA_W:DA_W + GLA_W, :])
           + _dot(rw, wo_ref[DA_W + GLA_W:MIX_W, :]))
    g1 = mod_ref[0, 0, 0:1, :]
    sc2 = mod_ref[0, 0, 1:2, :]
    sh2 = mod_ref[0, 0, 2:3, :]
    h1 = _ln(alpha * h_ref[0] + g1 * mix, vec_ref[0:1, :], vec_ref[1:2, :], LN_EPS)
    h1_ref[0] = h1
    m = h1 * (1.0 + sc2) + sh2
    for c in range(m.shape[1] // LANES):
        m_ref[0, pl.ds(c, ROW_TILE, stride=SUBLANES), :] = m[:, c * LANES:(c + 1) * LANES]
    m_hi, m_lo = _split2(m)
    logits = (jnp.dot(m_hi, rw_ref[0], preferred_element_type=F32)
              + jnp.dot(m_hi, rw_ref[1], preferred_element_type=F32)
              + jnp.dot(m_lo, rw_ref[0], preferred_element_type=F32)) + rb_ref[...]
    lane = lax.broadcasted_iota(jnp.int32, logits.shape, 1)
    ti = jnp.zeros(logits.shape, jnp.int32)
    tv = jnp.full(logits.shape, -1e30, F32)
    for j in range(TOP_K):
        mx = jnp.max(logits, axis=-1, keepdims=True)
        idx = jnp.min(jnp.where(logits == mx, lane, LANES), axis=-1, keepdims=True)
        ti = jnp.where(lane == j, idx, ti)
        tv = jnp.where(lane == j, mx, tv)
        logits = jnp.where(lane == idx, -jnp.inf, logits)
    e = jnp.exp(tv - jnp.max(tv, axis=-1, keepdims=True))
    ti_ref[0] = ti
    gt_ref[0] = e / jnp.sum(e, axis=-1, keepdims=True)


def _mixout(alpha, sub_scale, tile0, n_tiles, n_ctx_tiles,
            ao, gof, gob, zg, yf, yb, bonus, rg, h, mod2, wo, vec, nrm, rw_p, rb_p):
    bsz, lt, d = h.shape
    row = lambda b_, i: (b_, tile0 + i, 0)
    orow = lambda b_, i: (b_, i, 0)
    full = lambda shape: pl.BlockSpec(shape, lambda b_, i: tuple(0 for _ in shape))
    nrows = n_tiles * ROW_TILE
    return pl.pallas_call(
        functools.partial(_mixout_kernel, alpha, sub_scale),
        out_shape=(jax.ShapeDtypeStruct((bsz, nrows, d), F32),
                   jax.ShapeDtypeStruct((bsz, nrows * (d // LANES), LANES), F32),
                   jax.ShapeDtypeStruct((bsz, nrows, LANES), jnp.int32),
                   jax.ShapeDtypeStruct((bsz, nrows, LANES), F32)),
        grid=(bsz, n_tiles),
        in_specs=[pl.BlockSpec((1, ROW_TILE, DA_W), orow),
                  pl.BlockSpec((1, ROW_TILE, GLA_W), row),
                  pl.BlockSpec((1, ROW_TILE, GLA_W), row),
                  pl.BlockSpec((1, ROW_TILE, Z_GLA), row),
                  pl.BlockSpec((1, ROW_TILE, RW_W), row),
                  pl.BlockSpec((1, ROW_TILE, RW_W), row),
                  pl.BlockSpec((1, ROW_TILE, RW_W), row),
                  pl.BlockSpec((1, ROW_TILE, RW_W), row),
                  pl.BlockSpec((1, ROW_TILE, d), row),
                  pl.BlockSpec((1, 1, 3, d),
                               lambda b_, i: (b_, jnp.where(tile0 + i < n_ctx_tiles, 0, 1), 0, 0)),
                  full((MIX_W, d)), full((2, d)), full((4, GLA_W)),
                  full((DA_W, DA_W)), full((GLA_W, GLA_W)), full((2, d, LANES)), full((1, LANES))],
        out_specs=(pl.BlockSpec((1, ROW_TILE, d), orow),
                   pl.BlockSpec((1, ROW_TILE * (d // LANES), LANES), orow),
                   pl.BlockSpec((1, ROW_TILE, LANES), orow),
                   pl.BlockSpec((1, ROW_TILE, LANES), orow)),
        compiler_params=_cparams(("arbitrary", "arbitrary")),
        name="mix_out",
    )(ao, gof, gob, zg, yf, yb, bonus, rg, h, mod2, wo, vec, nrm,
      _group_matrix(DA_W, DA_V), _group_matrix(GLA_W, GLA_DV), rw_p, rb_p)


def _bwd_chunk(step, n_ctx_chunks, n_chunks):
    return jnp.where(step < n_ctx_chunks, n_ctx_chunks - 1 - step, n_chunks + n_ctx_chunks - 1 - step)


def _tri(n, reverse, strict):
    t = lax.broadcasted_iota(jnp.int32, (n, n), 0)
    s = lax.broadcasted_iota(jnp.int32, (n, n), 1)
    if reverse:
        return (s > t) if strict else (s >= t)
    return (s < t) if strict else (s <= t)


def _log_sigmoid(x):
    return jnp.minimum(x, 0.0) - jnp.log1p(jnp.exp(-jnp.abs(x)))


def _softplus(x):
    return jnp.maximum(x, 0.0) + jnp.log1p(jnp.exp(-jnp.abs(x)))


def _split3(x):
    t0 = x.astype(BF16)
    r1 = x - t0.astype(F32)
    t1 = r1.astype(BF16)
    t2 = (r1 - t1.astype(F32)).astype(BF16)
    return t0, t1, t2


def _cumsum_rows(x, reverse):
    tri = jnp.where(_tri(CHUNK, reverse, False), 1.0, 0.0).astype(BF16)
    return sum(jnp.dot(tri, t, preferred_element_type=F32) for t in _split3(x))


def _bdot_nt(a, b):
    return lax.dot_general(a, b, (((1,), (1,)), ((), ())), preferred_element_type=F32)


GLA_SUB = 16
GLA_EXP_CLAMP = 60.0


def _gla_kernel(zf_ref, zb_ref, w2_ref, bias_ref, of_ref, ob_ref, st_ref):
    @pl.when(pl.program_id(1) == 0)
    def _():
        st_ref[...] = jnp.zeros_like(st_ref)

    c = CHUNK
    nsub = c // GLA_SUB
    nb = zf_ref.shape[0]
    qs, ks, vs, bs, b_lasts = [], [], [], [], []
    for i, reverse, z_ref in [(i, rv, zr) for i in range(nb) for rv, zr in ((False, zf_ref), (True, zb_ref))]:
        zg = z_ref[i]
        col0 = GLA_KP if reverse else 0
        gpre = _dot(zg[:, 2 * GLA_KP + 2 * GLA_W:Z_GLA], w2_ref[:, col0:col0 + GLA_KP]) \
            + bias_ref[:, col0:col0 + GLA_KP]
        b = _cumsum_rows(_log_sigmoid(gpre) * (1.0 / GLA_TAU), reverse)
        last = 0 if reverse else c - 1
        qs.append(zg[:, 0:GLA_KP] * (GLA_DK ** -0.5))
        ks.append(zg[:, GLA_KP:2 * GLA_KP])
        vs.append(zg[:, 2 * GLA_KP:2 * GLA_KP + GLA_W])
        bs.append(b)
        b_lasts.append(b[last:last + 1, :])
    q, k, v, b, b_last = (jnp.stack(x, axis=0) for x in (qs, ks, vs, bs, b_lasts))
    vb = v.astype(BF16)
    st = st_ref[...]
    inter = _bmm_nt((q * jnp.exp(b)).astype(BF16), st.astype(BF16))
    kv = _bmm_tn(vb, (k * jnp.exp(b_last - b)).astype(BF16))
    vi = lax.broadcasted_iota(jnp.int32, (GLA_W, GLA_KP), 0) // GLA_DV
    ki = lax.broadcasted_iota(jnp.int32, (GLA_W, GLA_KP), 1) // GLA_DK
    st_ref[...] = st * jnp.exp(b_last) + jnp.where((vi == ki)[None], kv, 0.0)

    lane_head = lax.broadcasted_iota(jnp.int32, (1, 1, GLA_KP), 2) // GLA_DK
    per_head = [[None] * nsub for _ in range(GLA_HEADS)]
    for i in range(nsub):
        r0 = i * GLA_SUB
        b_ref = jnp.stack([b[p, r0:r0 + 1] if p % 2 == 0 else b[p, r0 + GLA_SUB - 1:r0 + GLA_SUB]
                           for p in range(2 * nb)], axis=0)
        qi = q[:, r0:r0 + GLA_SUB] * jnp.exp(b[:, r0:r0 + GLA_SUB] - b_ref)
        ki_ = (k * jnp.exp(jnp.minimum(b_ref - b, GLA_EXP_CLAMP))).astype(BF16)
        qh = jnp.concatenate([jnp.where(lane_head == h, qi, 0.0) for h in range(GLA_HEADS)], axis=1)
        a = _bmm_nt(qh.astype(BF16), ki_)
        for h in range(GLA_HEADS):
            per_head[h][i] = a[:, h * GLA_SUB:(h + 1) * GLA_SUB]
    causal = jnp.stack([_tri(c, False, False), _tri(c, True, False)] * nb, axis=0)
    lane = lax.broadcasted_iota(jnp.int32, (1, 1, LANES), 2)
    pieces = []
    for p in range(GLA_HEADS // 2):
        vp = vb[:, :, p * LANES:(p + 1) * LANES]
        halves = []
        for hh in range(2):
            a_h = jnp.where(causal, jnp.concatenate(per_head[2 * p + hh], axis=1), 0.0)
            halves.append(_bmm(a_h.astype(BF16), vp))
        pieces.append(jnp.where(lane < GLA_DV, halves[0], halves[1]))
    o = inter + jnp.concatenate(pieces, axis=2)
    for i in range(nb):
        of_ref[i] = o[2 * i]
        ob_ref[i] = o[2 * i + 1]


def _gla_scan(zg, w2p, biasp, n_ctx_chunks):
    bsz, lt, _ = zg.shape
    nc = lt // CHUNK
    nb = SCAN_BATCH if bsz % SCAN_BATCH == 0 else 1
    fwd = lambda b_, s: (b_, s, 0)
    bwd = lambda b_, s: (b_, _bwd_chunk(s, n_ctx_chunks, nc), 0)
    return pl.pallas_call(
        _gla_kernel,
        out_shape=(jax.ShapeDtypeStruct((bsz, lt, GLA_W), F32),
                   jax.ShapeDtypeStruct((bsz, lt, GLA_W), F32)),
        grid=(bsz // nb, nc),
        in_specs=[pl.BlockSpec((nb, CHUNK, Z_GLA), fwd),
                  pl.BlockSpec((nb, CHUNK, Z_GLA), bwd),
                  pl.BlockSpec((LANES, 2 * GLA_KP), lambda b_, s: (0, 0)),
                  pl.BlockSpec((1, 2 * GLA_KP), lambda b_, s: (0, 0))],
        out_specs=(pl.BlockSpec((nb, CHUNK, GLA_W), fwd),
                   pl.BlockSpec((nb, CHUNK, GLA_W), bwd)),
        scratch_shapes=[pltpu.VMEM((2 * nb, GLA_W, GLA_KP), F32)],
        compiler_params=_cparams(("arbitrary", "arbitrary")),
        name="gla_scan",
    )(zg, zg, w2p, biasp)


def _pack_gla_gate(w2, bias):
    w = jnp.zeros((LANES, 2 * GLA_KP), F32)
    w = w.at[0:GLA_RANK, 0:GLA_KW].set(w2[0]).at[GLA_RANK:2 * GLA_RANK, GLA_KP:GLA_KP + GLA_KW].set(w2[1])
    b = jnp.zeros((1, 2 * GLA_KP), F32)
    b = b.at[0, 0:GLA_KW].set(bias[0]).at[0, GLA_KP:GLA_KP + GLA_KW].set(bias[1])
    return w, b


def _rwprep_kernel(n_ctx_tiles, n_tiles,
                   z_ref, zp_ref, zn_ref, cw_ref, w2_ref, a2_ref, g2_ref, vec_ref, gm_ref,
                   sh_ref, df_ref, db_ref, g_ref, bonus_ref):
    i = pl.program_id(1)
    z = z_ref[0]
    x = z[:, 0:3 * RW_W]
    seg_first = (i == 0) | (i == n_ctx_tiles)
    seg_last = (i == n_ctx_tiles - 1) | (i == n_tiles - 1)
    prev_row = jnp.where(seg_first, 0.0, zp_ref[0, 7:8, 0:3 * RW_W])
    next_row = jnp.where(seg_last, 0.0, zn_ref[0, 0:1, 0:3 * RW_W])
    ridx = lax.broadcasted_iota(jnp.int32, (ROW_TILE, 1), 0)
    x_prev = jnp.where(ridx == 0, prev_row, pltpu.roll(x, 1, 0))
    x_next = jnp.where(ridx == ROW_TILE - 1, next_row, pltpu.roll(x, ROW_TILE - 1, 0))
    xc = x_prev * cw_ref[0:1, :] + x * cw_ref[1:2, :] + x_next * cw_ref[2:3, :]
    r = xc[:, 0:RW_W]
    k = xc[:, RW_W:2 * RW_W]
    v = xc[:, 2 * RW_W:3 * RW_W]
    gm = gm_ref[...]
    kk = k * vec_ref[0:1, :]
    kk = kk / jnp.maximum(jnp.sqrt(_group_sum(kk * kk, gm)), 1e-12)
    k_a = vec_ref[1:2, :]
    r_k = vec_ref[2:3, :]
    w_raw = _dot3(jnp.tanh(z[:, 3 * RW_W:3 * RW_W + LANES]), w2_ref[...])
    a_raw = _dot3(z[:, 3 * RW_W + LANES:3 * RW_W + 2 * LANES], a2_ref[...])
    g_ref[0] = _dot3(jax.nn.sigmoid(z[:, 3 * RW_W + 2 * LANES:Z_RW]), g2_ref[...])
    sh_ref[0, :, 0:RW_W] = r
    sh_ref[0, :, RW_W:2 * RW_W] = v
    sh_ref[0, :, 2 * RW_W:3 * RW_W] = kk
    rk_sum = None
    for d, d_ref in enumerate((df_ref, db_ref)):
        wr = w_raw[:, d * RW_W:(d + 1) * RW_W] + vec_ref[3 + d:4 + d, :]
        logw = -jnp.exp(-_softplus(-wr) - 0.5)
        a = jax.nn.sigmoid(a_raw[:, d * RW_W:(d + 1) * RW_W] + vec_ref[5 + d:6 + d, :])
        k_mod = k * (1.0 + (a - 1.0) * k_a)
        d_ref[0, :, 0:RW_W] = logw
        d_ref[0, :, RW_W:2 * RW_W] = kk * a
        d_ref[0, :, 2 * RW_W:3 * RW_W] = k_mod
        s = _group_sum(r * k_mod * r_k, gm)
        rk_sum = s if d == 0 else rk_sum + s
    bonus_ref[0] = rk_sum * v


def _rw_prep(zr, cw, w2p, a2p, g2, vec, n_ctx_tiles):
    bsz, lt, _ = zr.shape
    nt = lt // ROW_TILE
    hb = ROW_TILE // 8
    row = lambda b_, i: (b_, i, 0)
    full = lambda shape: pl.BlockSpec(shape, lambda b_, i: tuple(0 for _ in shape))
    o3 = jax.ShapeDtypeStruct((bsz, lt, 3 * RW_W), F32)
    o1 = jax.ShapeDtypeStruct((bsz, lt, RW_W), F32)
    return pl.pallas_call(
        functools.partial(_rwprep_kernel, n_ctx_tiles, nt),
        out_shape=(o3, o3, o3, o1, o1),
        grid=(bsz, nt),
        in_specs=[pl.BlockSpec((1, ROW_TILE, Z_RW), row),
                  pl.BlockSpec((1, 8, Z_RW), lambda b_, i: (b_, jnp.maximum(i * hb - 1, 0), 0)),
                  pl.BlockSpec((1, 8, Z_RW), lambda b_, i: (b_, jnp.minimum((i + 1) * hb, nt * hb - 1), 0)),
                  full((3, 3 * RW_W)), full((LANES, 2 * RW_W)), full((LANES, 2 * RW_W)),
                  full((RW_GATE_RANK, RW_W)), full((8, RW_W)), full((RW_W, RW_W))],
        out_specs=(pl.BlockSpec((1, ROW_TILE, 3 * RW_W), row),
                   pl.BlockSpec((1, ROW_TILE, 3 * RW_W), row),
                   pl.BlockSpec((1, ROW_TILE, 3 * RW_W), row),
                   pl.BlockSpec((1, ROW_TILE, RW_W), row),
                   pl.BlockSpec((1, ROW_TILE, RW_W), row)),
        compiler_params=_cparams(("arbitrary", "arbitrary")),
        name="rwkv_prep",
    )(zr, zr, zr, cw, w2p, a2p, g2, vec, _group_matrix(RW_W, RW_D))


def _block2(w):
    r, n = w.shape[1:]
    z = jnp.zeros((r, n), w.dtype)
    return jnp.concatenate([jnp.concatenate([w[0], z], 1), jnp.concatenate([z, w[1]], 1)], 0)


def _bmm(a, b):
    return lax.dot_general(a, b, (((2,), (1,)), ((0,), (0,))), preferred_element_type=F32)


def _bmm_nt(a, b):
    return lax.dot_general(a, b, (((2,), (2,)), ((0,), (0,))), preferred_element_type=F32)


def _bmm_tn(a, b):
    return lax.dot_general(a, b, (((1,), (1,)), ((0,), (0,))), preferred_element_type=F32)


def _rw_operands(sh, dd, reverse):
    c = CHUNK
    r, v, kk = sh[:, 0:RW_W], sh[:, RW_W:2 * RW_W], sh[:, 2 * RW_W:3 * RW_W]
    logw, beta, k = dd[:, 0:RW_W], dd[:, RW_W:2 * RW_W], dd[:, 2 * RW_W:3 * RW_W]
    b = _cumsum_rows(logw, reverse)
    last = 0 if reverse else c - 1
    e_b = jnp.exp(b)
    e_nb = jnp.exp(-b)
    e_last = jnp.exp(b[last:last + 1, :])
    e_tot = e_last * e_nb
    abar = (kk * jnp.exp(b - logw)).astype(BF16)
    rbar = (r * e_b).astype(BF16)
    kt = (k * e_nb).astype(BF16)
    bt = (beta * e_nb).astype(BF16)
    khat = (k * e_tot).astype(BF16)
    nbhat = (-(beta * e_tot)).astype(BF16)
    vb = v.astype(BF16)
    head_a = lax.broadcasted_iota(jnp.int32, (1, LANES), 1) < RW_D
    zero = jnp.zeros((c, LANES), BF16)

    def stack(*xs):
        rows = []
        for x in xs:
            rows += [jnp.where(head_a, x, zero), jnp.where(head_a, zero, x)]
        return jnp.concatenate(rows, axis=0)

    out = []
    for p in range(RW_HEADS // 2):
        sl = slice(p * LANES, (p + 1) * LANES)
        out.append(dict(xar=stack(abar[:, sl], rbar[:, sl]), yb=stack(bt[:, sl]), yk=stack(kt[:, sl]),
                        vs=stack(vb[:, sl]), kb=stack(khat[:, sl], nbhat[:, sl]),
                        e_col=jnp.broadcast_to(e_last[:, sl], (LANES, LANES)).T))
    return out


def _rw_kernel(sf_ref, df_ref, sb_ref, db_ref, yf_ref, yb_ref, h_ref):
    @pl.when(pl.program_id(1) == 0)
    def _():
        h_ref[...] = jnp.zeros_like(h_ref)

    c = CHUNK
    c2 = 2 * c
    npair = RW_HEADS // 2
    nb = sf_ref.shape[0]
    ops = []
    for i in range(nb):
        ops += _rw_operands(sf_ref[i], df_ref[i], False) + _rw_operands(sb_ref[i], db_ref[i], True)
    cat = lambda name: jnp.stack([o[name] for o in ops], axis=0)
    xar, yb_, yk, vs, kb, e_col = (cat(n) for n in ("xar", "yb", "yk", "vs", "kb", "e_col"))

    ti = lax.broadcasted_iota(jnp.int32, (c2, c2), 0)
    si = lax.broadcasted_iota(jnp.int32, (c2, c2), 1)
    same_head = (ti // c) == (si // c)
    both = lambda fwd, bwd: jnp.concatenate([jnp.broadcast_to(fwd[None], (npair, c2, c2)),
                                             jnp.broadcast_to(bwd[None], (npair, c2, c2))] * nb, axis=0)
    strict = both(same_head & (si < ti), same_head & (si > ti))
    incl = both(same_head & (si <= ti), same_head & (si >= ti))
    eye = jnp.where(ti == si, 1.0, 0.0).astype(F32)

    gb = _bmm_nt(xar, yb_)
    gk = _bmm_nt(xar, yk)
    l_ab = jnp.where(strict, gb[:, 0:c2], 0.0)
    l_rb = jnp.where(incl, gb[:, c2:2 * c2], 0.0).astype(BF16)
    l_ak = jnp.where(strict, gk[:, 0:c2], 0.0)
    l_rk = jnp.where(incl, gk[:, c2:2 * c2], 0.0)
    t_inv = None
    s = 1
    while s < c:
        same = (ti // (2 * s)) == (si // (2 * s))
        lo, hi = (ti // s) % 2, (si // s) % 2
        off = both(same & (lo == 1) & (hi == 0), same & (lo == 0) & (hi == 1))
        l_off = jnp.where(off, l_ab, 0.0)
        if t_inv is None:
            t_inv = eye[None] - l_off
        else:
            tb = t_inv.astype(BF16)
            t_inv = t_inv - _bmm(tb, _bmm(l_off.astype(BF16), tb).astype(BF16))
        s *= 2
    h0 = h_ref[...]
    xh = _bmm(xar, h0.astype(BF16))
    lv = _bmm(jnp.concatenate([l_ak, l_rk], axis=1).astype(BF16), vs)
    ub = _bmm(t_inv.astype(BF16), (xh[:, 0:c2] + lv[:, 0:c2]).astype(BF16)).astype(BF16)
    y2 = xh[:, c2:2 * c2] + lv[:, c2:2 * c2] - _bmm(l_rb, ub)
    y = y2[:, 0:c] + y2[:, c:c2]
    h_ref[...] = e_col * h0 + _bmm_tn(kb, jnp.concatenate([vs, ub], axis=1))
    for i in range(nb):
        o = 2 * npair * i
        yf_ref[i] = jnp.concatenate([y[o + p] for p in range(npair)], axis=1)
        yb_ref[i] = jnp.concatenate([y[o + npair + p] for p in range(npair)], axis=1)


def _rw_scan(shared, dfw, dbw, n_ctx_chunks):
    bsz, lt, _ = shared.shape
    nc = lt // CHUNK
    nb = SCAN_BATCH if bsz % SCAN_BATCH == 0 else 1
    fwd = lambda b_, s: (b_, s, 0)
    bwd = lambda b_, s: (b_, _bwd_chunk(s, n_ctx_chunks, nc), 0)
    return pl.pallas_call(
        _rw_kernel,
        out_shape=(jax.ShapeDtypeStruct((bsz, lt, RW_W), F32),
                   jax.ShapeDtypeStruct((bsz, lt, RW_W), F32)),
        grid=(bsz // nb, nc),
        in_specs=[pl.BlockSpec((nb, CHUNK, 3 * RW_W), fwd),
                  pl.BlockSpec((nb, CHUNK, 3 * RW_W), fwd),
                  pl.BlockSpec((nb, CHUNK, 3 * RW_W), bwd),
                  pl.BlockSpec((nb, CHUNK, 3 * RW_W), bwd)],
        out_specs=(pl.BlockSpec((nb, CHUNK, RW_W), fwd),
                   pl.BlockSpec((nb, CHUNK, RW_W), bwd)),
        scratch_shapes=[pltpu.VMEM((nb * RW_HEADS, LANES, LANES), F32)],
        compiler_params=_cparams(("arbitrary", "arbitrary")),
        name="rwkv_scan",
    )(shared, dfw, shared, dbw)


def _rank_kernel(n, ti_ref, dest_ref, meta_ref, cnt_ref, run_ref, start_ref):
    ph = pl.program_id(0)
    i = pl.program_id(1)
    ti = ti_ref[...]
    lane = lax.broadcasted_iota(jnp.int32, (ROW_TILE, LANES), 1)
    ohs = [jnp.where(ti[:, j:j + 1] == lane, 1.0, 0.0).astype(F32) for j in range(TOP_K)]
    oh = ohs[0] + ohs[1] + ohs[2] + ohs[3]
    tile_cnt = jnp.sum(oh, axis=0, keepdims=True)

    @pl.when((ph == 0) & (i == 0))
    def _():
        cnt_ref[...] = jnp.zeros_like(cnt_ref)
        run_ref[...] = jnp.zeros_like(run_ref)

    @pl.when(ph == 0)
    def _():
        cnt_ref[...] += tile_cnt

    @pl.when((ph == 0) & (i == n - 1))
    def _():
        cnt = cnt_ref[...]
        shift = MOE_BLOCK.bit_length() - 1
        padded = jnp.left_shift(jnp.right_shift(cnt.astype(jnp.int32) + (MOE_BLOCK - 1), shift),
                                shift).astype(F32)
        e0 = lax.broadcasted_iota(jnp.int32, (LANES, LANES), 0)
        e1 = lax.broadcasted_iota(jnp.int32, (LANES, LANES), 1)
        before = jnp.where(e0 < e1, 1.0, 0.0).astype(F32)
        start = _dot_hi(jnp.broadcast_to(padded, (8, LANES)), before)[0:1]
        start_ref[...] = start
        meta_ref[0:1, :] = cnt
        meta_ref[1:2, :] = start
        meta_ref[2:3, :] = padded
        meta_ref[3:8, :] = jnp.zeros((5, LANES), F32)

    @pl.when(ph == 1)
    def _():
        t0 = lax.broadcasted_iota(jnp.int32, (ROW_TILE, ROW_TILE), 0)
        t1 = lax.broadcasted_iota(jnp.int32, (ROW_TILE, ROW_TILE), 1)
        earlier = jnp.where(t1 < t0, 1.0, 0.0).astype(BF16)
        pos = (jnp.dot(earlier, oh.astype(BF16), preferred_element_type=F32)
               + run_ref[...] + start_ref[...])
        dest = jnp.zeros((ROW_TILE, LANES), F32)
        for j in range(TOP_K):
            dj = jnp.sum(ohs[j] * pos, axis=-1, keepdims=True)
            dest = jnp.where(lane == j, dj, dest)
        dest_ref[0] = dest.T[0:8, :].astype(jnp.int32)
        run_ref[...] += tile_cnt


def _moe_rank(ti):
    n = ti.shape[0]
    nt = n // ROW_TILE
    return pl.pallas_call(
        functools.partial(_rank_kernel, nt),
        out_shape=(jax.ShapeDtypeStruct((nt, 8, ROW_TILE), jnp.int32),
                   jax.ShapeDtypeStruct((8, LANES), F32)),
        grid=(2, nt),
        in_specs=[pl.BlockSpec((ROW_TILE, LANES), lambda p, i: (i, 0))],
        out_specs=(pl.BlockSpec((1, 8, ROW_TILE), lambda p, i: (i * p, 0, 0)),
                   pl.BlockSpec((8, LANES), lambda p, i: (0, 0))),
        scratch_shapes=[pltpu.VMEM((1, LANES), F32), pltpu.VMEM((1, LANES), F32),
                        pltpu.VMEM((1, LANES), F32)],
        compiler_params=_cparams(("arbitrary", "arbitrary")),
        name="moe_rank",
    )(ti)


MOE_TOK_BITS = 15


def _invert_kernel(n_tiles, plane_rows, lo_ref, hi_ref, dest_hbm, code_ref, idx_s, sem):
    s = pl.program_id(0)

    def fetch(i, slot):
        return pltpu.make_async_copy(dest_hbm.at[i], idx_s.at[slot], sem.at[slot])

    @pl.when(s < N_EXPERTS)
    def _():
        def fill(p, c):
            code_ref[p] = jnp.left_shift(TOP_K * plane_rows + jnp.bitwise_and(p, 2 * MOE_BLOCK - 1), MOE_TOK_BITS)
            return c

        lax.fori_loop(lo_ref[s], hi_ref[s], fill, 0)

    @pl.when(s == N_EXPERTS)
    def _():
        fetch(0, 0).start()

    @pl.when(s >= N_EXPERTS)
    def _():
        i = s - N_EXPERTS
        slot = i % 2

        @pl.when(i + 1 < n_tiles)
        def _():
            fetch(i + 1, 1 - slot).start()

        fetch(i, slot).wait()

        def tok(t, c):
            token = i * ROW_TILE + t
            for j in range(TOP_K):
                code_ref[idx_s[slot, 0, j * ROW_TILE + t]] = jnp.bitwise_or(
                    jnp.left_shift(j * plane_rows + token, MOE_TOK_BITS), token)
            return c

        lax.fori_loop(0, ROW_TILE, tok, 0, unroll=8)


def _moe_invert(dest2d, n_blocks, plane_rows, pad_lo, pad_hi):
    nt = dest2d.shape[0]
    assert nt * ROW_TILE < 2 ** MOE_TOK_BITS and TOP_K * plane_rows + 2 * MOE_BLOCK <= 2 ** (32 - MOE_TOK_BITS)
    return pl.pallas_call(
        functools.partial(_invert_kernel, nt, plane_rows),
        out_shape=jax.ShapeDtypeStruct((n_blocks * MOE_BLOCK,), jnp.int32),
        grid_spec=pltpu.PrefetchScalarGridSpec(
            num_scalar_prefetch=2,
            grid=(N_EXPERTS + nt,),
            in_specs=[pl.BlockSpec(memory_space=pl.ANY)],
            out_specs=pl.BlockSpec(memory_space=pltpu.SMEM),
            scratch_shapes=[pltpu.SMEM((2, 1, 8 * ROW_TILE), jnp.int32), pltpu.SemaphoreType.DMA((2,))]),
        compiler_params=_cparams(("arbitrary",)),
        name="moe_invert",
    )(pad_lo, pad_hi, dest2d.reshape(nt, 1, 8 * ROW_TILE))


def _fused_expert_kernel(plane_rows,
                         be_ref, nu_ref, code_hbm, m_hbm, wg_ref, bg_ref, wu_ref, bu_ref, wd_ref, bd_ref,
                         out_hbm, wg_s, wu_s, wd_s, xbuf0, xbuf1, ybuf0, ybuf1, idx_s, sem_i, sem_g, sem_s):
    b = pl.program_id(0)
    nu = nu_ref[0]
    blk = MOE_BLOCK
    xbufs = (xbuf0, xbuf1)
    ybufs = (ybuf0, ybuf1)

    def idx_copy(block, slot):
        return pltpu.make_async_copy(code_hbm.at[block], idx_s.at[slot], sem_i.at[slot])

    def gather(slot, r, buf):
        tok = jnp.bitwise_and(idx_s[slot, 0, r], 2 ** MOE_TOK_BITS - 1)
        return pltpu.make_async_copy(m_hbm.at[tok], xbufs[buf].at[pl.ds(r * SUBLANES, SUBLANES), :], sem_g.at[buf])

    def scatter(slot, r, buf):
        row = lax.shift_right_logical(idx_s[slot, 0, r], MOE_TOK_BITS)
        return pltpu.make_async_copy(ybufs[buf].at[pl.ds(r * SUBLANES, SUBLANES), :], out_hbm.at[row], sem_s.at[buf])

    def wait_rows(buf, sem):
        pltpu.make_async_copy(m_hbm.at[pl.ds(0, blk)], xbufs[buf].reshape(blk, SUBLANES, LANES), sem.at[buf]).wait()

    def step(cur):
        oth = 1 - cur
        s_prev, s_cur, s_next, s_far = (b + 3) % 4, b % 4, (b + 1) % 4, (b + 2) % 4
        idx_copy(jnp.minimum(b + 2, nu - 1), s_far).start()
        idx_copy(0, s_next).wait()
        for r in range(blk):
            scatter(s_prev, r, oth).start()
        for r in range(blk):
            gather(s_next, r, oth).start()
        wait_rows(cur, sem_g)
        nch = wg_s.shape[0] // LANES
        x = jnp.concatenate([xbufs[cur][pl.ds(c, blk, stride=SUBLANES), :] for c in range(nch)],
                            axis=1).astype(BF16)
        gt = jnp.minimum(jnp.dot(x, wg_s[...], preferred_element_type=F32) + bg_ref[0, 0], SWIGLU_LIMIT)
        up = jnp.clip(jnp.dot(x, wu_s[...], preferred_element_type=F32) + bu_ref[0, 0],
                      -SWIGLU_LIMIT, SWIGLU_LIMIT)
        act = (up + 1.0) * gt * jax.nn.sigmoid(SWIGLU_ALPHA * gt)
        y = jnp.dot(act.astype(BF16), wd_s[...], preferred_element_type=F32) + bd_ref[0, 0]
        for c in range(nch):
            ybufs[cur][pl.ds(c, blk, stride=SUBLANES), :] = y[:, c * LANES:(c + 1) * LANES]
        wait_rows(oth, sem_s)

        @pl.when(b == nu - 1)
        def _():
            wait_rows(oth, sem_g)
            idx_copy(0, s_far).wait()

            def last(r, c):
                scatter(s_cur, r, cur).start()
                return c

            lax.fori_loop(0, blk, last, 0, unroll=8)
            wait_rows(cur, sem_s)

    @pl.when(b < nu)
    def _():
        e = be_ref[b]
        changed = (b == 0) | (e != be_ref[jnp.maximum(b - 1, 0)])

        @pl.when(changed)
        def _():
            wg_s[...] = wg_ref[0, 0].astype(BF16)
            wu_s[...] = wu_ref[0, 0].astype(BF16)
            wd_s[...] = wd_ref[0, 0].astype(BF16)

        @pl.when(b == 0)
        def _():
            ybuf0[...] = jnp.zeros_like(ybuf0)
            ybuf1[...] = jnp.zeros_like(ybuf1)
            cp = idx_copy(0, 0)
            cp.start()
            cp.wait()
            idx_copy(jnp.minimum(1, nu - 1), 1).start()

            def spare(r, c):
                pltpu.make_async_copy(ybuf0.at[pl.ds(r * SUBLANES, SUBLANES), :],
                                      out_hbm.at[TOP_K * plane_rows + r], sem_s.at[0]).start()
                idx_s[3, 0, r] = jnp.left_shift(TOP_K * plane_rows + blk + r, MOE_TOK_BITS)
                return c

            lax.fori_loop(0, blk, spare, 0, unroll=8)
            wait_rows(0, sem_s)

            def first(r, c):
                gather(0, r, 0).start()
                return c

            lax.fori_loop(0, blk, first, 0, unroll=8)

        @pl.when(b % 2 == 0)
        def _():
            step(0)

        @pl.when(b % 2 == 1)
        def _():
            step(1)


def _moe_experts_fused(layer, block_expert, n_used, codes, m, plane_rows, wg, bg, wu, bu, wd, bd):
    nch = m.shape[1]
    d = nch * LANES
    nl, ne, _, f = wg.shape
    nb = codes.shape[0]
    wmap = lambda b, be, nu: (layer, be[jnp.maximum(jnp.minimum(b, nu[0] - 1), 0)], 0, 0)
    any_spec = pl.BlockSpec(memory_space=pl.ANY)
    return pl.pallas_call(
        functools.partial(_fused_expert_kernel, plane_rows),
        out_shape=jax.ShapeDtypeStruct((TOP_K * plane_rows + 2 * MOE_BLOCK, nch, LANES), F32),
        grid_spec=pltpu.PrefetchScalarGridSpec(
            num_scalar_prefetch=2,
            grid=(nb,),
            in_specs=[any_spec, any_spec,
                      pl.BlockSpec((1, 1, d, f), wmap), pl.BlockSpec((1, 1, 1, f), wmap),
                      pl.BlockSpec((1, 1, d, f), wmap), pl.BlockSpec((1, 1, 1, f), wmap),
                      pl.BlockSpec((1, 1, f, d), wmap), pl.BlockSpec((1, 1, 1, d), wmap)],
            out_specs=any_spec,
            scratch_shapes=[pltpu.VMEM((d, f), BF16), pltpu.VMEM((d, f), BF16), pltpu.VMEM((f, d), BF16),
                            pltpu.VMEM((MOE_BLOCK * nch, LANES), F32), pltpu.VMEM((MOE_BLOCK * nch, LANES), F32),
                            pltpu.VMEM((MOE_BLOCK * nch, LANES), F32), pltpu.VMEM((MOE_BLOCK * nch, LANES), F32),
                            pltpu.SMEM((4, 1, MOE_BLOCK), jnp.int32),
                            pltpu.SemaphoreType.DMA((4,)), pltpu.SemaphoreType.DMA((2,)),
                            pltpu.SemaphoreType.DMA((2,))]),
        compiler_params=_cparams(("arbitrary",)),
        name="moe_experts",
    )(block_expert, n_used, codes.reshape(nb, 1, MOE_BLOCK), m, wg, bg.reshape(nl, ne, 1, f),
      wu, bu.reshape(nl, ne, 1, f), wd, bd.reshape(nl, ne, 1, d))


def _combine2_kernel(alpha, gt_ref, h_ref, mod_ref, vec_ref, y0_ref, y1_ref, y2_ref, y3_ref, o_ref):
    gt = gt_ref[0]
    d = h_ref.shape[2]
    xs = []
    for c in range(d // LANES):
        rows = pl.ds(c, ROW_TILE, stride=SUBLANES)
        f = (gt[:, 0:1] * y0_ref[rows, :] + gt[:, 1:2] * y1_ref[rows, :]
             + gt[:, 2:3] * y2_ref[rows, :] + gt[:, 3:4] * y3_ref[rows, :])
        cols = slice(c * LANES, (c + 1) * LANES)
        xs.append(alpha * h_ref[0, :, cols] + mod_ref[0, 0, 0:1, cols] * f)
    mu = sum(jnp.sum(x, axis=-1, keepdims=True) for x in xs) * (1.0 / d)
    xs = [x - mu for x in xs]
    var = sum(jnp.sum(x * x, axis=-1, keepdims=True) for x in xs) * (1.0 / d)
    inv = lax.rsqrt(var + LN_EPS)
    for c, x in enumerate(xs):
        cols = slice(c * LANES, (c + 1) * LANES)
        o_ref[0, :, cols] = x * inv * vec_ref[0:1, cols] + vec_ref[1:2, cols]


def _moe_combine2(alpha, n_ctx_tiles, tile0, plane_rows, gates, h1, mod3, vec, out4):
    bsz, rows, d = h1.shape
    nch = d // LANES
    nt = rows // ROW_TILE
    pt = plane_rows // ROW_TILE
    row = lambda b_, i: (b_, i, 0)
    plane = lambda j: pl.BlockSpec((ROW_TILE * nch, LANES), lambda b_, i: (j * pt + b_ * nt + i, 0))
    out4 = out4.reshape((TOP_K * plane_rows + 2 * MOE_BLOCK) * nch, LANES)
    return pl.pallas_call(
        functools.partial(_combine2_kernel, alpha),
        out_shape=jax.ShapeDtypeStruct((bsz, rows, d), F32),
        grid=(bsz, nt),
        in_specs=[pl.BlockSpec((1, ROW_TILE, LANES), row),
                  pl.BlockSpec((1, ROW_TILE, d), row),
                  pl.BlockSpec((1, 1, 1, d), lambda b_, i: (b_, jnp.where(tile0 + i < n_ctx_tiles, 0, 1), 0, 0)),
                  pl.BlockSpec((2, d), lambda b_, i: (0, 0)),
                  plane(0), plane(1), plane(2), plane(3)],
        out_specs=pl.BlockSpec((1, ROW_TILE, d), row),
        compiler_params=_cparams(("arbitrary", "arbitrary")),
        name="moe_combine",
    )(gates, h1, mod3, vec, out4, out4, out4, out4)


def _moe(layer, alpha, n_ctx_tiles, tile0, h1, m, ti, gates, mod3, ln2, wg, bg, wu, bu, wd, bd):
    bsz, rows, d = h1.shape
    n = bsz * rows
    dest, meta = _moe_rank(ti.reshape(n, LANES))
    dest2d = dest.reshape(n // ROW_TILE, 8 * ROW_TILE)
    n_blocks = -(-(n * TOP_K) // MOE_BLOCK) + N_EXPERTS
    pad_end = (meta[1, :N_EXPERTS] + meta[2, :N_EXPERTS]).astype(jnp.int32)
    block_row = jnp.arange(n_blocks, dtype=jnp.int32) * MOE_BLOCK
    block_expert = jnp.minimum(jnp.sum((pad_end[None, :] <= block_row[:, None]).astype(jnp.int32), axis=1),
                               N_EXPERTS - 1)
    n_used = (pad_end[-1:] // MOE_BLOCK).astype(jnp.int32)
    plane_rows = n
    pad_lo = (meta[1, :N_EXPERTS] + meta[0, :N_EXPERTS]).astype(jnp.int32)
    pad_hi = jnp.concatenate([pad_end[:-1], jnp.full((1,), n_blocks * MOE_BLOCK, jnp.int32)])
    codes = _moe_invert(dest2d, n_blocks, plane_rows, pad_lo, pad_hi).reshape(n_blocks, MOE_BLOCK)
    out4 = _moe_experts_fused(layer, block_expert, n_used, codes, m.reshape(n, d // LANES, LANES), plane_rows,
                              wg, bg, wu, bu, wd, bd)
    return _moe_combine2(alpha, n_ctx_tiles, tile0, plane_rows, gates, h1, mod3, ln2, out4)


def kernel(x, c, ctx, c_ctx, ln_in_g, ln_in_b, ada_w, ada_b, w_in, lam_q1, lam_k1, lam_q2, lam_k2,
           da_subln_g, gla_gate_w2, gla_gate_b, gla_norm_g, rw_conv_w, rw_w2, rw_w0, rw_a2, rw_a0,
           rw_g2, rw_k_k, rw_k_a, rw_r_k, rw_lnx_g, rw_lnx_b, w_out, ln1_g, ln1_b, router_w, router_b,
           moe_w_gate, moe_b_gate, moe_w_up, moe_b_up, moe_w_down, moe_b_down, ln2_g, ln2_b):
    bsz, seq, d = x.shape
    n_ctx = ctx.shape[1]
    depth = w_in.shape[0]
    assert n_ctx % ROW_TILE == 0 and seq % ROW_TILE == 0 and seq % GRID_W == 0
    assert w_in.shape[2] == 3488 and bsz + 1 <= 8
    nct = n_ctx // ROW_TILE
    ncc = n_ctx // CHUNK
    nt = (n_ctx + seq) // ROW_TILE
    alpha = (2 * depth) ** 0.25

    c_all = jnp.concatenate([c, c_ctx[None], jnp.zeros((8 - bsz - 1, d), F32)], axis=0)
    mods = _ada_mod(c_all, ada_w, ada_b).reshape(depth, 8, 6, d)
    h = _ln_in(ctx, x, ln_in_g, ln_in_b)
    cos_t, sin_t = _rope_tables(n_ctx, seq)

    def pick(l, idx):
        mc = jnp.broadcast_to(mods[l, bsz][None, idx], (bsz, len(idx), d))
        return jnp.stack([mc, mods[l, :bsz][:, idx]], axis=1)

    for l in range(depth):
        last = l == depth - 1
        tile0 = nct if last else 0
        n_out_tiles = nt - tile0
        lam_init = 0.8 - 0.6 * math.exp(-0.3 * l)
        lam = (jnp.exp(jnp.sum(lam_q1[l] * lam_k1[l])) - jnp.exp(jnp.sum(lam_q2[l] * lam_k2[l])) + lam_init)

        qk, v, zg, zr = _inproj(h, pick(l, [1, 0]), _pack_w_in(w_in[l]), cos_t, sin_t, nct)
        ao = _attention(qk, v, lam, nct, nt - nct, n_ctx + seq)
        if not last:
            ao = jnp.concatenate([_attention(qk, v, lam, 0, nct, n_ctx), ao], axis=1)
        w2p, biasp = _pack_gla_gate(gla_gate_w2[l], gla_gate_b[l])
        gof, gob = _gla_scan(zg, w2p, biasp, ncc)
        rvec = jnp.stack([rw_k_k[l], rw_k_a[l], rw_r_k[l].reshape(-1), rw_w0[l, 0], rw_w0[l, 1],
                          rw_a0[l, 0], rw_a0[l, 1], jnp.zeros((RW_W,), F32)], axis=0)
        shared, dfw, dbw, rg, bonus = _rw_prep(zr, rw_conv_w[l], _block2(rw_w2[l]), _block2(rw_a2[l]),
                                               rw_g2[l], rvec, nct)
        yf, yb = _rw_scan(shared, dfw, dbw, ncc)

        nrm = jnp.stack([jnp.pad(jnp.tile(da_subln_g[l], DA_HEADS), (0, GLA_W - DA_W)),
                         jnp.tile(gla_norm_g[l], GLA_HEADS), rw_lnx_g[l], rw_lnx_b[l]], axis=0)
        rw_f = jnp.pad(router_w[l], ((0, 0), (0, LANES - N_EXPERTS)))
        rw_hi = rw_f.astype(BF16)
        rw_p = jnp.stack([rw_hi, (rw_f - rw_hi.astype(F32)).astype(BF16)], axis=0)
        rb_p = jnp.pad(router_b[l], (0, LANES - N_EXPERTS), constant_values=-1e30).reshape(1, LANES)
        h1, m, ti, gates = _mixout(alpha, 1.0 - lam_init, tile0, n_out_tiles, nct,
                                   ao, gof, gob, zg, yf, yb, bonus, rg, h, pick(l, [2, 4, 3]),
                                   w_out[l].astype(BF16), jnp.stack([ln1_g[l], ln1_b[l]], 0), nrm, rw_p, rb_p)
        h = _moe(l, alpha, nct, tile0, h1, m, ti, gates, pick(l, [5]), jnp.stack([ln2_g[l], ln2_b[l]], 0),
                 moe_w_gate, moe_b_gate, moe_w_up, moe_b_up, moe_w_down, moe_b_down)
    return h
```

```python
import functools
import math

import jax
import jax.numpy as jnp
import numpy as np
from jax import lax
from jax.experimental import pallas as pl
from jax.experimental.pallas import tpu as pltpu

F32 = jnp.float32
BF16 = jnp.bfloat16
HI = lax.Precision.HIGHEST

GRID_W = 64
DA_HEADS, DA_QK, DA_V = 4, 32, 64
GLA_HEADS, GLA_DK, GLA_DV, GLA_RANK, GLA_TAU = 6, 32, 64, 16, 16.0
RW_HEADS, RW_D, RW_DECAY_RANK, RW_A_RANK, RW_GATE_RANK = 6, 64, 64, 64, 128
RW_GN_EPS = 64e-5
N_EXPERTS, TOP_K = 32, 4
SWIGLU_LIMIT, SWIGLU_ALPHA = 7.0, 1.702
ROPE_BASE = 10000.0
LN_EPS = 1e-5

DA_W = DA_HEADS * DA_V
GLA_KW = GLA_HEADS * GLA_DK
GLA_W = GLA_HEADS * GLA_DV
RW_W = RW_HEADS * RW_D
MIX_W = DA_W + GLA_W + RW_W

LANES = 128
ROW_TILE = 256
CHUNK = 64
SCAN_BATCH = 4
MOE_BLOCK = 256
VMEM_LIMIT = 56 * 1024 * 1024
SUBLANES = 8

Z_ATTN = 3 * DA_W
GLA_KP = 256
Z_GLA = 2 * GLA_KP + 2 * GLA_W + LANES
Z_RW = 3 * RW_W + 3 * LANES
Z_ALL = Z_ATTN + Z_GLA + Z_RW


def _cparams(sem):
    return pltpu.CompilerParams(dimension_semantics=sem, vmem_limit_bytes=VMEM_LIMIT)


def _ln(x, g, b, eps):
    mu = jnp.mean(x, axis=-1, keepdims=True)
    xc = x - mu
    var = jnp.mean(xc * xc, axis=-1, keepdims=True)
    return xc * lax.rsqrt(var + eps) * g + b


def _silu(x):
    return x * jax.nn.sigmoid(x)


def _dot(a, b):
    return jnp.dot(a.astype(BF16), b.astype(BF16), preferred_element_type=F32)


def _dot_hi(a, b):
    return jnp.dot(a, b, precision=HI, preferred_element_type=F32)


def _split2(x):
    hi = x.astype(BF16)
    return hi, (x - hi.astype(F32)).astype(BF16)


def _dot3(a, b):
    a_hi, a_lo = _split2(a)
    b_hi, b_lo = _split2(b)
    return (jnp.dot(a_hi, b_hi, preferred_element_type=F32) + jnp.dot(a_hi, b_lo, preferred_element_type=F32)
            + jnp.dot(a_lo, b_hi, preferred_element_type=F32))


def _group_sum(x, gmat):
    hi = x.astype(BF16)
    lo = (x - hi.astype(F32)).astype(BF16)
    return (jnp.dot(hi, gmat, preferred_element_type=F32)
            + jnp.dot(lo, gmat, preferred_element_type=F32))


def _group_matrix(width, group):
    idx = np.arange(width) // group
    return jnp.asarray((idx[:, None] == idx[None, :]).astype(np.float32), dtype=BF16)


def _ada_kernel(c_ref, w_ref, b_ref, o_ref):
    o_ref[0] = _dot_hi(_silu(c_ref[...]), w_ref[0]) + b_ref[0]


def _ada_mod(c_all, ada_w, ada_b):
    nl, d, n6 = ada_w.shape
    tn = 1536
    return pl.pallas_call(
        _ada_kernel,
        out_shape=jax.ShapeDtypeStruct((nl, c_all.shape[0], n6), F32),
        grid=(nl, n6 // tn),
        in_specs=[pl.BlockSpec((c_all.shape[0], d), lambda l, j: (0, 0)),
                  pl.BlockSpec((1, d, tn), lambda l, j: (l, 0, j)),
                  pl.BlockSpec((1, 1, tn), lambda l, j: (l, 0, j))],
        out_specs=pl.BlockSpec((1, c_all.shape[0], tn), lambda l, j: (l, 0, j)),
        compiler_params=_cparams(("arbitrary", "arbitrary")),
        name="ada_mod",
    )(c_all, ada_w, ada_b.reshape(nl, 1, n6))


def _ln_in_kernel(n_ctx_tiles, c_ref, x_ref, g_ref, b_ref, o_ref):
    @pl.when(pl.program_id(1) < n_ctx_tiles)
    def _():
        o_ref[0] = _ln(c_ref[0], g_ref[...], b_ref[...], LN_EPS)

    @pl.when(pl.program_id(1) >= n_ctx_tiles)
    def _():
        o_ref[0] = _ln(x_ref[0], g_ref[...], b_ref[...], LN_EPS)


def _ln_in(ctx, x, g, b):
    bsz, n_ctx, d = ctx.shape
    nct = n_ctx // ROW_TILE
    nt = nct + x.shape[1] // ROW_TILE
    return pl.pallas_call(
        functools.partial(_ln_in_kernel, nct),
        out_shape=jax.ShapeDtypeStruct((bsz, nt * ROW_TILE, d), F32),
        grid=(bsz, nt),
        in_specs=[pl.BlockSpec((1, ROW_TILE, d), lambda b_, i: (b_, jnp.minimum(i, nct - 1), 0)),
                  pl.BlockSpec((1, ROW_TILE, d), lambda b_, i: (b_, jnp.maximum(i - nct, 0), 0)),
                  pl.BlockSpec((1, d), lambda b_, i: (0, 0)),
                  pl.BlockSpec((1, d), lambda b_, i: (0, 0))],
        out_specs=pl.BlockSpec((1, ROW_TILE, d), lambda b_, i: (b_, i, 0)),
        compiler_params=_cparams(("arbitrary", "arbitrary")),
        name="ln_in",
    )(ctx, x, g.reshape(1, d), b.reshape(1, d))


def _inproj_kernel(h_ref, mod_ref, w_ref, cos_ref, sin_ref, qk_ref, v_ref, gla_ref, rw_ref):
    h = h_ref[0]
    sc = mod_ref[0, 0, 0:1, :]
    sh = mod_ref[0, 0, 1:2, :]
    xm = (h * (1.0 + sc) + sh).astype(BF16)
    qk = jnp.dot(xm, w_ref[:, 0:2 * DA_W], preferred_element_type=F32)
    cos = cos_ref[...]
    sin = sin_ref[...]
    lane = lax.broadcasted_iota(jnp.int32, (1, LANES), 1)
    first = (lane % 16) < 8
    qscale = DA_QK ** -0.5
    for j in range(4):
        x = qk[:, j * LANES:(j + 1) * LANES]
        rot = jnp.where(first, pltpu.roll(x, LANES - 8, 1), pltpu.roll(x, 8, 1))
        y = x * cos + rot * sin
        if j < 2:
            y = y * qscale
        qk_ref[0, :, j * LANES:(j + 1) * LANES] = y.astype(BF16)
    v_ref[0] = jnp.dot(xm, w_ref[:, 2 * DA_W:Z_ATTN], preferred_element_type=F32).astype(BF16)
    gla_ref[0] = jnp.dot(xm, w_ref[:, Z_ATTN:Z_ATTN + Z_GLA], preferred_element_type=F32)
    rw_ref[0] = jnp.dot(xm, w_ref[:, Z_ATTN + Z_GLA:Z_ALL], preferred_element_type=F32)


def _inproj(h, mod1, w_p, cos_t, sin_t, n_ctx_tiles):
    bsz, lt, d = h.shape
    nt = lt // ROW_TILE
    row = lambda b_, i: (b_, i, 0)
    return pl.pallas_call(
        _inproj_kernel,
        out_shape=(jax.ShapeDtypeStruct((bsz, lt, 2 * DA_W), BF16),
                   jax.ShapeDtypeStruct((bsz, lt, DA_W), BF16),
                   jax.ShapeDtypeStruct((bsz, lt, Z_GLA), F32),
                   jax.ShapeDtypeStruct((bsz, lt, Z_RW), F32)),
        grid=(bsz, nt),
        in_specs=[pl.BlockSpec((1, ROW_TILE, d), row),
                  pl.BlockSpec((1, 1, 2, d),
                               lambda b_, i: (b_, jnp.where(i < n_ctx_tiles, 0, 1), 0, 0)),
                  pl.BlockSpec((d, Z_ALL), lambda b_, i: (0, 0)),
                  pl.BlockSpec((ROW_TILE, LANES), lambda b_, i: (i, 0)),
                  pl.BlockSpec((ROW_TILE, LANES), lambda b_, i: (i, 0))],
        out_specs=(pl.BlockSpec((1, ROW_TILE, 2 * DA_W), row),
                   pl.BlockSpec((1, ROW_TILE, DA_W), row),
                   pl.BlockSpec((1, ROW_TILE, Z_GLA), row),
                   pl.BlockSpec((1, ROW_TILE, Z_RW), row)),
        compiler_params=_cparams(("arbitrary", "arbitrary")),
        name="inproj",
    )(h, mod1, w_p, cos_t, sin_t)


def _pack_w_in(w):
    d = w.shape[0]
    o = 0
    parts = {}
    for name, n in (("da_q", 256), ("da_k", 256), ("da_v", 256), ("gla_q", GLA_KW), ("gla_k", GLA_KW),
                    ("gla_v", GLA_W), ("gla_gf", GLA_RANK), ("gla_gb", GLA_RANK), ("gla_r", GLA_W),
                    ("rw_rkv", 3 * RW_W), ("rw_wf", 64), ("rw_wb", 64), ("rw_af", 64), ("rw_ab", 64),
                    ("rw_g", 128)):
        parts[name] = w[:, o:o + n]
        o += n
    z = lambda n: jnp.zeros((d, n), w.dtype)
    cols = [parts["da_q"], parts["da_k"], parts["da_v"],
            parts["gla_q"], z(GLA_KP - GLA_KW), parts["gla_k"], z(GLA_KP - GLA_KW),
            parts["gla_v"], parts["gla_r"], parts["gla_gf"], parts["gla_gb"], z(LANES - 2 * GLA_RANK),
            parts["rw_rkv"], parts["rw_wf"], parts["rw_wb"], parts["rw_af"], parts["rw_ab"], parts["rw_g"]]
    return jnp.concatenate(cols, axis=1).astype(BF16)


def _rope_tables(n_ctx, seq):
    t = np.arange(seq)
    row = (t // GRID_W).astype(np.float32)
    col = (t % GRID_W).astype(np.float32)
    quarter = DA_QK // 4
    inv = (ROPE_BASE ** (-np.arange(quarter, dtype=np.float32) / quarter)).astype(np.float32)
    ang_r = row[:, None] * inv
    ang_c = col[:, None] * inv
    ang = np.concatenate([ang_r, ang_r, ang_c, ang_c], -1).astype(np.float32)
    cos = np.tile(np.cos(ang), (1, LANES // DA_QK))
    sin = np.tile(np.sin(ang), (1, LANES // DA_QK))
    sign = np.where((np.arange(LANES) % 16) < 8, -1.0, 1.0).astype(np.float32)
    cos = np.concatenate([np.ones((n_ctx, LANES), np.float32), cos], 0)
    sin = np.concatenate([np.zeros((n_ctx, LANES), np.float32), sin * sign], 0)
    return jnp.asarray(cos, F32), jnp.asarray(sin, F32)


ATTN_KEY_CHUNK = 1280


def _attn_kernel(lam_ref, q_ref, k_ref, v_ref, o_ref):
    q = q_ref[0]
    lam = lam_ref[0, 0]
    tq = q.shape[0]
    n_keys = k_ref.shape[1]
    lane = lax.broadcasted_iota(jnp.int32, (1, LANES), 1)
    zero = jnp.zeros_like(q)
    outs = []
    for hh in range(2):
        q2 = jnp.concatenate(
            [jnp.where((lane >= hh * 64 + m * DA_QK) & (lane < hh * 64 + (m + 1) * DA_QK), q, zero)
             for m in range(2)], axis=0)
        m_run = l_run = acc = None
        for k0 in range(0, n_keys, ATTN_KEY_CHUNK):
            k1 = min(k0 + ATTN_KEY_CHUNK, n_keys)
            s = _bdot_nt(q2, k_ref[0, k0:k1, :])
            m_blk = jnp.max(s, axis=-1, keepdims=True)
            m_new = m_blk if m_run is None else jnp.maximum(m_run, m_blk)
            p = jnp.exp((s - m_new).astype(BF16))
            v_ext = jnp.concatenate([v_ref[0, k0:k1, :], jnp.ones((k1 - k0, LANES), BF16)], axis=1)
            pv_ext = jnp.dot(p, v_ext, preferred_element_type=F32)
            pv, p_sum = pv_ext[:, 0:LANES], pv_ext[:, LANES:2 * LANES]
            if m_run is None:
                l_run, acc = p_sum, pv
            else:
                scale = jnp.exp(m_run - m_new)
                l_run = scale * l_run + p_sum
                acc = scale * acc + pv
            m_run = m_new
        o = acc / l_run
        outs.append(o[0:tq] - lam * o[tq:2 * tq])
    o_ref[0] = jnp.where(lane < 64, outs[0], outs[1])


def _attention(qk, v, lam, q_tile0, n_q_tiles, n_k_rows):
    bsz, lt, _ = qk.shape
    out = pl.pallas_call(
        _attn_kernel,
        out_shape=jax.ShapeDtypeStruct((bsz, n_q_tiles * ROW_TILE, DA_W), F32),
        grid=(bsz, 2, n_q_tiles),
        in_specs=[pl.BlockSpec(memory_space=pltpu.SMEM),
                  pl.BlockSpec((1, ROW_TILE, LANES), lambda b_, p, i: (b_, q_tile0 + i, p)),
                  pl.BlockSpec((1, n_k_rows, LANES), lambda b_, p, i: (b_, 0, 2 + p)),
                  pl.BlockSpec((1, n_k_rows, LANES), lambda b_, p, i: (b_, 0, p))],
        out_specs=pl.BlockSpec((1, ROW_TILE, LANES), lambda b_, p, i: (b_, i, p)),
        compiler_params=_cparams(("arbitrary", "arbitrary", "arbitrary")),
        name="diff_attn",
    )(lam.reshape(1, 1), qk, qk, v)
    return out


def _mixout_kernel(alpha, sub_scale,
                   ao_ref, gof_ref, gob_ref, zg_ref, yf_ref, yb_ref, bonus_ref, rg_ref, h_ref, mod_ref,
                   wo_ref, vec_ref, nrm_ref, g256_ref, g384_ref, rw_ref, rb_ref,
                   h1_ref, m_ref, ti_ref, gt_ref):
    a = ao_ref[0]
    a = a * lax.rsqrt(_group_sum(a * a, g256_ref[...]) * (1.0 / DA_V) + LN_EPS) * nrm_ref[0:1, 0:DA_W] * sub_scale
    o = gof_ref[0] + gob_ref[0]
    r = zg_ref[0, :, 2 * GLA_KP + GLA_W:2 * GLA_KP + 2 * GLA_W]
    gl = (o * lax.rsqrt(_group_sum(o * o, g384_ref[...]) * (1.0 / GLA_DV) + LN_EPS)
          * nrm_ref[1:2, :] * _silu(r))
    y = yf_ref[0] + yb_ref[0]
    mu = _group_sum(y, g384_ref[...]) * (1.0 / RW_D)
    yc = y - mu
    var = _group_sum(yc * yc, g384_ref[...]) * (1.0 / RW_D)
    yn = yc * lax.rsqrt(var + RW_GN_EPS) * nrm_ref[2:3, :] + nrm_ref[3:4, :]
    rw = (yn + bonus_ref[0]) * rg_ref[0]
    mix = (_dot(a, wo_ref[0:DA_W, :]) + _dot(gl, wo_ref[DA_W:DA_W + GLA_W, :])
           + _dot(rw, wo_ref[DA_W + GLA_W:MIX_W, :]))
    g1 = mod_ref[0, 0, 0:1, :]
    sc2 = mod_ref[0, 0, 1:2, :]
    sh2 = mod_ref[0, 0, 2:3, :]
    h1 = _ln(alpha * h_ref[0] + g1 * mix, vec_ref[0:1, :], vec_ref[1:2, :], LN_EPS)
    h1_ref[0] = h1
    m = h1 * (1.0 + sc2) + sh2
    for c in range(m.shape[1] // LANES):
        m_ref[0, pl.ds(c, ROW_TILE, stride=SUBLANES), :] = m[:, c * LANES:(c + 1) * LANES]
    m_hi, m_lo = _split2(m)
    logits = (jnp.dot(m_hi, rw_ref[0], preferred_element_type=F32)
              + jnp.dot(m_hi, rw_ref[1], preferred_element_type=F32)
              + jnp.dot(m_lo, rw_ref[0], preferred_element_type=F32)) + rb_ref[...]
    lane = lax.broadcasted_iota(jnp.int32, logits.shape, 1)
    ti = jnp.zeros(logits.shape, jnp.int32)
    tv = jnp.full(logits.shape, -1e30, F32)
    for j in range(TOP_K):
        mx = jnp.max(logits, axis=-1, keepdims=True)
        idx = jnp.min(jnp.where(logits == mx, lane, LANES), axis=-1, keepdims=True)
        ti = jnp.where(lane == j, idx, ti)
        tv = jnp.where(lane == j, mx, tv)
        logits = jnp.where(lane == idx, -jnp.inf, logits)
    e = jnp.exp(tv - jnp.max(tv, axis=-1, keepdims=True))
    ti_ref[0] = ti
    gt_ref[0] = e / jnp.sum(e, axis=-1, keepdims=True)


def _mixout(alpha, sub_scale, tile0, n_tiles, n_ctx_tiles,
            ao, gof, gob, zg, yf, yb, bonus, rg, h, mod2, wo, vec, nrm, rw_p, rb_p):
    bsz, lt, d = h.shape
    row = lambda b_, i: (b_, tile0 + i, 0)
    orow = lambda b_, i: (b_, i, 0)
    full = lambda shape: pl.BlockSpec(shape, lambda b_, i: tuple(0 for _ in shape))
    nrows = n_tiles * ROW_TILE
    return pl.pallas_call(
        functools.partial(_mixout_kernel, alpha, sub_scale),
        out_shape=(jax.ShapeDtypeStruct((bsz, nrows, d), F32),
                   jax.ShapeDtypeStruct((bsz, nrows * (d // LANES), LANES), F32),
                   jax.ShapeDtypeStruct((bsz, nrows, LANES), jnp.int32),
                   jax.ShapeDtypeStruct((bsz, nrows, LANES), F32)),
        grid=(bsz, n_tiles),
        in_specs=[pl.BlockSpec((1, ROW_TILE, DA_W), orow),
                  pl.BlockSpec((1, ROW_TILE, GLA_W), row),
                  pl.BlockSpec((1, ROW_TILE, GLA_W), row),
                  pl.BlockSpec((1, ROW_TILE, Z_GLA), row),
                  pl.BlockSpec((1, ROW_TILE, RW_W), row),
                  pl.BlockSpec((1, ROW_TILE, RW_W), row),
                  pl.BlockSpec((1, ROW_TILE, RW_W), row),
                  pl.BlockSpec((1, ROW_TILE, RW_W), row),
                  pl.BlockSpec((1, ROW_TILE, d), row),
                  pl.BlockSpec((1, 1, 3, d),
                               lambda b_, i: (b_, jnp.where(tile0 + i < n_ctx_tiles, 0, 1), 0, 0)),
                  full((MIX_W, d)), full((2, d)), full((4, GLA_W)),
                  full((DA_W, DA_W)), full((GLA_W, GLA_W)), full((2, d, LANES)), full((1, LANES))],
        out_specs=(pl.BlockSpec((1, ROW_TILE, d), orow),
                   pl.BlockSpec((1, ROW_TILE * (d // LANES), LANES), orow),
                   pl.BlockSpec((1, ROW_TILE, LANES), orow),
                   pl.BlockSpec((1, ROW_TILE, LANES), orow)),
        compiler_params=_cparams(("arbitrary", "arbitrary")),
        name="mix_out",
    )(ao, gof, gob, zg, yf, yb, bonus, rg, h, mod2, wo, vec, nrm,
      _group_matrix(DA_W, DA_V), _group_matrix(GLA_W, GLA_DV), rw_p, rb_p)


def _bwd_chunk(step, n_ctx_chunks, n_chunks):
    return jnp.where(step < n_ctx_chunks, n_ctx_chunks - 1 - step, n_chunks + n_ctx_chunks - 1 - step)


def _tri(n, reverse, strict):
    t = lax.broadcasted_iota(jnp.int32, (n, n), 0)
    s = lax.broadcasted_iota(jnp.int32, (n, n), 1)
    if reverse:
        return (s > t) if strict else (s >= t)
    return (s < t) if strict else (s <= t)


def _log_sigmoid(x):
    return jnp.minimum(x, 0.0) - jnp.log1p(jnp.exp(-jnp.abs(x)))


def _split3(x):
    t0 = x.astype(BF16)
    r1 = x - t0.astype(F32)
    t1 = r1.astype(BF16)
    t2 = (r1 - t1.astype(F32)).astype(BF16)
    return t0, t1, t2


def _cumsum_rows(x, reverse):
    tri = jnp.where(_tri(CHUNK, reverse, False), 1.0, 0.0).astype(BF16)
    return sum(jnp.dot(tri, t, preferred_element_type=F32) for t in _split3(x))


def _bdot_nt(a, b):
    return lax.dot_general(a, b, (((1,), (1,)), ((), ())), preferred_element_type=F32)


GLA_SUB = 16
GLA_EXP_CLAMP = 60.0


def _gla_kernel(zf_ref, zb_ref, w2_ref, bias_ref, of_ref, ob_ref, st_ref):
    @pl.when(pl.program_id(1) == 0)
    def _():
        st_ref[...] = jnp.zeros_like(st_ref)

    c = CHUNK
    nsub = c // GLA_SUB
    nb = zf_ref.shape[0]
    qs, ks, vs, bs, b_lasts = [], [], [], [], []
    for i, reverse, z_ref in [(i, rv, zr) for i in range(nb) for rv, zr in ((False, zf_ref), (True, zb_ref))]:
        zg = z_ref[i]
        col0 = GLA_KP if reverse else 0
        gpre = _dot(zg[:, 2 * GLA_KP + 2 * GLA_W:Z_GLA], w2_ref[:, col0:col0 + GLA_KP]) \
            + bias_ref[:, col0:col0 + GLA_KP]
        b = _cumsum_rows(_log_sigmoid(gpre) * (1.0 / GLA_TAU), reverse)
        last = 0 if reverse else c - 1
        qs.append(zg[:, 0:GLA_KP] * (GLA_DK ** -0.5))
        ks.append(zg[:, GLA_KP:2 * GLA_KP])
        vs.append(zg[:, 2 * GLA_KP:2 * GLA_KP + GLA_W])
        bs.append(b)
        b_lasts.append(b[last:last + 1, :])
    q, k, v, b, b_last = (jnp.stack(x, axis=0) for x in (qs, ks, vs, bs, b_lasts))
    vb = v.astype(BF16)
    st = st_ref[...]
    inter = _bmm_nt((q * jnp.exp(b)).astype(BF16), st.astype(BF16))
    kv = _bmm_tn(vb, (k * jnp.exp(b_last - b)).astype(BF16))
    vi = lax.broadcasted_iota(jnp.int32, (GLA_W, GLA_KP), 0) // GLA_DV
    ki = lax.broadcasted_iota(jnp.int32, (GLA_W, GLA_KP), 1) // GLA_DK
    st_ref[...] = st * jnp.exp(b_last) + jnp.where((vi == ki)[None], kv, 0.0)

    lane_head = lax.broadcasted_iota(jnp.int32, (1, 1, GLA_KP), 2) // GLA_DK
    per_head = [[None] * nsub for _ in range(GLA_HEADS)]
    for i in range(nsub):
        r0 = i * GLA_SUB
        b_ref = jnp.stack([b[p, r0:r0 + 1] if p % 2 == 0 else b[p, r0 + GLA_SUB - 1:r0 + GLA_SUB]
                           for p in range(2 * nb)], axis=0)
        qi = q[:, r0:r0 + GLA_SUB] * jnp.exp(b[:, r0:r0 + GLA_SUB] - b_ref)
        ki_ = (k * jnp.exp(jnp.minimum(b_ref - b, GLA_EXP_CLAMP))).astype(BF16)
        qh = jnp.concatenate([jnp.where(lane_head == h, qi, 0.0) for h in range(GLA_HEADS)], axis=1)
        a = _bmm_nt(qh.astype(BF16), ki_)
        for h in range(GLA_HEADS):
            per_head[h][i] = a[:, h * GLA_SUB:(h + 1) * GLA_SUB]
    causal = jnp.stack([_tri(c, False, False), _tri(c, True, False)] * nb, axis=0)
    lane = lax.broadcasted_iota(jnp.int32, (1, 1, LANES), 2)
    pieces = []
    for p in range(GLA_HEADS // 2):
        vp = vb[:, :, p * LANES:(p + 1) * LANES]
        halves = []
        for hh in range(2):
            a_h = jnp.where(causal, jnp.concatenate(per_head[2 * p + hh], axis=1), 0.0)
            halves.append(_bmm(a_h.astype(BF16), vp))
        pieces.append(jnp.where(lane < GLA_DV, halves[0], halves[1]))
    o = inter + jnp.concatenate(pieces, axis=2)
    for i in range(nb):
        of_ref[i] = o[2 * i]
        ob_ref[i] = o[2 * i + 1]


def _gla_scan(zg, w2p, biasp, n_ctx_chunks):
    bsz, lt, _ = zg.shape
    nc = lt // CHUNK
    nb = SCAN_BATCH if bsz % SCAN_BATCH == 0 else 1
    fwd = lambda b_, s: (b_, s, 0)
    bwd = lambda b_, s: (b_, _bwd_chunk(s, n_ctx_chunks, nc), 0)
    return pl.pallas_call(
        _gla_kernel,
        out_shape=(jax.ShapeDtypeStruct((bsz, lt, GLA_W), F32),
                   jax.ShapeDtypeStruct((bsz, lt, GLA_W), F32)),
        grid=(bsz // nb, nc),
        in_specs=[pl.BlockSpec((nb, CHUNK, Z_GLA), fwd),
                  pl.BlockSpec((nb, CHUNK, Z_GLA), bwd),
                  pl.BlockSpec((LANES, 2 * GLA_KP), lambda b_, s: (0, 0)),
                  pl.BlockSpec((1, 2 * GLA_KP), lambda b_, s: (0, 0))],
        out_specs=(pl.BlockSpec((nb, CHUNK, GLA_W), fwd),
                   pl.BlockSpec((nb, CHUNK, GLA_W), bwd)),
        scratch_shapes=[pltpu.VMEM((2 * nb, GLA_W, GLA_KP), F32)],
        compiler_params=_cparams(("arbitrary", "arbitrary")),
        name="gla_scan",
    )(zg, zg, w2p, biasp)


def _pack_gla_gate(w2, bias):
    w = jnp.zeros((LANES, 2 * GLA_KP), F32)
    w = w.at[0:GLA_RANK, 0:GLA_KW].set(w2[0]).at[GLA_RANK:2 * GLA_RANK, GLA_KP:GLA_KP + GLA_KW].set(w2[1])
    b = jnp.zeros((1, 2 * GLA_KP), F32)
    b = b.at[0, 0:GLA_KW].set(bias[0]).at[0, GLA_KP:GLA_KP + GLA_KW].set(bias[1])
    return w, b


def _rwprep_kernel(n_ctx_tiles, n_tiles,
                   z_ref, zp_ref, zn_ref, cw_ref, w2_ref, a2_ref, g2_ref, vec_ref, gm_ref,
                   sh_ref, df_ref, db_ref, g_ref, bonus_ref):
    i = pl.program_id(1)
    z = z_ref[0]
    x = z[:, 0:3 * RW_W]
    seg_first = (i == 0) | (i == n_ctx_tiles)
    seg_last = (i == n_ctx_tiles - 1) | (i == n_tiles - 1)
    prev_row = jnp.where(seg_first, 0.0, zp_ref[0, 7:8, 0:3 * RW_W])
    next_row = jnp.where(seg_last, 0.0, zn_ref[0, 0:1, 0:3 * RW_W])
    ridx = lax.broadcasted_iota(jnp.int32, (ROW_TILE, 1), 0)
    x_prev = jnp.where(ridx == 0, prev_row, pltpu.roll(x, 1, 0))
    x_next = jnp.where(ridx == ROW_TILE - 1, next_row, pltpu.roll(x, ROW_TILE - 1, 0))
    xc = x_prev * cw_ref[0:1, :] + x * cw_ref[1:2, :] + x_next * cw_ref[2:3, :]
    r = xc[:, 0:RW_W]
    k = xc[:, RW_W:2 * RW_W]
    v = xc[:, 2 * RW_W:3 * RW_W]
    gm = gm_ref[...]
    kk = k * vec_ref[0:1, :]
    kk = kk / jnp.maximum(jnp.sqrt(_group_sum(kk * kk, gm)), 1e-12)
    k_a = vec_ref[1:2, :]
    r_k = vec_ref[2:3, :]
    w_raw = _dot3(jnp.tanh(z[:, 3 * RW_W:3 * RW_W + LANES]), w2_ref[...])
    a_raw = _dot3(z[:, 3 * RW_W + LANES:3 * RW_W + 2 * LANES], a2_ref[...])
    g_ref[0] = _dot3(jax.nn.sigmoid(z[:, 3 * RW_W + 2 * LANES:Z_RW]), g2_ref[...])
    sh_ref[0, :, 0:RW_W] = r
    sh_ref[0, :, RW_W:2 * RW_W] = v
    sh_ref[0, :, 2 * RW_W:3 * RW_W] = kk
    rk_sum = None
    for d, d_ref in enumerate((df_ref, db_ref)):
        wr = w_raw[:, d * RW_W:(d + 1) * RW_W] + vec_ref[3 + d:4 + d, :]
        logw = -math.exp(-0.5) * jax.nn.sigmoid(wr)
        a = jax.nn.sigmoid(a_raw[:, d * RW_W:(d + 1) * RW_W] + vec_ref[5 + d:6 + d, :])
        k_mod = k * (1.0 + (a - 1.0) * k_a)
        d_ref[0, :, 0:RW_W] = logw
        d_ref[0, :, RW_W:2 * RW_W] = kk * a
        d_ref[0, :, 2 * RW_W:3 * RW_W] = k_mod
        s = _group_sum(r * k_mod * r_k, gm)
        rk_sum = s if d == 0 else rk_sum + s
    bonus_ref[0] = rk_sum * v


def _rw_prep(zr, cw, w2p, a2p, g2, vec, n_ctx_tiles):
    bsz, lt, _ = zr.shape
    nt = lt // ROW_TILE
    hb = ROW_TILE // 8
    row = lambda b_, i: (b_, i, 0)
    full = lambda shape: pl.BlockSpec(shape, lambda b_, i: tuple(0 for _ in shape))
    o3 = jax.ShapeDtypeStruct((bsz, lt, 3 * RW_W), F32)
    o1 = jax.ShapeDtypeStruct((bsz, lt, RW_W), F32)
    return pl.pallas_call(
        functools.partial(_rwprep_kernel, n_ctx_tiles, nt),
        out_shape=(o3, o3, o3, o1, o1),
        grid=(bsz, nt),
        in_specs=[pl.BlockSpec((1, ROW_TILE, Z_RW), row),
                  pl.BlockSpec((1, 8, Z_RW), lambda b_, i: (b_, jnp.maximum(i * hb - 1, 0), 0)),
                  pl.BlockSpec((1, 8, Z_RW), lambda b_, i: (b_, jnp.minimum((i + 1) * hb, nt * hb - 1), 0)),
                  full((3, 3 * RW_W)), full((LANES, 2 * RW_W)), full((LANES, 2 * RW_W)),
                  full((RW_GATE_RANK, RW_W)), full((8, RW_W)), full((RW_W, RW_W))],
        out_specs=(pl.BlockSpec((1, ROW_TILE, 3 * RW_W), row),
                   pl.BlockSpec((1, ROW_TILE, 3 * RW_W), row),
                   pl.BlockSpec((1, ROW_TILE, 3 * RW_W), row),
                   pl.BlockSpec((1, ROW_TILE, RW_W), row),
                   pl.BlockSpec((1, ROW_TILE, RW_W), row)),
        compiler_params=_cparams(("arbitrary", "arbitrary")),
        name="rwkv_prep",
    )(zr, zr, zr, cw, w2p, a2p, g2, vec, _group_matrix(RW_W, RW_D))


def _block2(w):
    r, n = w.shape[1:]
    z = jnp.zeros((r, n), w.dtype)
    return jnp.concatenate([jnp.concatenate([w[0], z], 1), jnp.concatenate([z, w[1]], 1)], 0)


def _bmm(a, b):
    return lax.dot_general(a, b, (((2,), (1,)), ((0,), (0,))), preferred_element_type=F32)


def _bmm_nt(a, b):
    return lax.dot_general(a, b, (((2,), (2,)), ((0,), (0,))), preferred_element_type=F32)


def _bmm_tn(a, b):
    return lax.dot_general(a, b, (((1,), (1,)), ((0,), (0,))), preferred_element_type=F32)


def _rw_operands(sh, dd, reverse):
    c = CHUNK
    r, v, kk = sh[:, 0:RW_W], sh[:, RW_W:2 * RW_W], sh[:, 2 * RW_W:3 * RW_W]
    logw, beta, k = dd[:, 0:RW_W], dd[:, RW_W:2 * RW_W], dd[:, 2 * RW_W:3 * RW_W]
    b = _cumsum_rows(logw, reverse)
    last = 0 if reverse else c - 1
    e_b = jnp.exp(b)
    e_nb = jnp.exp(-b)
    e_last = jnp.exp(b[last:last + 1, :])
    e_tot = e_last * e_nb
    abar = (kk * jnp.exp(b - logw)).astype(BF16)
    rbar = (r * e_b).astype(BF16)
    kt = (k * e_nb).astype(BF16)
    bt = (beta * e_nb).astype(BF16)
    khat = (k * e_tot).astype(BF16)
    nbhat = (-(beta * e_tot)).astype(BF16)
    vb = v.astype(BF16)
    head_a = lax.broadcasted_iota(jnp.int32, (1, LANES), 1) < RW_D
    zero = jnp.zeros((c, LANES), BF16)

    def stack(*xs):
        rows = []
        for x in xs:
            rows += [jnp.where(head_a, x, zero), jnp.where(head_a, zero, x)]
        return jnp.concatenate(rows, axis=0)

    out = []
    for p in range(RW_HEADS // 2):
        sl = slice(p * LANES, (p + 1) * LANES)
        out.append(dict(xar=stack(abar[:, sl], rbar[:, sl]), yb=stack(bt[:, sl]), yk=stack(kt[:, sl]),
                        vs=stack(vb[:, sl]), kb=stack(khat[:, sl], nbhat[:, sl]),
                        e_col=jnp.broadcast_to(e_last[:, sl], (LANES, LANES)).T))
    return out


def _rw_kernel(sf_ref, df_ref, sb_ref, db_ref, yf_ref, yb_ref, h_ref):
    @pl.when(pl.program_id(1) == 0)
    def _():
        h_ref[...] = jnp.zeros_like(h_ref)

    c = CHUNK
    c2 = 2 * c
    npair = RW_HEADS // 2
    nb = sf_ref.shape[0]
    ops = []
    for i in range(nb):
        ops += _rw_operands(sf_ref[i], df_ref[i], False) + _rw_operands(sb_ref[i], db_ref[i], True)
    cat = lambda name: jnp.stack([o[name] for o in ops], axis=0)
    xar, yb_, yk, vs, kb, e_col = (cat(n) for n in ("xar", "yb", "yk", "vs", "kb", "e_col"))

    ti = lax.broadcasted_iota(jnp.int32, (c2, c2), 0)
    si = lax.broadcasted_iota(jnp.int32, (c2, c2), 1)
    same_head = (ti // c) == (si // c)
    both = lambda fwd, bwd: jnp.concatenate([jnp.broadcast_to(fwd[None], (npair, c2, c2)),
                                             jnp.broadcast_to(bwd[None], (npair, c2, c2))] * nb, axis=0)
    strict = both(same_head & (si < ti), same_head & (si > ti))
    incl = both(same_head & (si <= ti), same_head & (si >= ti))
    eye = jnp.where(ti == si, 1.0, 0.0).astype(F32)

    gb = _bmm_nt(xar, yb_)
    gk = _bmm_nt(xar, yk)
    l_ab = jnp.where(strict, gb[:, 0:c2], 0.0)
    l_rb = jnp.where(incl, gb[:, c2:2 * c2], 0.0).astype(BF16)
    l_ak = jnp.where(strict, gk[:, 0:c2], 0.0)
    l_rk = jnp.where(incl, gk[:, c2:2 * c2], 0.0)
    t_inv = None
    s = 1
    while s < c:
        same = (ti // (2 * s)) == (si // (2 * s))
        lo, hi = (ti // s) % 2, (si // s) % 2
        off = both(same & (lo == 1) & (hi == 0), same & (lo == 0) & (hi == 1))
        l_off = jnp.where(off, l_ab, 0.0)
        if t_inv is None:
            t_inv = eye[None] - l_off
        else:
            tb = t_inv.astype(BF16)
            t_inv = t_inv - _bmm(tb, _bmm(l_off.astype(BF16), tb).astype(BF16))
        s *= 2
    h0 = h_ref[...]
    xh = _bmm(xar, h0.astype(BF16))
    lv = _bmm(jnp.concatenate([l_ak, l_rk], axis=1).astype(BF16), vs)
    ub = _bmm(t_inv.astype(BF16), (xh[:, 0:c2] + lv[:, 0:c2]).astype(BF16)).astype(BF16)
    y2 = xh[:, c2:2 * c2] + lv[:, c2:2 * c2] - _bmm(l_rb, ub)
    y = y2[:, 0:c] + y2[:, c:c2]
    h_ref[...] = e_col * h0 + _bmm_tn(kb, jnp.concatenate([vs, ub], axis=1))
    for i in range(nb):
        o = 2 * npair * i
        yf_ref[i] = jnp.concatenate([y[o + p] for p in range(npair)], axis=1)
        yb_ref[i] = jnp.concatenate([y[o + npair + p] for p in range(npair)], axis=1)


def _rw_scan(shared, dfw, dbw, n_ctx_chunks):
    bsz, lt, _ = shared.shape
    nc = lt // CHUNK
    nb = SCAN_BATCH if bsz % SCAN_BATCH == 0 else 1
    fwd = lambda b_, s: (b_, s, 0)
    bwd = lambda b_, s: (b_, _bwd_chunk(s, n_ctx_chunks, nc), 0)
    return pl.pallas_call(
        _rw_kernel,
        out_shape=(jax.ShapeDtypeStruct((bsz, lt, RW_W), F32),
                   jax.ShapeDtypeStruct((bsz, lt, RW_W), F32)),
        grid=(bsz // nb, nc),
        in_specs=[pl.BlockSpec((nb, CHUNK, 3 * RW_W), fwd),
                  pl.BlockSpec((nb, CHUNK, 3 * RW_W), fwd),
                  pl.BlockSpec((nb, CHUNK, 3 * RW_W), bwd),
                  pl.BlockSpec((nb, CHUNK, 3 * RW_W), bwd)],
        out_specs=(pl.BlockSpec((nb, CHUNK, RW_W), fwd),
                   pl.BlockSpec((nb, CHUNK, RW_W), bwd)),
        scratch_shapes=[pltpu.VMEM((nb * RW_HEADS, LANES, LANES), F32)],
        compiler_params=_cparams(("arbitrary", "arbitrary")),
        name="rwkv_scan",
    )(shared, dfw, shared, dbw)


def _rank_kernel(n, ti_ref, dest_ref, meta_ref, cnt_ref, run_ref, start_ref):
    ph = pl.program_id(0)
    i = pl.program_id(1)
    ti = ti_ref[...]
    lane = lax.broadcasted_iota(jnp.int32, (ROW_TILE, LANES), 1)
    ohs = [jnp.where(ti[:, j:j + 1] == lane, 1.0, 0.0).astype(F32) for j in range(TOP_K)]
    oh = ohs[0] + ohs[1] + ohs[2] + ohs[3]
    tile_cnt = jnp.sum(oh, axis=0, keepdims=True)

    @pl.when((ph == 0) & (i == 0))
    def _():
        cnt_ref[...] = jnp.zeros_like(cnt_ref)
        run_ref[...] = jnp.zeros_like(run_ref)

    @pl.when(ph == 0)
    def _():
        cnt_ref[...] += tile_cnt

    @pl.when((ph == 0) & (i == n - 1))
    def _():
        cnt = cnt_ref[...]
        shift = MOE_BLOCK.bit_length() - 1
        padded = jnp.left_shift(jnp.right_shift(cnt.astype(jnp.int32) + (MOE_BLOCK - 1), shift),
                                shift).astype(F32)
        e0 = lax.broadcasted_iota(jnp.int32, (LANES, LANES), 0)
        e1 = lax.broadcasted_iota(jnp.int32, (LANES, LANES), 1)
        before = jnp.where(e0 < e1, 1.0, 0.0).astype(F32)
        start = _dot_hi(jnp.broadcast_to(padded, (8, LANES)), before)[0:1]
        start_ref[...] = start
        meta_ref[0:1, :] = cnt
        meta_ref[1:2, :] = start
        meta_ref[2:3, :] = padded
        meta_ref[3:8, :] = jnp.zeros((5, LANES), F32)

    @pl.when(ph == 1)
    def _():
        t0 = lax.broadcasted_iota(jnp.int32, (ROW_TILE, ROW_TILE), 0)
        t1 = lax.broadcasted_iota(jnp.int32, (ROW_TILE, ROW_TILE), 1)
        earlier = jnp.where(t1 < t0, 1.0, 0.0).astype(BF16)
        pos = (jnp.dot(earlier, oh.astype(BF16), preferred_element_type=F32)
               + run_ref[...] + start_ref[...])
        dest = jnp.zeros((ROW_TILE, LANES), F32)
        for j in range(TOP_K):
            dj = jnp.sum(ohs[j] * pos, axis=-1, keepdims=True)
            dest = jnp.where(lane == j, dj, dest)
        dest_ref[0] = dest.T[0:8, :].astype(jnp.int32)
        run_ref[...] += tile_cnt


def _moe_rank(ti):
    n = ti.shape[0]
    nt = n // ROW_TILE
    return pl.pallas_call(
        functools.partial(_rank_kernel, nt),
        out_shape=(jax.ShapeDtypeStruct((nt, 8, ROW_TILE), jnp.int32),
                   jax.ShapeDtypeStruct((8, LANES), F32)),
        grid=(2, nt),
        in_specs=[pl.BlockSpec((ROW_TILE, LANES), lambda p, i: (i, 0))],
        out_specs=(pl.BlockSpec((1, 8, ROW_TILE), lambda p, i: (i * p, 0, 0)),
                   pl.BlockSpec((8, LANES), lambda p, i: (0, 0))),
        scratch_shapes=[pltpu.VMEM((1, LANES), F32), pltpu.VMEM((1, LANES), F32),
                        pltpu.VMEM((1, LANES), F32)],
        compiler_params=_cparams(("arbitrary", "arbitrary")),
        name="moe_rank",
    )(ti)


MOE_TOK_BITS = 15


def _invert_kernel(n_tiles, plane_rows, lo_ref, hi_ref, dest_hbm, code_ref, idx_s, sem):
    s = pl.program_id(0)

    def fetch(i, slot):
        return pltpu.make_async_copy(dest_hbm.at[i], idx_s.at[slot], sem.at[slot])

    @pl.when(s < N_EXPERTS)
    def _():
        def fill(p, c):
            code_ref[p] = jnp.left_shift(TOP_K * plane_rows + jnp.bitwise_and(p, 2 * MOE_BLOCK - 1), MOE_TOK_BITS)
            return c

        lax.fori_loop(lo_ref[s], hi_ref[s], fill, 0)

    @pl.when(s == N_EXPERTS)
    def _():
        fetch(0, 0).start()

    @pl.when(s >= N_EXPERTS)
    def _():
        i = s - N_EXPERTS
        slot = i % 2

        @pl.when(i + 1 < n_tiles)
        def _():
            fetch(i + 1, 1 - slot).start()

        fetch(i, slot).wait()

        def tok(t, c):
            token = i * ROW_TILE + t
            for j in range(TOP_K):
                code_ref[idx_s[slot, 0, j * ROW_TILE + t]] = jnp.bitwise_or(
                    jnp.left_shift(j * plane_rows + token, MOE_TOK_BITS), token)
            return c

        lax.fori_loop(0, ROW_TILE, tok, 0, unroll=8)


def _moe_invert(dest2d, n_blocks, plane_rows, pad_lo, pad_hi):
    nt = dest2d.shape[0]
    assert nt * ROW_TILE < 2 ** MOE_TOK_BITS and TOP_K * plane_rows + 2 * MOE_BLOCK <= 2 ** (32 - MOE_TOK_BITS)
    return pl.pallas_call(
        functools.partial(_invert_kernel, nt, plane_rows),
        out_shape=jax.ShapeDtypeStruct((n_blocks * MOE_BLOCK,), jnp.int32),
        grid_spec=pltpu.PrefetchScalarGridSpec(
            num_scalar_prefetch=2,
            grid=(N_EXPERTS + nt,),
            in_specs=[pl.BlockSpec(memory_space=pl.ANY)],
            out_specs=pl.BlockSpec(memory_space=pltpu.SMEM),
            scratch_shapes=[pltpu.SMEM((2, 1, 8 * ROW_TILE), jnp.int32), pltpu.SemaphoreType.DMA((2,))]),
        compiler_params=_cparams(("arbitrary",)),
        name="moe_invert",
    )(pad_lo, pad_hi, dest2d.reshape(nt, 1, 8 * ROW_TILE))


def _fused_expert_kernel(plane_rows,
                         be_ref, nu_ref, code_hbm, m_hbm, wg_ref, bg_ref, wu_ref, bu_ref, wd_ref, bd_ref,
                         out_hbm, wg_s, wu_s, wd_s, xbuf0, xbuf1, ybuf0, ybuf1, idx_s, sem_i, sem_g, sem_s):
    b = pl.program_id(0)
    nu = nu_ref[0]
    blk = MOE_BLOCK
    xbufs = (xbuf0, xbuf1)
    ybufs = (ybuf0, ybuf1)

    def idx_copy(block, slot):
        return pltpu.make_async_copy(code_hbm.at[block], idx_s.at[slot], sem_i.at[slot])

    def gather(slot, r, buf):
        tok = jnp.bitwise_and(idx_s[slot, 0, r], 2 ** MOE_TOK_BITS - 1)
        return pltpu.make_async_copy(m_hbm.at[tok], xbufs[buf].at[pl.ds(r * SUBLANES, SUBLANES), :], sem_g.at[buf])

    def scatter(slot, r, buf):
        row = lax.shift_right_logical(idx_s[slot, 0, r], MOE_TOK_BITS)
        return pltpu.make_async_copy(ybufs[buf].at[pl.ds(r * SUBLANES, SUBLANES), :], out_hbm.at[row], sem_s.at[buf])

    def wait_rows(buf, sem):
        pltpu.make_async_copy(m_hbm.at[pl.ds(0, blk)], xbufs[buf].reshape(blk, SUBLANES, LANES), sem.at[buf]).wait()

    def step(cur):
        oth = 1 - cur
        s_prev, s_cur, s_next, s_far = (b + 3) % 4, b % 4, (b + 1) % 4, (b + 2) % 4
        idx_copy(jnp.minimum(b + 2, nu - 1), s_far).start()
        idx_copy(0, s_next).wait()
        for r in range(blk):
            scatter(s_prev, r, oth).start()
        for r in range(blk):
            gather(s_next, r, oth).start()
        wait_rows(cur, sem_g)
        nch = wg_s.shape[0] // LANES
        x = jnp.concatenate([xbufs[cur][pl.ds(c, blk, stride=SUBLANES), :] for c in range(nch)],
                            axis=1).astype(BF16)
        gt = jnp.minimum(jnp.dot(x, wg_s[...], preferred_element_type=F32) + bg_ref[0, 0], SWIGLU_LIMIT)
        up = jnp.clip(jnp.dot(x, wu_s[...], preferred_element_type=F32) + bu_ref[0, 0],
                      -SWIGLU_LIMIT, SWIGLU_LIMIT)
        act = (up + 1.0) * gt * jax.nn.sigmoid(SWIGLU_ALPHA * gt)
        y = jnp.dot(act.astype(BF16), wd_s[...], preferred_element_type=F32) + bd_ref[0, 0]
        for c in range(nch):
            ybufs[cur][pl.ds(c, blk, stride=SUBLANES), :] = y[:, c * LANES:(c + 1) * LANES]
        wait_rows(oth, sem_s)

        @pl.when(b == nu - 1)
        def _():
            wait_rows(oth, sem_g)
            idx_copy(0, s_far).wait()

            def last(r, c):
                scatter(s_cur, r, cur).start()
                return c

            lax.fori_loop(0, blk, last, 0, unroll=8)
            wait_rows(cur, sem_s)

    @pl.when(b < nu)
    def _():
        e = be_ref[b]
        changed = (b == 0) | (e != be_ref[jnp.maximum(b - 1, 0)])

        @pl.when(changed)
        def _():
            wg_s[...] = wg_ref[0, 0].astype(BF16)
            wu_s[...] = wu_ref[0, 0].astype(BF16)
            wd_s[...] = wd_ref[0, 0].astype(BF16)

        @pl.when(b == 0)
        def _():
            ybuf0[...] = jnp.zeros_like(ybuf0)
            ybuf1[...] = jnp.zeros_like(ybuf1)
            cp = idx_copy(0, 0)
            cp.start()
            cp.wait()
            idx_copy(jnp.minimum(1, nu - 1), 1).start()

            def spare(r, c):
                pltpu.make_async_copy(ybuf0.at[pl.ds(r * SUBLANES, SUBLANES), :],
                                      out_hbm.at[TOP_K * plane_rows + r], sem_s.at[0]).start()
                idx_s[3, 0, r] = jnp.left_shift(TOP_K * plane_rows + blk + r, MOE_TOK_BITS)
                return c

            lax.fori_loop(0, blk, spare, 0, unroll=8)
            wait_rows(0, sem_s)

            def first(r, c):
                gather(0, r, 0).start()
                return c

            lax.fori_loop(0, blk, first, 0, unroll=8)

        @pl.when(b % 2 == 0)
        def _():
            step(0)

        @pl.when(b % 2 == 1)
        def _():
            step(1)


def _moe_experts_fused(layer, block_expert, n_used, codes, m, plane_rows, wg, bg, wu, bu, wd, bd):
    nch = m.shape[1]
    d = nch * LANES
    nl, ne, _, f = wg.shape
    nb = codes.shape[0]
    wmap = lambda b, be, nu: (layer, be[jnp.maximum(jnp.minimum(b, nu[0] - 1), 0)], 0, 0)
    any_spec = pl.BlockSpec(memory_space=pl.ANY)
    return pl.pallas_call(
        functools.partial(_fused_expert_kernel, plane_rows),
        out_shape=jax.ShapeDtypeStruct((TOP_K * plane_rows + 2 * MOE_BLOCK, nch, LANES), F32),
        grid_spec=pltpu.PrefetchScalarGridSpec(
            num_scalar_prefetch=2,
            grid=(nb,),
            in_specs=[any_spec, any_spec,
                      pl.BlockSpec((1, 1, d, f), wmap), pl.BlockSpec((1, 1, 1, f), wmap),
                      pl.BlockSpec((1, 1, d, f), wmap), pl.BlockSpec((1, 1, 1, f), wmap),
                      pl.BlockSpec((1, 1, f, d), wmap), pl.BlockSpec((1, 1, 1, d), wmap)],
            out_specs=any_spec,
            scratch_shapes=[pltpu.VMEM((d, f), BF16), pltpu.VMEM((d, f), BF16), pltpu.VMEM((f, d), BF16),
                            pltpu.VMEM((MOE_BLOCK * nch, LANES), F32), pltpu.VMEM((MOE_BLOCK * nch, LANES), F32),
                            pltpu.VMEM((MOE_BLOCK * nch, LANES), F32), pltpu.VMEM((MOE_BLOCK * nch, LANES), F32),
                            pltpu.SMEM((4, 1, MOE_BLOCK), jnp.int32),
                            pltpu.SemaphoreType.DMA((4,)), pltpu.SemaphoreType.DMA((2,)),
                            pltpu.SemaphoreType.DMA((2,))]),
        compiler_params=_cparams(("arbitrary",)),
        name="moe_experts",
    )(block_expert, n_used, codes.reshape(nb, 1, MOE_BLOCK), m, wg, bg.reshape(nl, ne, 1, f),
      wu, bu.reshape(nl, ne, 1, f), wd, bd.reshape(nl, ne, 1, d))


def _combine2_kernel(alpha, gt_ref, h_ref, mod_ref, vec_ref, y0_ref, y1_ref, y2_ref, y3_ref, o_ref):
    gt = gt_ref[0]
    d = h_ref.shape[2]
    xs = []
    for c in range(d // LANES):
        rows = pl.ds(c, ROW_TILE, stride=SUBLANES)
        f = (gt[:, 0:1] * y0_ref[rows, :] + gt[:, 1:2] * y1_ref[rows, :]
             + gt[:, 2:3] * y2_ref[rows, :] + gt[:, 3:4] * y3_ref[rows, :])
        cols = slice(c * LANES, (c + 1) * LANES)
        xs.append(alpha * h_ref[0, :, cols] + mod_ref[0, 0, 0:1, cols] * f)
    mu = sum(jnp.sum(x, axis=-1, keepdims=True) for x in xs) * (1.0 / d)
    xs = [x - mu for x in xs]
    var = sum(jnp.sum(x * x, axis=-1, keepdims=True) for x in xs) * (1.0 / d)
    inv = lax.rsqrt(var + LN_EPS)
    for c, x in enumerate(xs):
        cols = slice(c * LANES, (c + 1) * LANES)
        o_ref[0, :, cols] = x * inv * vec_ref[0:1, cols] + vec_ref[1:2, cols]


def _moe_combine2(alpha, n_ctx_tiles, tile0, plane_rows, gates, h1, mod3, vec, out4):
    bsz, rows, d = h1.shape
    nch = d // LANES
    nt = rows // ROW_TILE
    pt = plane_rows // ROW_TILE
    row = lambda b_, i: (b_, i, 0)
    plane = lambda j: pl.BlockSpec((ROW_TILE * nch, LANES), lambda b_, i: (j * pt + b_ * nt + i, 0))
    out4 = out4.reshape((TOP_K * plane_rows + 2 * MOE_BLOCK) * nch, LANES)
    return pl.pallas_call(
        functools.partial(_combine2_kernel, alpha),
        out_shape=jax.ShapeDtypeStruct((bsz, rows, d), F32),
        grid=(bsz, nt),
        in_specs=[pl.BlockSpec((1, ROW_TILE, LANES), row),
                  pl.BlockSpec((1, ROW_TILE, d), row),
                  pl.BlockSpec((1, 1, 1, d), lambda b_, i: (b_, jnp.where(tile0 + i < n_ctx_tiles, 0, 1), 0, 0)),
                  pl.BlockSpec((2, d), lambda b_, i: (0, 0)),
                  plane(0), plane(1), plane(2), plane(3)],
        out_specs=pl.BlockSpec((1, ROW_TILE, d), row),
        compiler_params=_cparams(("arbitrary", "arbitrary")),
        name="moe_combine",
    )(gates, h1, mod3, vec, out4, out4, out4, out4)


def _moe(layer, alpha, n_ctx_tiles, tile0, h1, m, ti, gates, mod3, ln2, wg, bg, wu, bu, wd, bd):
    bsz, rows, d = h1.shape
    n = bsz * rows
    dest, meta = _moe_rank(ti.reshape(n, LANES))
    dest2d = dest.reshape(n // ROW_TILE, 8 * ROW_TILE)
    n_blocks = -(-(n * TOP_K) // MOE_BLOCK) + N_EXPERTS
    pad_end = (meta[1, :N_EXPERTS] + meta[2, :N_EXPERTS]).astype(jnp.int32)
    block_row = jnp.arange(n_blocks, dtype=jnp.int32) * MOE_BLOCK
    block_expert = jnp.minimum(jnp.sum((pad_end[None, :] <= block_row[:, None]).astype(jnp.int32), axis=1),
                               N_EXPERTS - 1)
    n_used = (pad_end[-1:] // MOE_BLOCK).astype(jnp.int32)
    plane_rows = n
    pad_lo = (meta[1, :N_EXPERTS] + meta[0, :N_EXPERTS]).astype(jnp.int32)
    pad_hi = jnp.concatenate([pad_end[:-1], jnp.full((1,), n_blocks * MOE_BLOCK, jnp.int32)])
    codes = _moe_invert(dest2d, n_blocks, plane_rows, pad_lo, pad_hi).reshape(n_blocks, MOE_BLOCK)
    out4 = _moe_experts_fused(layer, block_expert, n_used, codes, m.reshape(n, d // LANES, LANES), plane_rows,
                              wg, bg, wu, bu, wd, bd)
    return _moe_combine2(alpha, n_ctx_tiles, tile0, plane_rows, gates, h1, mod3, ln2, out4)


def kernel(x, c, ctx, c_ctx, ln_in_g, ln_in_b, ada_w, ada_b, w_in, lam_q1, lam_k1, lam_q2, lam_k2,
           da_subln_g, gla_gate_w2, gla_gate_b, gla_norm_g, rw_conv_w, rw_w2, rw_w0, rw_a2, rw_a0,
           rw_g2, rw_k_k, rw_k_a, rw_r_k, rw_lnx_g, rw_lnx_b, w_out, ln1_g, ln1_b, router_w, router_b,
           moe_w_gate, moe_b_gate, moe_w_up, moe_b_up, moe_w_down, moe_b_down, ln2_g, ln2_b):
    bsz, seq, d = x.shape
    n_ctx = ctx.shape[1]
    depth = w_in.shape[0]
    assert n_ctx % ROW_TILE == 0 and seq % ROW_TILE == 0 and seq % GRID_W == 0
    assert w_in.shape[2] == 3488 and bsz + 1 <= 8
    nct = n_ctx // ROW_TILE
    ncc = n_ctx // CHUNK
    nt = (n_ctx + seq) // ROW_TILE
    alpha = (2 * depth) ** 0.25

    c_all = jnp.concatenate([c, c_ctx[None], jnp.zeros((8 - bsz - 1, d), F32)], axis=0)
    mods = _ada_mod(c_all, ada_w, ada_b).reshape(depth, 8, 6, d)
    h = _ln_in(ctx, x, ln_in_g, ln_in_b)
    cos_t, sin_t = _rope_tables(n_ctx, seq)

    def pick(l, idx):
        mc = jnp.broadcast_to(mods[l, bsz][None, idx], (bsz, len(idx), d))
        return jnp.stack([mc, mods[l, :bsz][:, idx]], axis=1)

    for l in range(depth):
        last = l == depth - 1
        tile0 = nct if last else 0
        n_out_tiles = nt - tile0
        lam_init = 0.8 - 0.6 * math.exp(-0.3 * l)
        lam = (jnp.exp(jnp.sum(lam_q1[l] * lam_k1[l])) - jnp.exp(jnp.sum(lam_q2[l] * lam_k2[l])) + lam_init)

        qk, v, zg, zr = _inproj(h, pick(l, [1, 0]), _pack_w_in(w_in[l]), cos_t, sin_t, nct)
        ao = _attention(qk, v, lam, nct, nt - nct, n_ctx + seq)
        if not last:
            ao = jnp.concatenate([_attention(qk, v, lam, 0, nct, n_ctx), ao], axis=1)
        w2p, biasp = _pack_gla_gate(gla_gate_w2[l], gla_gate_b[l])
        gof, gob = _gla_scan(zg, w2p, biasp, ncc)
        rvec = jnp.stack([rw_k_k[l], rw_k_a[l], rw_r_k[l].reshape(-1), rw_w0[l, 0], rw_w0[l, 1],
                          rw_a0[l, 0], rw_a0[l, 1], jnp.zeros((RW_W,), F32)], axis=0)
        shared, dfw, dbw, rg, bonus = _rw_prep(zr, rw_conv_w[l], _block2(rw_w2[l]), _block2(rw_a2[l]),
                                               rw_g2[l], rvec, nct)
        yf, yb = _rw_scan(shared, dfw, dbw, ncc)

        nrm = jnp.stack([jnp.pad(jnp.tile(da_subln_g[l], DA_HEADS), (0, GLA_W - DA_W)),
                         jnp.tile(gla_norm_g[l], GLA_HEADS), rw_lnx_g[l], rw_lnx_b[l]], axis=0)
        rw_f = jnp.pad(router_w[l], ((0, 0), (0, LANES - N_EXPERTS)))
        rw_hi = rw_f.astype(BF16)
        rw_p = jnp.stack([rw_hi, (rw_f - rw_hi.astype(F32)).astype(BF16)], axis=0)
        rb_p = jnp.pad(router_b[l], (0, LANES - N_EXPERTS), constant_values=-1e30).reshape(1, LANES)
        h1, m, ti, gates = _mixout(alpha, 1.0 - lam_init, tile0, n_out_tiles, nct,
                                   ao, gof, gob, zg, yf, yb, bonus, rg, h, pick(l, [2, 4, 3]),
                                   w_out[l].astype(BF16), jnp.stack([ln1_g[l], ln1_b[l]], 0), nrm, rw_p, rb_p)
        h = _moe(l, alpha, nct, tile0, h1, m, ti, gates, pick(l, [5]), jnp.stack([ln2_g[l], ln2_b[l]], 0),
                 moe_w_gate, moe_b_gate, moe_w_up, moe_b_up, moe_w_down, moe_b_down)
    return h
```
